```python
import jax, jax.numpy as jnp
from jax import lax
import numpy as np

D_MODEL = 1024
BATCH = 4
SEQ = 4096
DEPTH = 2
DEC_BATCH = 8
DEC_SEQ = 16
PAST_LEN = 4096

CHUNK = 64
HEAD_DIM = 64
N_Q_HEADS = 8
N_KV_HEADS = 2
Q_PER_KV = N_Q_HEADS // N_KV_HEADS
D_ATTN = N_Q_HEADS * HEAD_DIM
D_KV = N_KV_HEADS * HEAD_DIM
WINDOW = 128
WINDOW_CHUNKS = WINDOW // CHUNK
ROT_DIM = HEAD_DIM // 4
ROPE_THETA = 500000.0
POOL_WINDOWS = (2, 4, 8, 16)
D_POOL = 512
POOL_GROUP = D_POOL // len(POOL_WINDOWS)
POOL_STATE = max(POOL_WINDOWS) - 1
D_MIX = D_ATTN + D_POOL
D_IN = D_ATTN + 2 * D_KV + D_POOL
D_FF = 2816
CONV_W = 3
EPS = 1e-6
NEG_INF = -1e30

kernel_name = "hybrid_swa_sink_pool_convffn_stream_step"


def rms_norm(x, g):
    xf = x.astype(jnp.float32)
    y = xf * lax.rsqrt(jnp.mean(xf * xf, axis=-1, keepdims=True) + EPS)
    return (y * g.astype(jnp.float32)).astype(x.dtype)


def partial_rope(x, pos):
    inv = ROPE_THETA ** (-jnp.arange(0, ROT_DIM, 2, dtype=jnp.float32) / ROT_DIM)
    ang = pos.astype(jnp.float32)[:, None] * inv[None, :]
    cos = jnp.cos(ang)[:, None, :].astype(x.dtype)
    sin = jnp.sin(ang)[:, None, :].astype(x.dtype)
    half = ROT_DIM // 2
    x1 = x[..., :half]
    x2 = x[..., half:ROT_DIM]
    return jnp.concatenate([x1 * cos - x2 * sin, x2 * cos + x1 * sin, x[..., ROT_DIM:]], axis=-1)


def mixer_inputs(xn, pos, w_in, q_gain, k_gain):
    b, l = xn.shape[0], xn.shape[1]
    h = xn @ w_in
    q, k, v, u = jnp.split(h, [D_ATTN, D_ATTN + D_KV, D_ATTN + 2 * D_KV], axis=-1)
    q = q.reshape(b, l, N_Q_HEADS, HEAD_DIM)
    k = k.reshape(b, l, N_KV_HEADS, HEAD_DIM)
    v = v.reshape(b, l, N_KV_HEADS, HEAD_DIM)
    q = partial_rope(rms_norm(q, q_gain), pos)
    k = partial_rope(rms_norm(k, k_gain), pos)
    return q, k, v, u


def sink_attention(q, k, v, sinks, mask):
    lead = q.shape[:-3]
    sq = q.shape[-3]
    qg = q.reshape(*lead, sq, N_KV_HEADS, Q_PER_KV, HEAD_DIM)
    s = jnp.einsum('...qkgd,...skd->...kgqs', qg, k,
                   preferred_element_type=jnp.float32) * (HEAD_DIM ** -0.5)
    if mask is not None:
        s = jnp.where(mask, s, NEG_INF)
    sink = sinks.astype(jnp.float32).reshape(N_KV_HEADS, Q_PER_KV, 1, 1)
    m = jnp.maximum(jnp.max(s, axis=-1, keepdims=True), sink)
    e = jnp.exp(s - m)
    p = e / (jnp.sum(e, axis=-1, keepdims=True) + jnp.exp(sink - m))
    o = jnp.einsum('...kgqs,...skd->...qkgd', p.astype(v.dtype), v)
    return o.reshape(*lead, sq, D_ATTN)


def window_attention_prompt(q, k, v, sinks):
    b, l = q.shape[0], q.shape[1]
    nc = l // CHUNK
    qc = q.reshape(b, nc, CHUNK, N_Q_HEADS, HEAD_DIM)

    def band(t):
        tc = t.reshape(b, nc, CHUNK, N_KV_HEADS, HEAD_DIM)
        tp = jnp.pad(tc, ((0, 0), (WINDOW_CHUNKS, 0), (0, 0), (0, 0), (0, 0)))
        return jnp.concatenate([tp[:, j:j + nc] for j in range(WINDOW_CHUNKS + 1)], axis=2)

    kb = band(k)
    vb = band(v)
    rel = jnp.repeat(jnp.arange(WINDOW_CHUNKS + 1) - WINDOW_CHUNKS, CHUNK)
    key_chunk = jnp.arange(nc)[:, None] + rel[None, :]
    mask = (key_chunk >= 0)[:, None, None, None, :]
    o = sink_attention(qc, kb, vb, sinks, mask)
    return o.reshape(b, l, D_ATTN)


def pool_mixer(u, prev, w_pool, scale):
    full = u if prev is None else jnp.concatenate([prev, u], axis=1)
    l = u.shape[1]
    p0 = full.shape[1] - l
    m = max(POOL_WINDOWS)
    cs = jnp.pad(jnp.cumsum(full.astype(jnp.float32), axis=1), ((0, 0), (m + 1, 0), (0, 0)))
    end = jnp.arange(p0, p0 + l) + 1
    hi = cs[:, m + p0 + 1:m + p0 + l + 1]
    uf = u.astype(jnp.float32)
    outs = []
    for g, w in enumerate(POOL_WINDOWS):
        sl = slice(g * POOL_GROUP, (g + 1) * POOL_GROUP)
        lo = cs[:, m + p0 + 1 - w:m + p0 + l + 1 - w, sl]
        cnt = jnp.minimum(end, w).astype(jnp.float32)[None, :, None]
        d = ((hi[..., sl] - lo) / cnt - uf[..., sl]).astype(u.dtype)
        outs.append(d @ w_pool[g])
    return jnp.concatenate(outs, axis=-1) * scale, full[:, -POOL_STATE:]


def conv_ffn(x, prev, w_up, conv_w, conv_b, w_down):
    h = x @ w_up
    l = h.shape[1]
    if prev is None:
        full = jnp.pad(h, ((0, 0), (CONV_W - 1, 0), (0, 0)))
    else:
        full = jnp.concatenate([prev, h], axis=1)
    c = conv_b
    for j in range(CONV_W):
        c = c + full[:, j:j + l] * conv_w[j]
    gate, val = jnp.split(c, 2, axis=-1)
    y = (jax.nn.silu(gate) * val) @ w_down
    return y, full[:, -(CONV_W - 1):]


def trunk_layer(x, pos, cache_k, cache_v, pool_prev, conv_prev,
                norm_mix, w_in, q_gain, k_gain, sinks, w_pool, pool_scale, w_out,
                norm_ffn, w_up, conv_w, conv_b, w_down):
    xn = rms_norm(x, norm_mix)
    q, k, v, u = mixer_inputs(xn, pos, w_in, q_gain, k_gain)
    if cache_k is None:
        attn = window_attention_prompt(q, k, v, sinks)
        keys, vals = k, v
    else:
        keys = jnp.concatenate([cache_k, k], axis=1)
        vals = jnp.concatenate([cache_v, v], axis=1)
        attn = sink_attention(q, keys, vals, sinks, None)
    pool, new_pool = pool_mixer(u, pool_prev, w_pool, pool_scale)
    x = x + jnp.concatenate([attn, pool], axis=-1) @ w_out
    f, new_conv = conv_ffn(rms_norm(x, norm_ffn), conv_prev, w_up, conv_w, conv_b, w_down)
    x = x + f
    return x, keys[:, -WINDOW:], vals[:, -WINDOW:], new_pool, new_conv


def setup_inputs(seed: int = 0) -> dict:
    key = jax.random.key(seed)
    ks = jax.random.split(key, 20)
    f32 = jnp.float32
    win_buf = min(WINDOW, PAST_LEN)
    nrm = lambda k, shape, s: jax.random.normal(k, shape, f32) * s
    return {
        'x_prompt': nrm(ks[0], (BATCH, SEQ, D_MODEL), 1.0),
        'x_sample': nrm(ks[1], (DEC_BATCH, DEC_SEQ, D_MODEL), 1.0),
        'cache_k': nrm(ks[2], (DEPTH, DEC_BATCH, win_buf, N_KV_HEADS, HEAD_DIM), 1.0),
        'cache_v': nrm(ks[3], (DEPTH, DEC_BATCH, win_buf, N_KV_HEADS, HEAD_DIM), 1.0),
        'state_pool': nrm(ks[4], (DEPTH, DEC_BATCH, POOL_STATE, D_POOL), 1.0),
        'state_conv': nrm(ks[5], (DEPTH, DEC_BATCH, CONV_W - 1, 2 * D_FF), 1.0),
        'norm_mix': 1.0 + nrm(ks[6], (DEPTH, D_MODEL), 0.05),
        'w_in': nrm(ks[7], (DEPTH, D_MODEL, D_IN), D_MODEL ** -0.5),
        'q_norm': 1.0 + nrm(ks[8], (DEPTH, HEAD_DIM), 0.05),
        'k_norm': 1.0 + nrm(ks[9], (DEPTH, HEAD_DIM), 0.05),
        'attn_sinks': nrm(ks[10], (DEPTH, N_Q_HEADS), 0.5),
        'w_pool': nrm(ks[11], (DEPTH, len(POOL_WINDOWS), POOL_GROUP, POOL_GROUP), POOL_GROUP ** -0.5),
        'pool_scale': 1.0 + nrm(ks[12], (DEPTH, D_POOL), 0.1),
        'w_out': nrm(ks[13], (DEPTH, D_MIX, D_MODEL), D_MIX ** -0.5),
        'norm_ffn': 1.0 + nrm(ks[14], (DEPTH, D_MODEL), 0.05),
        'w_up': nrm(ks[15], (DEPTH, D_MODEL, 2 * D_FF), D_MODEL ** -0.5),
        'conv_w': nrm(ks[16], (DEPTH, CONV_W, 2 * D_FF), CONV_W ** -0.5),
        'conv_b': nrm(ks[17], (DEPTH, 2 * D_FF), 0.02),
        'w_down': nrm(ks[18], (DEPTH, D_FF, D_MODEL), D_FF ** -0.5),
    }


def reference(x_prompt, x_sample, cache_k, cache_v, state_pool, state_conv,
              norm_mix, w_in, q_norm, k_norm, attn_sinks, w_pool, pool_scale, w_out,
              norm_ffn, w_up, conv_w, conv_b, w_down):
    pos_p = jnp.arange(x_prompt.shape[1])
    pos_s = PAST_LEN + jnp.arange(x_sample.shape[1])
    yp, ys = x_prompt, x_sample
    kp, vp, pp, cp = [], [], [], []
    kq, vq, pq, cq = [], [], [], []
    for i in range(DEPTH):
        w = (norm_mix[i], w_in[i], q_norm[i], k_norm[i], attn_sinks[i], w_pool[i], pool_scale[i],
             w_out[i], norm_ffn[i], w_up[i], conv_w[i], conv_b[i], w_down[i])
        yp, k1, v1, p1, c1 = trunk_layer(yp, pos_p, None, None, None, None, *w)
        ys, k2, v2, p2, c2 = trunk_layer(ys, pos_s, cache_k[i], cache_v[i], state_pool[i], state_conv[i], *w)
        kp.append(k1); vp.append(v1); pp.append(p1); cp.append(c1)
        kq.append(k2); vq.append(v2); pq.append(p2); cq.append(c2)
    return (yp, ys,
            jnp.stack(kp), jnp.stack(vp), jnp.stack(pp), jnp.stack(cp),
            jnp.stack(kq), jnp.stack(vq), jnp.stack(pq), jnp.stack(cq))
```

```python
import functools

import jax
import jax.numpy as jnp
from jax import lax
from jax.experimental import pallas as pl
from jax.experimental.pallas import tpu as pltpu

F32 = jnp.float32
BF16 = jnp.bfloat16

CHUNK = 64
HEAD_DIM = 64
N_Q_HEADS = 8
N_KV_HEADS = 2
Q_PER_KV = N_Q_HEADS // N_KV_HEADS
D_ATTN = N_Q_HEADS * HEAD_DIM
D_KV = N_KV_HEADS * HEAD_DIM
WINDOW = 128
ROT_DIM = 16
ROPE_THETA = 500000.0
POOL_WINDOWS = (2, 4, 8, 16)
POOL_GROUP = 128
D_POOL = POOL_GROUP * len(POOL_WINDOWS)
POOL_HALO = 16
CONV_W = 3
CONV_HALO = 8
EPS = 1e-6
NEG_INF = -1e30

LANES = 128
MXU_COLS = 256

PROMPT_TILE = 512
FF_CHUNK = 256
VMEM_LIMIT_BYTES = 56 * 1024 * 1024


def _dot(a, b):
    return jnp.dot(a, b, preferred_element_type=F32)


def _dot_nt(a, b):
    return lax.dot_general(a, b, (((1,), (1,)), ((), ())), preferred_element_type=F32)


def _rms_rows(x, gain):
    ms = jnp.mean(x * x, axis=-1, keepdims=True)
    return x * lax.rsqrt(ms + EPS) * gain


def _head_rms(t, bd, gain):
    ss = _dot((t * t).astype(BF16), bd)
    return t * lax.rsqrt(ss * (1.0 / HEAD_DIM) + EPS) * gain


def _rope(t, cos, sin_a, sin_b):
    return (t * cos + pltpu.roll(t, LANES - ROT_DIM // 2, 1) * sin_a
            + pltpu.roll(t, ROT_DIM // 2, 1) * sin_b)


def _mixer_inputs(x, gmix, w_in, qg, kg, bd, cos, sin_a, sin_b):
    xn = _rms_rows(x, gmix).astype(BF16)
    h = _dot(xn, w_in)
    q_parts = []
    for j in range(D_ATTN // MXU_COLS):
        qb = _head_rms(h[:, j * MXU_COLS:(j + 1) * MXU_COLS], bd,
                       qg[:, j * MXU_COLS:(j + 1) * MXU_COLS])
        for l in range(MXU_COLS // LANES):
            q_parts.append(_rope(qb[:, l * LANES:(l + 1) * LANES], cos, sin_a, sin_b))
    q = (jnp.concatenate(q_parts, axis=1) * (HEAD_DIM ** -0.5)).astype(BF16)
    k = _head_rms(h[:, D_ATTN:D_ATTN + D_KV], bd[:D_KV, :D_KV], kg)
    k = _rope(k, cos, sin_a, sin_b)
    v = h[:, D_ATTN + D_KV:D_ATTN + 2 * D_KV]
    u = h[:, D_ATTN + 2 * D_KV:]
    return q, k, v, u


def _sink_softmax_pv(s, sink_col, vb):
    m = jnp.maximum(jnp.max(s, axis=-1, keepdims=True), sink_col)
    e = jnp.exp(s - m)
    den = jnp.sum(e, axis=-1, keepdims=True) + jnp.exp(sink_col - m)
    return _dot(e.astype(BF16), vb) * (1.0 / den)


def _store_heads(mix_buf, rows, g, o, n_rows):
    for p in range(Q_PER_KV // 2):
        pair = jnp.concatenate(
            [o[(2 * p) * n_rows:(2 * p + 1) * n_rows], o[(2 * p + 1) * n_rows:(2 * p + 2) * n_rows]],
            axis=1)
        c0 = (g * Q_PER_KV + 2 * p) * HEAD_DIM
        mix_buf[rows, c0:c0 + 2 * HEAD_DIM] = pair.astype(BF16)


def _window_sum(a, w):
    s = 1
    while s < w:
        a = a + pltpu.roll(a, s, 0)
        s *= 2
    return a


def _conv_gate(hg, hv, prev_g, prev_v, cw_g, cw_v, cb_g, cb_v, shift_fn):
    g1, g2 = shift_fn(hg, prev_g)
    v1, v2 = shift_fn(hv, prev_v)
    cg = cb_g + g2 * cw_g[0:1] + g1 * cw_g[1:2] + hg * cw_g[2:3]
    cv = cb_v + v2 * cw_v[0:1] + v1 * cw_v[1:2] + hv * cw_v[2:3]
    return ((cg / (1.0 + jnp.exp(-cg))) * cv).astype(BF16)


def _prompt_kernel(sinks_ref, x_ref, cos_ref, sa_ref, sb_ref,
                   gmix_ref, win_ref, qg_ref, kg_ref, bd_ref, wpool_ref, pscale_ref, wout_ref,
                   gffn_ref, wup_ref, cw_ref, cb_ref, wdown_ref,
                   y_ref, klast_ref, vlast_ref, ulast_ref, hlast_ref,
                   kfull, vfull, ufull, hprev, mix_buf):
    i = pl.program_id(1)
    T = x_ref.shape[0]
    d_ff = wdown_ref.shape[0]
    n_chunks = T // CHUNK

    @pl.when(i == 0)
    def _():
        kfull[:, 0:WINDOW, :] = jnp.zeros((N_KV_HEADS, WINDOW, HEAD_DIM), BF16)
        vfull[:, 0:WINDOW, :] = jnp.zeros((N_KV_HEADS, WINDOW, HEAD_DIM), BF16)
        ufull[0:POOL_HALO, :] = jnp.zeros((POOL_HALO, D_POOL), F32)
        hprev[...] = jnp.zeros(hprev.shape, F32)

    x = x_ref[...]
    q, k, v, u = _mixer_inputs(x, gmix_ref[...], win_ref[...], qg_ref[...], kg_ref[...],
                               bd_ref[...], cos_ref[...], sa_ref[...], sb_ref[...])
    klast_ref[...] = k[T - WINDOW:, :]
    vlast_ref[...] = v[T - WINDOW:, :]
    ulast_ref[...] = u[T - POOL_HALO:, :]
    kb16 = k.astype(BF16)
    vb16 = v.astype(BF16)
    for g in range(N_KV_HEADS):
        kfull[g, WINDOW:WINDOW + T, :] = kb16[:, g * HEAD_DIM:(g + 1) * HEAD_DIM]
        vfull[g, WINDOW:WINDOW + T, :] = vb16[:, g * HEAD_DIM:(g + 1) * HEAD_DIM]
    ufull[POOL_HALO:POOL_HALO + T, :] = u

    band = WINDOW + CHUNK
    key_idx = lax.broadcasted_iota(jnp.int32, (1, band), 1)
    for g in range(N_KV_HEADS):
        sink_col = jnp.concatenate(
            [jnp.full((CHUNK, 1), sinks_ref[g * Q_PER_KV + hh], F32) for hh in range(Q_PER_KV)],
            axis=0)
        for j in range(n_chunks):
            r0 = j * CHUNK
            qs = jnp.concatenate(
                [q[r0:r0 + CHUNK, (g * Q_PER_KV + hh) * HEAD_DIM:(g * Q_PER_KV + hh + 1) * HEAD_DIM]
                 for hh in range(Q_PER_KV)], axis=0)
            kb = kfull[g, r0:r0 + band, :]
            vb = vfull[g, r0:r0 + band, :]
            s = _dot_nt(qs, kb)
            if r0 < WINDOW:
                first_valid = jnp.where(i == 0, WINDOW - r0, 0)
                s = jnp.where(key_idx >= first_valid, s, NEG_INF)
            o = _sink_softmax_pv(s, sink_col, vb)
            _store_heads(mix_buf, slice(r0, r0 + CHUNK), g, o, CHUNK)

    uf = ufull[...]
    pos = i * T + lax.broadcasted_iota(jnp.int32, (T, 1), 0)
    for gi, w in enumerate(POOL_WINDOWS):
        sl = slice(gi * POOL_GROUP, (gi + 1) * POOL_GROUP)
        tsum = _window_sum(uf[:, sl], w)[POOL_HALO:]
        cnt = jnp.minimum(pos + 1, w).astype(F32)
        d = (tsum / cnt - u[:, sl]).astype(BF16)
        pool = _dot(d, wpool_ref[gi]) * pscale_ref[:, sl]
        mix_buf[:, D_ATTN + gi * POOL_GROUP:D_ATTN + (gi + 1) * POOL_GROUP] = pool.astype(BF16)

    for g in range(N_KV_HEADS):
        kfull[g, 0:WINDOW, :] = kfull[g, T:T + WINDOW, :]
        vfull[g, 0:WINDOW, :] = vfull[g, T:T + WINDOW, :]
    ufull[0:POOL_HALO, :] = u[T - POOL_HALO:, :]

    x1 = x + _dot(mix_buf[...], wout_ref[...])

    xn2 = _rms_rows(x1, gffn_ref[...]).astype(BF16)

    def shift_fn(hc, prev):
        full = jnp.concatenate([prev, hc], axis=0)
        return (pltpu.roll(full, 1, 0)[CONV_HALO:], pltpu.roll(full, 2, 0)[CONV_HALO:])

    acc = x1
    for c in range(d_ff // FF_CHUNK):
        cg = slice(c * FF_CHUNK, (c + 1) * FF_CHUNK)
        cv = slice(d_ff + c * FF_CHUNK, d_ff + (c + 1) * FF_CHUNK)
        hg = _dot(xn2, wup_ref[:, cg])
        hv = _dot(xn2, wup_ref[:, cv])
        act = _conv_gate(hg, hv, hprev[:, cg], hprev[:, cv], cw_ref[:, cg], cw_ref[:, cv],
                         cb_ref[:, cg], cb_ref[:, cv], shift_fn)
        hlast_ref[:, cg] = hg[T - CONV_HALO:, :]
        hlast_ref[:, cv] = hv[T - CONV_HALO:, :]
        hprev[:, cg] = hg[T - CONV_HALO:, :]
        hprev[:, cv] = hv[T - CONV_HALO:, :]
        acc = acc + _dot(act, wdown_ref[cg, :])
    y_ref[...] = acc


def _const_spec(shape, single_buffer=False):
    nd = len(shape)
    idx = lambda *_: (0,) * nd
    if single_buffer:
        return pl.BlockSpec(shape, idx, pipeline_mode=pl.Buffered(1))
    return pl.BlockSpec(shape, idx)


def _prompt_layer(x, tabs, w):
    B, L, D = x.shape
    T = PROMPT_TILE
    d_ff = w["w_down"].shape[0]
    n_tiles = L // T
    tile_spec = pl.BlockSpec((None, T, D), lambda b, i: (b, i, 0))
    tab_spec = pl.BlockSpec((T, LANES), lambda b, i: (i, 0))

    def last_spec(rows, cols):
        return pl.BlockSpec((None, rows, cols), lambda b, i: (b, 0, 0))

    in_specs = [
        pl.BlockSpec(memory_space=pltpu.SMEM),
        tile_spec, tab_spec, tab_spec, tab_spec,
        _const_spec((1, D)), _const_spec(w["w_in"].shape, True),
        _const_spec((1, D_ATTN)), _const_spec((1, D_KV)), _const_spec((MXU_COLS, MXU_COLS)),
        _const_spec(w["w_pool"].shape), _const_spec((1, D_POOL)), _const_spec(w["w_out"].shape, True),
        _const_spec((1, D)), _const_spec(w["w_up"].shape, True),
        _const_spec((CONV_W, 2 * d_ff)), _const_spec((1, 2 * d_ff)), _const_spec(w["w_down"].shape, True),
    ]
    out_shape = (
        jax.ShapeDtypeStruct((B, L, D), F32),
        jax.ShapeDtypeStruct((B, WINDOW, D_KV), F32),
        jax.ShapeDtypeStruct((B, WINDOW, D_KV), F32),
        jax.ShapeDtypeStruct((B, POOL_HALO, D_POOL), F32),
        jax.ShapeDtypeStruct((B, CONV_HALO, 2 * d_ff), F32),
    )
    out_specs = (tile_spec, last_spec(WINDOW, D_KV), last_spec(WINDOW, D_KV),
                 last_spec(POOL_HALO, D_POOL), last_spec(CONV_HALO, 2 * d_ff))
    scratch = [
        pltpu.VMEM((N_KV_HEADS, WINDOW + T, HEAD_DIM), BF16),
        pltpu.VMEM((N_KV_HEADS, WINDOW + T, HEAD_DIM), BF16),
        pltpu.VMEM((POOL_HALO + T, D_POOL), F32),
        pltpu.VMEM((CONV_HALO, 2 * d_ff), F32),
        pltpu.VMEM((T, D_ATTN + D_POOL), BF16),
    ]
    return pl.pallas_call(
        _prompt_kernel,
        out_shape=out_shape,
        grid=(B, n_tiles),
        in_specs=in_specs,
        out_specs=out_specs,
        scratch_shapes=scratch,
        name="prompt_layer",
        compiler_params=pltpu.CompilerParams(
            dimension_semantics=("arbitrary", "arbitrary"),
            vmem_limit_bytes=VMEM_LIMIT_BYTES),
    )(w["sinks"], x, tabs[0], tabs[1], tabs[2],
      w["g_mix"], w["w_in"], w["q_gain"], w["k_gain"], w["bd"], w["w_pool"], w["pool_scale"],
      w["w_out"], w["g_ffn"], w["w_up"], w["conv_w"], w["conv_b"], w["w_down"])


def _sample_kernel(sinks_ref, x_ref, cos_ref, sa_ref, sb_ref, ck_ref, cv_ref, pprev_ref,
                   c1_ref, c2_ref,
                   gmix_ref, win_ref, qg_ref, kg_ref, bd_ref, wpool_ref, pscale_ref, wout_ref,
                   gffn_ref, wup_ref, cw_ref, cb_ref, wdown_ref,
                   y_ref, kout_ref, vout_ref, uout_ref, hout_ref,
                   ufull, mix_buf):
    M = x_ref.shape[0]
    n_streams = ck_ref.shape[0]
    n_new = M // n_streams
    past = ck_ref.shape[1]
    d_ff = wdown_ref.shape[0]

    x = x_ref[...]
    q, k, v, u = _mixer_inputs(x, gmix_ref[...], win_ref[...], qg_ref[...], kg_ref[...],
                               bd_ref[...], cos_ref[...], sa_ref[...], sb_ref[...])
    uout_ref[...] = u

    for s_ in range(n_streams):
        rows = slice(s_ * n_new, (s_ + 1) * n_new)
        keys = jnp.concatenate([ck_ref[s_], k[rows]], axis=0)
        vals = jnp.concatenate([cv_ref[s_], v[rows]], axis=0)
        kout_ref[s_] = keys[past + n_new - WINDOW:, :]
        vout_ref[s_] = vals[past + n_new - WINDOW:, :]
        keys16 = keys.astype(BF16)
        vals16 = vals.astype(BF16)
        for g in range(N_KV_HEADS):
            sink_col = jnp.concatenate(
                [jnp.full((n_new, 1), sinks_ref[g * Q_PER_KV + hh], F32) for hh in range(Q_PER_KV)],
                axis=0)
            qs = jnp.concatenate(
                [q[rows, (g * Q_PER_KV + hh) * HEAD_DIM:(g * Q_PER_KV + hh + 1) * HEAD_DIM]
                 for hh in range(Q_PER_KV)], axis=0)
            sc = _dot_nt(qs, keys16[:, g * HEAD_DIM:(g + 1) * HEAD_DIM])
            o = _sink_softmax_pv(sc, sink_col, vals16[:, g * HEAD_DIM:(g + 1) * HEAD_DIM])
            _store_heads(mix_buf, rows, g, o, n_new)
        ufull[s_ * 2 * n_new:s_ * 2 * n_new + n_new, :] = pprev_ref[s_]
        ufull[s_ * 2 * n_new + n_new:(s_ + 1) * 2 * n_new, :] = u[rows]

    uf = ufull[...]
    for gi, w in enumerate(POOL_WINDOWS):
        sl = slice(gi * POOL_GROUP, (gi + 1) * POOL_GROUP)
        ws = _window_sum(uf[:, sl], w)
        tsum = jnp.concatenate(
            [ws[s_ * 2 * n_new + n_new:(s_ + 1) * 2 * n_new] for s_ in range(n_streams)], axis=0)
        d = (tsum / float(w) - u[:, sl]).astype(BF16)
        pool = _dot(d, wpool_ref[gi]) * pscale_ref[:, sl]
        mix_buf[:, D_ATTN + gi * POOL_GROUP:D_ATTN + (gi + 1) * POOL_GROUP] = pool.astype(BF16)

    x1 = x + _dot(mix_buf[...], wout_ref[...])
    xn2 = _rms_rows(x1, gffn_ref[...]).astype(BF16)

    t_in_stream = lax.broadcasted_iota(jnp.int32, (M, 1), 0) % n_new

    def shift_fn(hc, prev):
        p1, p2 = prev
        return (jnp.where(t_in_stream >= 1, pltpu.roll(hc, 1, 0), p1),
                jnp.where(t_in_stream >= 2, pltpu.roll(hc, 2, 0), p2))

    acc = x1
    for c in range(d_ff // FF_CHUNK):
        cg = slice(c * FF_CHUNK, (c + 1) * FF_CHUNK)
        cv = slice(d_ff + c * FF_CHUNK, d_ff + (c + 1) * FF_CHUNK)
        hg = _dot(xn2, wup_ref[:, cg])
        hv = _dot(xn2, wup_ref[:, cv])
        hout_ref[:, cg] = hg
        hout_ref[:, cv] = hv
        act = _conv_gate(hg, hv, (c1_ref[:, cg], c2_ref[:, cg]), (c1_ref[:, cv], c2_ref[:, cv]),
                         cw_ref[:, cg], cw_ref[:, cv], cb_ref[:, cg], cb_ref[:, cv], shift_fn)
        acc = acc + _dot(act, wdown_ref[cg, :])
    y_ref[...] = acc


def _sample_layer(x, tabs, cache_k, cache_v, pool_prev, conv_prev, w):
    S, n_new, D = x.shape
    M = S * n_new
    d_ff = w["w_down"].shape[0]
    past = cache_k.shape[1]
    pprev = jnp.pad(pool_prev, ((0, 0), (n_new - pool_prev.shape[1], 0), (0, 0)))
    c1 = jnp.pad(conv_prev[:, 1:2], ((0, 0), (0, n_new - 1), (0, 0))).reshape(M, 2 * d_ff)
    c2 = jnp.pad(conv_prev, ((0, 0), (0, n_new - 2), (0, 0))).reshape(M, 2 * d_ff)
    vm = pl.BlockSpec(memory_space=pltpu.VMEM)
    out_shape = (
        jax.ShapeDtypeStruct((M, D), F32),
        jax.ShapeDtypeStruct((S, WINDOW, D_KV), F32),
        jax.ShapeDtypeStruct((S, WINDOW, D_KV), F32),
        jax.ShapeDtypeStruct((M, D_POOL), F32),
        jax.ShapeDtypeStruct((M, 2 * d_ff), F32),
    )
    y, ko, vo, uo, ho = pl.pallas_call(
        _sample_kernel,
        out_shape=out_shape,
        in_specs=[pl.BlockSpec(memory_space=pltpu.SMEM)] + [vm] * 22,
        out_specs=(vm,) * 5,
        scratch_shapes=[pltpu.VMEM((2 * M, D_POOL), F32), pltpu.VMEM((M, D_ATTN + D_POOL), BF16)],
        name="sample_layer",
        compiler_params=pltpu.CompilerParams(vmem_limit_bytes=VMEM_LIMIT_BYTES),
    )(w["sinks"], x.reshape(M, D), tabs[0], tabs[1], tabs[2],
      cache_k.reshape(S, past, D_KV), cache_v.reshape(S, past, D_KV), pprev, c1, c2,
      w["g_mix"], w["w_in"], w["q_gain"], w["k_gain"], w["bd"], w["w_pool"], w["pool_scale"],
      w["w_out"], w["g_ffn"], w["w_up"], w["conv_w"], w["conv_b"], w["w_down"])
    return (y.reshape(S, n_new, D), ko, vo, uo.reshape(S, n_new, D_POOL),
            ho.reshape(S, n_new, 2 * d_ff))


def _rope_tables(pos, reps=1):
    half = ROT_DIM // 2
    inv = ROPE_THETA ** (-jnp.arange(0, ROT_DIM, 2, dtype=F32) / ROT_DIM)
    ang = pos.astype(F32)[:, None] * inv[None, :]
    cos, sin = jnp.cos(ang), jnp.sin(ang)
    n = pos.shape[0]
    ones = jnp.ones((n, HEAD_DIM - ROT_DIM), F32)
    zeros_h = jnp.zeros((n, half), F32)
    zeros_r = jnp.zeros((n, HEAD_DIM - ROT_DIM), F32)
    c = jnp.concatenate([cos, cos, ones], axis=1)
    sa = jnp.concatenate([-sin, zeros_h, zeros_r], axis=1)
    sb = jnp.concatenate([zeros_h, sin, zeros_r], axis=1)
    tile = lambda t: jnp.tile(t, (reps, LANES // HEAD_DIM))
    return tile(c), tile(sa), tile(sb)


def kernel(x_prompt, x_sample, cache_k, cache_v, state_pool, state_conv, norm_mix, w_in, q_norm,
           k_norm, attn_sinks, w_pool, pool_scale, w_out, norm_ffn, w_up, conv_w, conv_b, w_down):
    depth = w_in.shape[0]
    B, L, D = x_prompt.shape
    S, n_new, _ = x_sample.shape
    past_len = L
    assert L % PROMPT_TILE == 0 and PROMPT_TILE % CHUNK == 0 and PROMPT_TILE >= WINDOW
    assert w_down.shape[1] % FF_CHUNK == 0

    tabs_p = _rope_tables(jnp.arange(L))
    tabs_s = _rope_tables(past_len + jnp.arange(n_new), reps=S)
    head_id = jnp.arange(MXU_COLS) // HEAD_DIM
    bd = (head_id[:, None] == head_id[None, :]).astype(BF16)

    yp, ys = x_prompt, x_sample
    outs = [[] for _ in range(8)]
    for i in range(depth):
        w = dict(
            sinks=attn_sinks[i],
            g_mix=norm_mix[i][None, :], w_in=w_in[i].astype(BF16),
            q_gain=jnp.tile(q_norm[i], N_Q_HEADS)[None, :], k_gain=jnp.tile(k_norm[i], N_KV_HEADS)[None, :],
            bd=bd, w_pool=w_pool[i].astype(BF16), pool_scale=pool_scale[i][None, :],
            w_out=w_out[i].astype(BF16), g_ffn=norm_ffn[i][None, :], w_up=w_up[i].astype(BF16),
            conv_w=conv_w[i], conv_b=conv_b[i][None, :], w_down=w_down[i].astype(BF16),
        )
        yp, k1, v1, u1, h1 = _prompt_layer(yp, tabs_p, w)
        ys, k2, v2, u2, h2 = _sample_layer(ys, tabs_s, cache_k[i], cache_v[i], state_pool[i],
                                           state_conv[i], w)
        n_pool = state_pool.shape[2]
        n_conv = state_conv.shape[2]
        outs[0].append(k1.reshape(B, WINDOW, N_KV_HEADS, HEAD_DIM))
        outs[1].append(v1.reshape(B, WINDOW, N_KV_HEADS, HEAD_DIM))
        outs[2].append(u1[:, POOL_HALO - n_pool:])
        outs[3].append(h1[:, CONV_HALO - n_conv:])
        outs[4].append(k2.reshape(S, WINDOW, N_KV_HEADS, HEAD_DIM))
        outs[5].append(v2.reshape(S, WINDOW, N_KV_HEADS, HEAD_DIM))
        outs[6].append(u2[:, n_new - n_pool:])
        outs[7].append(h2[:, n_new - n_conv:])
    return (yp, ys) + tuple(jnp.stack(o) for o in outs)
```

```python
import jax
import jax.numpy as jnp
from jax import lax
from jax.experimental import pallas as pl
from jax.experimental.pallas import tpu as pltpu

F32 = jnp.float32
BF16 = jnp.bfloat16

CHUNK = 64
HEAD_DIM = 64
N_Q_HEADS = 8
N_KV_HEADS = 2
Q_PER_KV = N_Q_HEADS // N_KV_HEADS
D_ATTN = N_Q_HEADS * HEAD_DIM
D_KV = N_KV_HEADS * HEAD_DIM
WINDOW = 128
ROT_DIM = 16
ROPE_THETA = 500000.0
POOL_WINDOWS = (2, 4, 8, 16)
POOL_GROUP = 128
D_POOL = POOL_GROUP * len(POOL_WINDOWS)
POOL_HALO = 16
CONV_W = 3
CONV_HALO = 8
EPS = 1e-6
NEG_INF = -1e30

LANES = 128
MXU_COLS = 256
ATT_KEYS = 256
VT_ROWS = D_KV + 16

PROMPT_TILE = 512
FF_CHUNK = 256
VMEM_LIMIT_BYTES = 56 * 1024 * 1024


def _dot(a, b):
    return jnp.dot(a, b, preferred_element_type=F32)


def _dot_nt(a, b):
    return lax.dot_general(a, b, (((1,), (1,)), ((), ())), preferred_element_type=F32)


def _dot_tn(a, b):
    return lax.dot_general(a, b, (((0,), (0,)), ((), ())), preferred_element_type=F32)


def _rms_rows(x, gain):
    ms = jnp.mean(x * x, axis=-1, keepdims=True)
    return x * lax.rsqrt(ms + EPS) * gain


def _head_rms(t, bd, gain):
    ss = _dot((t * t).astype(BF16), bd)
    return t * lax.rsqrt(ss * (1.0 / HEAD_DIM) + EPS) * gain


def _rope(t, cos, sin_a, sin_b):
    return (t * cos + pltpu.roll(t, LANES - ROT_DIM // 2, 1) * sin_a
            + pltpu.roll(t, ROT_DIM // 2, 1) * sin_b)


def _mixer_inputs(x, gmix, w_in, qg, kg, bd, cos, sin_a, sin_b):
    xn = _rms_rows(x, gmix).astype(BF16)
    h = _dot(xn, w_in)
    q_parts = []
    for j in range(D_ATTN // MXU_COLS):
        qb = _head_rms(h[:, j * MXU_COLS:(j + 1) * MXU_COLS], bd,
                       qg[:, j * MXU_COLS:(j + 1) * MXU_COLS])
        for l in range(MXU_COLS // LANES):
            q_parts.append(_rope(qb[:, l * LANES:(l + 1) * LANES], cos, sin_a, sin_b))
    q = jnp.concatenate(q_parts, axis=1) * (HEAD_DIM ** -0.5)
    k = _head_rms(h[:, D_ATTN:D_ATTN + D_KV], bd[:D_KV, :D_KV], kg)
    k = _rope(k, cos, sin_a, sin_b)
    v = h[:, D_ATTN + D_KV:D_ATTN + 2 * D_KV]
    u = h[:, D_ATTN + 2 * D_KV:]
    return q, k, v, u


def _window_sum(a, w):
    s = 1
    while s < w:
        a = a + pltpu.roll(a, s, 0)
        s *= 2
    return a


def _conv_gate(hg, hv, cw_g, cw_v, cb_g, cb_v):
    cg = cb_g + hg[2] * cw_g[0:1] + hg[1] * cw_g[1:2] + hg[0] * cw_g[2:3]
    cv = cb_v + hv[2] * cw_v[0:1] + hv[1] * cw_v[1:2] + hv[0] * cw_v[2:3]
    return ((cg / (1.0 + jnp.exp(-cg))) * cv).astype(BF16)


def _prompt_kernel(sinks_ref, x_ref, cos_ref, sa_ref, sb_ref,
                   gmix_ref, win_ref, qg_ref, kg_ref, bd_ref, eye_ref, wpool_ref, pscale_ref, wout_ref,
                   gffn_ref, wup_ref, cw_ref, cb_ref, wdown_ref,
                   y_ref, klast_ref, vlast_ref, ulast_ref, hlast_ref,
                   kfull, vfull, vt, ufull, hprev, hbuf, mix_buf, act_buf):
    i = pl.program_id(1)
    T = x_ref.shape[0]
    d_ff = wdown_ref.shape[0]
    n_chunks = T // CHUNK

    @pl.when(i == 0)
    def _():
        kfull[0:WINDOW, :] = jnp.zeros((WINDOW, D_KV), BF16)
        kfull[WINDOW + T:, :] = jnp.zeros((ATT_KEYS - WINDOW - CHUNK, D_KV), BF16)
        vfull[0:WINDOW, :] = jnp.zeros((WINDOW, D_KV), BF16)
        vfull[WINDOW + T:, :] = jnp.zeros((ATT_KEYS - WINDOW - CHUNK, D_KV), BF16)
        for c in range(2):
            vt[c, D_KV:, :] = jnp.ones((VT_ROWS - D_KV, vt.shape[2]), BF16)
        ufull[0:POOL_HALO, :] = jnp.zeros((POOL_HALO, D_POOL), F32)
        hprev[...] = jnp.zeros(hprev.shape, F32)

    x = x_ref[...]
    q, k, v, u = _mixer_inputs(x, gmix_ref[...], win_ref[...], qg_ref[...], kg_ref[...],
                               bd_ref[...], cos_ref[...], sa_ref[...], sb_ref[...])
    klast_ref[...] = k[T - WINDOW:, :]
    vlast_ref[...] = v[T - WINDOW:, :]
    ulast_ref[...] = u[T - POOL_HALO:, :]
    kfull[WINDOW:WINDOW + T, :] = k.astype(BF16)
    vfull[WINDOW:WINDOW + T, :] = v.astype(BF16)
    ufull[POOL_HALO:POOL_HALO + T, :] = u

    eye = eye_ref[...]
    for c in range(2):
        vrows = vfull[c * CHUNK:c * CHUNK + T + WINDOW, :]
        vt[c, 0:D_KV, :] = _dot_nt(eye[:D_KV, :D_KV], vrows).astype(BF16)
    band = WINDOW + CHUNK
    lane = lax.broadcasted_iota(jnp.int32, (1, D_ATTN), 1)
    head_of_lane = lax.broadcasted_iota(jnp.int32, (1, Q_PER_KV * CHUNK), 1) // CHUNK
    key_row = lax.broadcasted_iota(jnp.int32, (ATT_KEYS, 1), 0)
    q_of_group, sink_rows = [], []
    for g in range(N_KV_HEADS):
        in_g = ((lane // HEAD_DIM) % N_KV_HEADS == g).astype(F32)
        q_of_group.append((q * in_g).astype(BF16))
        row = jnp.full((1, Q_PER_KV * CHUNK), sinks_ref[g * Q_PER_KV], F32)
        for hh in range(1, Q_PER_KV):
            row = jnp.where(head_of_lane == hh, sinks_ref[g * Q_PER_KV + hh], row)
        sink_rows.append(row)
    for j in range(n_chunks):
        r0 = j * CHUNK
        kb = kfull[r0:r0 + ATT_KEYS, :]
        vtb = vt[j % 2, :, (j // 2) * LANES:(j // 2) * LANES + ATT_KEYS]
        first_valid = jnp.where(i == 0, WINDOW - r0, 0) if r0 < WINDOW else 0
        valid = (key_row >= first_valid) & (key_row < band)
        halves = []
        for g in range(N_KV_HEADS):
            qs = jnp.concatenate(
                [q_of_group[g][r0:r0 + CHUNK, hh * LANES:(hh + 1) * LANES]
                 for hh in range(Q_PER_KV)], axis=0)
            s = jnp.where(valid, _dot_nt(kb, qs), NEG_INF)
            m = jnp.maximum(jnp.max(s, axis=0, keepdims=True), sink_rows[g])
            e = jnp.exp(s - m).astype(BF16)
            oa = _dot(vtb, e)
            den = oa[D_KV:D_KV + 1] + jnp.exp(sink_rows[g] - m)
            halves.append(oa[g * HEAD_DIM:(g + 1) * HEAD_DIM] * (1.0 / den))
        ot = jnp.concatenate(halves, axis=0).astype(BF16)
        o = _dot_nt(eye, ot)
        for hh in range(Q_PER_KV):
            mix_buf[r0:r0 + CHUNK, hh * LANES:(hh + 1) * LANES] = (
                o[hh * CHUNK:(hh + 1) * CHUNK].astype(BF16))

    uf = ufull[...]
    pos = i * T + lax.broadcasted_iota(jnp.int32, (T, 1), 0)
    for gi, w in enumerate(POOL_WINDOWS):
        sl = slice(gi * POOL_GROUP, (gi + 1) * POOL_GROUP)
        tsum = _window_sum(uf[:, sl], w)[POOL_HALO:]
        cnt = jnp.minimum(pos + 1, w).astype(F32)
        d = (tsum / cnt - u[:, sl]).astype(BF16)
        pool = _dot(d, wpool_ref[gi]) * pscale_ref[:, sl]
        mix_buf[:, D_ATTN + gi * POOL_GROUP:D_ATTN + (gi + 1) * POOL_GROUP] = pool.astype(BF16)

    kfull[0:WINDOW, :] = kfull[T:T + WINDOW, :]
    vfull[0:WINDOW, :] = vfull[T:T + WINDOW, :]
    ufull[0:POOL_HALO, :] = u[T - POOL_HALO:, :]

    x1 = x + _dot(mix_buf[...], wout_ref[...])

    xn2 = _rms_rows(x1, gffn_ref[...]).astype(BF16)
    F = FF_CHUNK
    for c in range(d_ff // F):
        cols = (slice(c * F, (c + 1) * F), slice(d_ff + c * F, d_ff + (c + 1) * F))
        hb = hbuf.at[c % 2]
        taps = []
        for half, cs in enumerate(cols):
            hs = slice(half * F, (half + 1) * F)
            h = _dot(xn2, wup_ref[:, cs])
            hb[0:CONV_HALO, hs] = hprev[:, cs]
            hb[CONV_HALO:CONV_HALO + T, hs] = h
            hlast_ref[:, cs] = h[T - CONV_HALO:, :]
            hprev[:, cs] = h[T - CONV_HALO:, :]
            taps.append((h, hb[CONV_HALO - 1:CONV_HALO - 1 + T, hs],
                         hb[CONV_HALO - 2:CONV_HALO - 2 + T, hs]))
        act_buf[:, cols[0]] = _conv_gate(taps[0], taps[1], cw_ref[:, cols[0]], cw_ref[:, cols[1]],
                                         cb_ref[:, cols[0]], cb_ref[:, cols[1]])
    y_ref[...] = x1 + _dot(act_buf[...], wdown_ref[...])


def _const_spec(shape, single_buffer=False):
    nd = len(shape)
    idx = lambda *_: (0,) * nd
    if single_buffer:
        return pl.BlockSpec(shape, idx, pipeline_mode=pl.Buffered(1))
    return pl.BlockSpec(shape, idx)


def _prompt_layer(x, tabs, w):
    B, L, D = x.shape
    T = PROMPT_TILE
    d_ff = w["w_down"].shape[0]
    n_tiles = L // T
    tile_spec = pl.BlockSpec((None, T, D), lambda b, i: (b, i, 0))
    tab_spec = pl.BlockSpec((T, LANES), lambda b, i: (i, 0))

    def last_spec(rows, cols):
        return pl.BlockSpec((None, rows, cols), lambda b, i: (b, 0, 0))

    in_specs = [
        pl.BlockSpec(memory_space=pltpu.SMEM),
        tile_spec, tab_spec, tab_spec, tab_spec,
        _const_spec((1, D)), _const_spec(w["w_in"].shape, True),
        _const_spec((1, D_ATTN)), _const_spec((1, D_KV)), _const_spec((MXU_COLS, MXU_COLS)),
        _const_spec((MXU_COLS, MXU_COLS)),
        _const_spec(w["w_pool"].shape), _const_spec((1, D_POOL)), _const_spec(w["w_out"].shape, True),
        _const_spec((1, D)), _const_spec(w["w_up"].shape, True),
        _const_spec((CONV_W, 2 * d_ff)), _const_spec((1, 2 * d_ff)), _const_spec(w["w_down"].shape, True),
    ]
    out_shape = (
        jax.ShapeDtypeStruct((B, L, D), F32),
        jax.ShapeDtypeStruct((B, WINDOW, D_KV), F32),
        jax.ShapeDtypeStruct((B, WINDOW, D_KV), F32),
        jax.ShapeDtypeStruct((B, POOL_HALO, D_POOL), F32),
        jax.ShapeDtypeStruct((B, CONV_HALO, 2 * d_ff), F32),
    )
    out_specs = (tile_spec, last_spec(WINDOW, D_KV), last_spec(WINDOW, D_KV),
                 last_spec(POOL_HALO, D_POOL), last_spec(CONV_HALO, 2 * d_ff))
    scratch = [
        pltpu.VMEM((T + ATT_KEYS - CHUNK, D_KV), BF16),
        pltpu.VMEM((T + ATT_KEYS - CHUNK, D_KV), BF16),
        pltpu.VMEM((2, VT_ROWS, T + WINDOW), BF16),
        pltpu.VMEM((POOL_HALO + T, D_POOL), F32),
        pltpu.VMEM((CONV_HALO, 2 * d_ff), F32),
        pltpu.VMEM((2, CONV_HALO + T, 2 * FF_CHUNK), F32),
        pltpu.VMEM((T, D_ATTN + D_POOL), BF16),
        pltpu.VMEM((T, d_ff), BF16),
    ]
    return pl.pallas_call(
        _prompt_kernel,
        out_shape=out_shape,
        grid=(B, n_tiles),
        in_specs=in_specs,
        out_specs=out_specs,
        scratch_shapes=scratch,
        name="prompt_layer",
        compiler_params=pltpu.CompilerParams(
            dimension_semantics=("arbitrary", "arbitrary"),
            vmem_limit_bytes=VMEM_LIMIT_BYTES),
    )(w["sinks"], x, tabs[0], tabs[1], tabs[2],
      w["g_mix"], w["w_in"], w["q_gain"], w["k_gain"], w["bd"], w["eye"], w["w_pool"], w["pool_scale"],
      w["w_out"], w["g_ffn"], w["w_up"], w["conv_w"], w["conv_b"], w["w_down"])


def _sample_kernel(sinks_ref, x_ref, cos_ref, sa_ref, sb_ref, ck_ref, cv_ref, pprev_ref,
                   c1_ref, c2_ref,
                   gmix_ref, win_ref, qg_ref, kg_ref, bd_ref, wpool_ref, pscale_ref, wout_ref,
                   gffn_ref, wup_ref, cw_ref, cb_ref, wdown_ref,
                   y_ref, kout_ref, vout_ref, uout_ref, hout_ref,
                   ufull, mix_buf):
    M = x_ref.shape[0]
    n_streams = ck_ref.shape[0]
    n_new = M // n_streams
    past = ck_ref.shape[1]
    d_ff = wdown_ref.shape[0]

    x = x_ref[...]
    q, k, v, u = _mixer_inputs(x, gmix_ref[...], win_ref[...], qg_ref[...], kg_ref[...],
                               bd_ref[...], cos_ref[...], sa_ref[...], sb_ref[...])
    q = q.astype(BF16)
    uout_ref[...] = u

    for s_ in range(n_streams):
        rows = slice(s_ * n_new, (s_ + 1) * n_new)
        keys = jnp.concatenate([ck_ref[s_], k[rows]], axis=0)
        vals = jnp.concatenate([cv_ref[s_], v[rows]], axis=0)
        kout_ref[s_] = keys[past + n_new - WINDOW:, :]
        vout_ref[s_] = vals[past + n_new - WINDOW:, :]
        keys16 = keys.astype(BF16)
        vals16 = vals.astype(BF16)
        outs = []
        for g in range(N_KV_HEADS):
            sink_col = jnp.concatenate(
                [jnp.full((n_new, 1), sinks_ref[g * Q_PER_KV + hh], F32) for hh in range(Q_PER_KV)],
                axis=0)
            qs = jnp.concatenate(
                [q[rows, hh * LANES + g * HEAD_DIM:hh * LANES + (g + 1) * HEAD_DIM]
                 for hh in range(Q_PER_KV)], axis=0)
            sc = _dot_nt(qs, keys16[:, g * HEAD_DIM:(g + 1) * HEAD_DIM])
            m = jnp.maximum(jnp.max(sc, axis=-1, keepdims=True), sink_col)
            e = jnp.exp(sc - m)
            den = jnp.sum(e, axis=-1, keepdims=True) + jnp.exp(sink_col - m)
            outs.append(_dot(e.astype(BF16), vals16[:, g * HEAD_DIM:(g + 1) * HEAD_DIM]) * (1.0 / den))
        for hh in range(Q_PER_KV):
            pair = jnp.concatenate([o[hh * n_new:(hh + 1) * n_new] for o in outs], axis=1)
            mix_buf[rows, hh * LANES:(hh + 1) * LANES] = pair.astype(BF16)
        ufull[s_ * 2 * n_new:s_ * 2 * n_new + n_new, :] = pprev_ref[s_]
        ufull[s_ * 2 * n_new + n_new:(s_ + 1) * 2 * n_new, :] = u[rows]

    uf = ufull[...]
    for gi, w in enumerate(POOL_WINDOWS):
        sl = slice(gi * POOL_GROUP, (gi + 1) * POOL_GROUP)
        ws = _window_sum(uf[:, sl], w)
        tsum = jnp.concatenate(
            [ws[s_ * 2 * n_new + n_new:(s_ + 1) * 2 * n_new] for s_ in range(n_streams)], axis=0)
        d = (tsum / float(w) - u[:, sl]).astype(BF16)
        pool = _dot(d, wpool_ref[gi]) * pscale_ref[:, sl]
        mix_buf[:, D_ATTN + gi * POOL_GROUP:D_ATTN + (gi + 1) * POOL_GROUP] = pool.astype(BF16)

    x1 = x + _dot(mix_buf[...], wout_ref[...])
    xn2 = _rms_rows(x1, gffn_ref[...]).astype(BF16)

    t_in_stream = lax.broadcasted_iota(jnp.int32, (M, 1), 0) % n_new

    def taps(h, cs):
        return (h, jnp.where(t_in_stream >= 1, pltpu.roll(h, 1, 0), c1_ref[:, cs]),
                jnp.where(t_in_stream >= 2, pltpu.roll(h, 2, 0), c2_ref[:, cs]))

    acc = x1
    for c in range(d_ff // FF_CHUNK):
        cg = slice(c * FF_CHUNK, (c + 1) * FF_CHUNK)
        cv = slice(d_ff + c * FF_CHUNK, d_ff + (c + 1) * FF_CHUNK)
        hg = _dot(xn2, wup_ref[:, cg])
        hv = _dot(xn2, wup_ref[:, cv])
        hout_ref[:, cg] = hg
        hout_ref[:, cv] = hv
        act = _conv_gate(taps(hg, cg), taps(hv, cv), cw_ref[:, cg], cw_ref[:, cv],
                         cb_ref[:, cg], cb_ref[:, cv])
        acc = acc + _dot(act, wdown_ref[cg, :])
    y_ref[...] = acc


def _sample_layer(x, tabs, cache_k, cache_v, pool_prev, conv_prev, w):
    S, n_new, D = x.shape
    M = S * n_new
    d_ff = w["w_down"].shape[0]
    past = cache_k.shape[1]
    pprev = jnp.pad(pool_prev, ((0, 0), (n_new - pool_prev.shape[1], 0), (0, 0)))
    c1 = jnp.pad(conv_prev[:, 1:2], ((0, 0), (0, n_new - 1), (0, 0))).reshape(M, 2 * d_ff)
    c2 = jnp.pad(conv_prev, ((0, 0), (0, n_new - 2), (0, 0))).reshape(M, 2 * d_ff)
    vm = pl.BlockSpec(memory_space=pltpu.VMEM)
    out_shape = (
        jax.ShapeDtypeStruct((M, D), F32),
        jax.ShapeDtypeStruct((S, WINDOW, D_KV), F32),
        jax.ShapeDtypeStruct((S, WINDOW, D_KV), F32),
        jax.ShapeDtypeStruct((M, D_POOL), F32),
        jax.ShapeDtypeStruct((M, 2 * d_ff), F32),
    )
    y, ko, vo, uo, ho = pl.pallas_call(
        _sample_kernel,
        out_shape=out_shape,
        in_specs=[pl.BlockSpec(memory_space=pltpu.SMEM)] + [vm] * 22,
        out_specs=(vm,) * 5,
        scratch_shapes=[pltpu.VMEM((2 * M, D_POOL), F32), pltpu.VMEM((M, D_ATTN + D_POOL), BF16)],
        name="sample_layer",
        compiler_params=pltpu.CompilerParams(vmem_limit_bytes=VMEM_LIMIT_BYTES),
    )(w["sinks"], x.reshape(M, D), tabs[0], tabs[1], tabs[2],
      cache_k.reshape(S, past, D_KV), cache_v.reshape(S, past, D_KV), pprev, c1, c2,
      w["g_mix"], w["w_in"], w["q_gain"], w["k_gain"], w["bd"], w["w_pool"], w["pool_scale"],
      w["w_out"], w["g_ffn"], w["w_up"], w["conv_w"], w["conv_b"], w["w_down"])
    return (y.reshape(S, n_new, D), ko, vo, uo.reshape(S, n_new, D_POOL),
            ho.reshape(S, n_new, 2 * d_ff))


def _rope_tables(pos, reps=1):
    half = ROT_DIM // 2
    inv = ROPE_THETA ** (-jnp.arange(0, ROT_DIM, 2, dtype=F32) / ROT_DIM)
    ang = pos.astype(F32)[:, None] * inv[None, :]
    cos, sin = jnp.cos(ang), jnp.sin(ang)
    n = pos.shape[0]
    ones = jnp.ones((n, HEAD_DIM - ROT_DIM), F32)
    zeros_h = jnp.zeros((n, half), F32)
    zeros_r = jnp.zeros((n, HEAD_DIM - ROT_DIM), F32)
    c = jnp.concatenate([cos, cos, ones], axis=1)
    sa = jnp.concatenate([-sin, zeros_h, zeros_r], axis=1)
    sb = jnp.concatenate([zeros_h, sin, zeros_r], axis=1)
    tile = lambda t: jnp.tile(t, (reps, LANES // HEAD_DIM))
    return tile(c), tile(sa), tile(sb)


def kernel(x_prompt, x_sample, cache_k, cache_v, state_pool, state_conv, norm_mix, w_in, q_norm,
           k_norm, attn_sinks, w_pool, pool_scale, w_out, norm_ffn, w_up, conv_w, conv_b, w_down):
    depth = w_in.shape[0]
    B, L, D = x_prompt.shape
    S, n_new, _ = x_sample.shape
    past_len = L
    assert L % PROMPT_TILE == 0 and PROMPT_TILE % CHUNK == 0 and PROMPT_TILE >= WINDOW
    assert w_down.shape[1] % FF_CHUNK == 0

    tabs_p = _rope_tables(jnp.arange(L))
    tabs_s = _rope_tables(past_len + jnp.arange(n_new), reps=S)
    head_id = jnp.arange(MXU_COLS) // HEAD_DIM
    bd = (head_id[:, None] == head_id[None, :]).astype(BF16)
    hh_, g_, d_ = jnp.meshgrid(jnp.arange(Q_PER_KV), jnp.arange(N_KV_HEADS), jnp.arange(HEAD_DIM),
                               indexing="ij")
    head_perm = ((g_ * Q_PER_KV + hh_) * HEAD_DIM + d_).reshape(-1)

    yp, ys = x_prompt, x_sample
    outs = [[] for _ in range(8)]
    for i in range(depth):
        w_in_i = jnp.concatenate([w_in[i][:, head_perm], w_in[i][:, D_ATTN:]], axis=1)
        w_out_i = jnp.concatenate([w_out[i][head_perm, :], w_out[i][D_ATTN:, :]], axis=0)
        w = dict(
            sinks=attn_sinks[i],
            g_mix=norm_mix[i][None, :], w_in=w_in_i.astype(BF16),
            q_gain=jnp.tile(q_norm[i], N_Q_HEADS)[None, :], k_gain=jnp.tile(k_norm[i], N_KV_HEADS)[None, :],
            bd=bd, eye=jnp.eye(MXU_COLS, dtype=BF16), w_pool=w_pool[i].astype(BF16), pool_scale=pool_scale[i][None, :],
            w_out=w_out_i.astype(BF16), g_ffn=norm_ffn[i][None, :], w_up=w_up[i].astype(BF16),
            conv_w=conv_w[i], conv_b=conv_b[i][None, :], w_down=w_down[i].astype(BF16),
        )
        yp, k1, v1, u1, h1 = _prompt_layer(yp, tabs_p, w)
        ys, k2, v2, u2, h2 = _sample_layer(ys, tabs_s, cache_k[i], cache_v[i], state_pool[i],
                                           state_conv[i], w)
        n_pool = state_pool.shape[2]
        n_conv = state_conv.shape[2]
        outs[0].append(k1.reshape(B, WINDOW, N_KV_HEADS, HEAD_DIM))
        outs[1].append(v1.reshape(B, WINDOW, N_KV_HEADS, HEAD_DIM))
        outs[2].append(u1[:, POOL_HALO - n_pool:])
        outs[3].append(h1[:, CONV_HALO - n_conv:])
        outs[4].append(k2.reshape(S, WINDOW, N_KV_HEADS, HEAD_DIM))
        outs[5].append(v2.reshape(S, WINDOW, N_KV_HEADS, HEAD_DIM))
        outs[6].append(u2[:, n_new - n_pool:])
        outs[7].append(h2[:, n_new - n_conv:])
    return (yp, ys) + tuple(jnp.stack(o) for o in outs)
```

```python
import jax
import jax.numpy as jnp
from jax import lax
from jax.experimental import pallas as pl
from jax.experimental.pallas import tpu as pltpu

F32 = jnp.float32
BF16 = jnp.bfloat16

CHUNK = 64
HEAD_DIM = 64
N_Q_HEADS = 8
N_KV_HEADS = 2
Q_PER_KV = N_Q_HEADS // N_KV_HEADS
D_ATTN = N_Q_HEADS * HEAD_DIM
D_KV = N_KV_HEADS * HEAD_DIM
WINDOW = 128
ROT_DIM = 16
ROPE_THETA = 500000.0
POOL_WINDOWS = (2, 4, 8, 16)
POOL_GROUP = 128
D_POOL = POOL_GROUP * len(POOL_WINDOWS)
POOL_HALO = 16
CONV_W = 3
CONV_HALO = 8
EPS = 1e-6
NEG_INF = -1e30

LANES = 128
MXU_COLS = 256
ATT_KEYS = 256
VT_ROWS = D_KV + 16

PROMPT_TILE = 512
FF_CHUNK = 256
DOWN_GROUP = 3
DOWN_LAG = 2
VMEM_LIMIT_BYTES = 56 * 1024 * 1024


def _dot(a, b):
    return jnp.dot(a, b, preferred_element_type=F32)


def _dot_nt(a, b):
    return lax.dot_general(a, b, (((1,), (1,)), ((), ())), preferred_element_type=F32)


def _dot_tn(a, b):
    return lax.dot_general(a, b, (((0,), (0,)), ((), ())), preferred_element_type=F32)


def _rms_rows(x, gain):
    ms = jnp.mean(x * x, axis=-1, keepdims=True)
    return x * lax.rsqrt(ms + EPS) * gain


def _head_rms(t, bd, gain):
    ss = _dot((t * t).astype(BF16), bd)
    return t * lax.rsqrt(ss * (1.0 / HEAD_DIM) + EPS) * gain


def _rope(t, cos, sin_a, sin_b):
    return (t * cos + pltpu.roll(t, LANES - ROT_DIM // 2, 1) * sin_a
            + pltpu.roll(t, ROT_DIM // 2, 1) * sin_b)


def _mixer_inputs(x, gmix, w_in, qg, kg, bd, cos, sin_a, sin_b):
    xn = _rms_rows(x, gmix).astype(BF16)
    h = _dot(xn, w_in)
    q_parts = []
    for j in range(D_ATTN // MXU_COLS):
        qb = _head_rms(h[:, j * MXU_COLS:(j + 1) * MXU_COLS], bd,
                       qg[:, j * MXU_COLS:(j + 1) * MXU_COLS])
        for l in range(MXU_COLS // LANES):
            q_parts.append(_rope(qb[:, l * LANES:(l + 1) * LANES], cos, sin_a, sin_b))
    q = jnp.concatenate(q_parts, axis=1) * (HEAD_DIM ** -0.5)
    k = _head_rms(h[:, D_ATTN:D_ATTN + D_KV], bd[:D_KV, :D_KV], kg)
    k = _rope(k, cos, sin_a, sin_b)
    v = h[:, D_ATTN + D_KV:D_ATTN + 2 * D_KV]
    u = h[:, D_ATTN + 2 * D_KV:]
    return q, k, v, u


def _window_sum(a, w):
    s = 1
    while s < w:
        a = a + pltpu.roll(a, s, 0)
        s *= 2
    return a


def _conv_gate(hg, hv, cw_g, cw_v, cb_g, cb_v):
    cg = cb_g + hg[2] * cw_g[0:1] + hg[1] * cw_g[1:2] + hg[0] * cw_g[2:3]
    cv = cb_v + hv[2] * cw_v[0:1] + hv[1] * cw_v[1:2] + hv[0] * cw_v[2:3]
    return ((cg / (1.0 + jnp.exp(-cg))) * cv).astype(BF16)


def _prompt_kernel(sinks_ref, x_ref, cos_ref, sa_ref, sb_ref,
                   gmix_ref, win_ref, qg_ref, kg_ref, bd_ref, eye_ref, wpool_ref, pscale_ref, wout_ref,
                   gffn_ref, wup_ref, cw_ref, cb_ref, wdown_ref,
                   y_ref, klast_ref, vlast_ref, ulast_ref, hlast_ref,
                   kfull, vfull, vt, ufull, hprev, hbuf, mix_buf, act_buf):
    i = pl.program_id(1)
    T = x_ref.shape[0]
    d_ff = wdown_ref.shape[0]
    n_chunks = T // CHUNK

    @pl.when(i == 0)
    def _():
        kfull[0:WINDOW, :] = jnp.zeros((WINDOW, D_KV), BF16)
        kfull[WINDOW + T:, :] = jnp.zeros((ATT_KEYS - WINDOW - CHUNK, D_KV), BF16)
        vfull[0:WINDOW, :] = jnp.zeros((WINDOW, D_KV), BF16)
        vfull[WINDOW + T:, :] = jnp.zeros((ATT_KEYS - WINDOW - CHUNK, D_KV), BF16)
        for c in range(2):
            vt[c, D_KV:, :] = jnp.ones((VT_ROWS - D_KV, vt.shape[2]), BF16)
        ufull[0:POOL_HALO, :] = jnp.zeros((POOL_HALO, D_POOL), F32)
        hprev[...] = jnp.zeros(hprev.shape, F32)

    x = x_ref[...]
    q, k, v, u = _mixer_inputs(x, gmix_ref[...], win_ref[...], qg_ref[...], kg_ref[...],
                               bd_ref[...], cos_ref[...], sa_ref[...], sb_ref[...])
    klast_ref[...] = k[T - WINDOW:, :]
    vlast_ref[...] = v[T - WINDOW:, :]
    ulast_ref[...] = u[T - POOL_HALO:, :]
    kfull[WINDOW:WINDOW + T, :] = k.astype(BF16)
    vfull[WINDOW:WINDOW + T, :] = v.astype(BF16)
    ufull[POOL_HALO:POOL_HALO + T, :] = u

    eye = eye_ref[...]
    for c in range(2):
        vrows = vfull[c * CHUNK:c * CHUNK + T + WINDOW, :]
        vt[c, 0:D_KV, :] = _dot_nt(eye[:D_KV, :D_KV], vrows).astype(BF16)
    band = WINDOW + CHUNK
    n_q = N_Q_HEADS * CHUNK
    lane = lax.broadcasted_iota(jnp.int32, (1, D_ATTN), 1)
    head_of_col = lax.broadcasted_iota(jnp.int32, (1, n_q), 1) // CHUNK
    key_row = lax.broadcasted_iota(jnp.int32, (band, 1), 0)
    q_of_group = []
    for g in range(N_KV_HEADS):
        in_g = ((lane // HEAD_DIM) % N_KV_HEADS == g).astype(F32)
        q_of_group.append((q * in_g).astype(BF16))
    sink_row = jnp.full((1, n_q), sinks_ref[0], F32)
    for h8 in range(1, N_Q_HEADS):
        sink_row = jnp.where(head_of_col == h8, sinks_ref[h8], sink_row)
    half = Q_PER_KV * CHUNK
    scores = []
    for j in range(n_chunks):
        r0 = j * CHUNK
        qs = jnp.concatenate(
            [q_of_group[g][r0:r0 + CHUNK, hh * LANES:(hh + 1) * LANES]
             for g in range(N_KV_HEADS) for hh in range(Q_PER_KV)], axis=0)
        scores.append(_dot_nt(kfull[r0:r0 + band, :], qs))
    probs = []
    for j, s in enumerate(scores):
        r0 = j * CHUNK
        if r0 < WINDOW:
            first_valid = jnp.where(i == 0, WINDOW - r0, 0)
            s = jnp.where(key_row >= first_valid, s, NEG_INF)
        m = jnp.maximum(jnp.max(s, axis=0, keepdims=True), sink_row)
        probs.append((jnp.exp(s - m).astype(BF16), jnp.exp(sink_row - m)))
    outs_t = []
    for j, (e, sink_e) in enumerate(probs):
        vtb = vt[j % 2, :, (j // 2) * LANES:(j // 2) * LANES + band]
        oa = _dot(vtb, e)
        inv_den = 1.0 / (oa[D_KV:D_KV + 1] + sink_e)
        outs_t.append(jnp.concatenate(
            [oa[g * HEAD_DIM:(g + 1) * HEAD_DIM, g * half:(g + 1) * half]
             * inv_den[:, g * half:(g + 1) * half] for g in range(N_KV_HEADS)],
            axis=0).astype(BF16))
    for j, ot in enumerate(outs_t):
        r0 = j * CHUNK
        o = _dot_nt(eye, ot)
        for hh in range(Q_PER_KV):
            mix_buf[r0:r0 + CHUNK, hh * LANES:(hh + 1) * LANES] = (
                o[hh * CHUNK:(hh + 1) * CHUNK].astype(BF16))

    uf = ufull[...]
    pos = i * T + lax.broadcasted_iota(jnp.int32, (T, 1), 0)
    for gi, w in enumerate(POOL_WINDOWS):
        sl = slice(gi * POOL_GROUP, (gi + 1) * POOL_GROUP)
        tsum = _window_sum(uf[:, sl], w)[POOL_HALO:]
        cnt = jnp.minimum(pos + 1, w).astype(F32)
        d = (tsum / cnt - u[:, sl]).astype(BF16)
        pool = _dot(d, wpool_ref[gi]) * pscale_ref[:, sl]
        mix_buf[:, D_ATTN + gi * POOL_GROUP:D_ATTN + (gi + 1) * POOL_GROUP] = pool.astype(BF16)

    kfull[0:WINDOW, :] = kfull[T:T + WINDOW, :]
    vfull[0:WINDOW, :] = vfull[T:T + WINDOW, :]
    ufull[0:POOL_HALO, :] = u[T - POOL_HALO:, :]

    x1 = x + _dot(mix_buf[...], wout_ref[...])

    xn2 = _rms_rows(x1, gffn_ref[...]).astype(BF16)
    F = FF_CHUNK
    y, k_done = x1, 0
    for c in range(d_ff // F):
        cols = (slice(c * F, (c + 1) * F), slice(d_ff + c * F, d_ff + (c + 1) * F))
        hb = hbuf.at[c % 2]
        taps = []
        for half, cs in enumerate(cols):
            hs = slice(half * F, (half + 1) * F)
            h = _dot(xn2, wup_ref[:, cs])
            hb[0:CONV_HALO, hs] = hprev[:, cs]
            hb[CONV_HALO:CONV_HALO + T, hs] = h
            hlast_ref[:, cs] = h[T - CONV_HALO:, :]
            hprev[:, cs] = h[T - CONV_HALO:, :]
            taps.append((h, hb[CONV_HALO - 1:CONV_HALO - 1 + T, hs],
                         hb[CONV_HALO - 2:CONV_HALO - 2 + T, hs]))
        act_buf[:, cols[0]] = _conv_gate(taps[0], taps[1], cw_ref[:, cols[0]], cw_ref[:, cols[1]],
                                         cb_ref[:, cols[0]], cb_ref[:, cols[1]])
        c_done = c - DOWN_LAG
        if c_done >= 0 and (c_done + 1) % DOWN_GROUP == 0:
            ks = slice((c_done + 1 - DOWN_GROUP) * F, (c_done + 1) * F)
            y = y + _dot(act_buf[:, ks], wdown_ref[ks, :])
            k_done = (c_done + 1) * F
    y_ref[...] = y + _dot(act_buf[:, k_done:], wdown_ref[k_done:, :])


def _const_spec(shape, single_buffer=False):
    nd = len(shape)
    idx = lambda *_: (0,) * nd
    if single_buffer:
        return pl.BlockSpec(shape, idx, pipeline_mode=pl.Buffered(1))
    return pl.BlockSpec(shape, idx)


def _prompt_layer(x, tabs, w):
    B, L, D = x.shape
    T = PROMPT_TILE
    d_ff = w["w_down"].shape[0]
    n_tiles = L // T
    tile_spec = pl.BlockSpec((None, T, D), lambda b, i: (b, i, 0))
    tab_spec = pl.BlockSpec((T, LANES), lambda b, i: (i, 0))

    def last_spec(rows, cols):
        return pl.BlockSpec((None, rows, cols), lambda b, i: (b, 0, 0))

    in_specs = [
        pl.BlockSpec(memory_space=pltpu.SMEM),
        tile_spec, tab_spec, tab_spec, tab_spec,
        _const_spec((1, D)), _const_spec(w["w_in"].shape, True),
        _const_spec((1, D_ATTN)), _const_spec((1, D_KV)), _const_spec((MXU_COLS, MXU_COLS)),
        _const_spec((MXU_COLS, MXU_COLS)),
        _const_spec(w["w_pool"].shape), _const_spec((1, D_POOL)), _const_spec(w["w_out"].shape, True),
        _const_spec((1, D)), _const_spec(w["w_up"].shape, True),
        _const_spec((CONV_W, 2 * d_ff)), _const_spec((1, 2 * d_ff)), _const_spec(w["w_down"].shape, True),
    ]
    out_shape = (
        jax.ShapeDtypeStruct((B, L, D), F32),
        jax.ShapeDtypeStruct((B, WINDOW, D_KV), F32),
        jax.ShapeDtypeStruct((B, WINDOW, D_KV), F32),
        jax.ShapeDtypeStruct((B, POOL_HALO, D_POOL), F32),
        jax.ShapeDtypeStruct((B, CONV_HALO, 2 * d_ff), F32),
    )
    out_specs = (tile_spec, last_spec(WINDOW, D_KV), last_spec(WINDOW, D_KV),
                 last_spec(POOL_HALO, D_POOL), last_spec(CONV_HALO, 2 * d_ff))
    scratch = [
        pltpu.VMEM((T + ATT_KEYS - CHUNK, D_KV), BF16),
        pltpu.VMEM((T + ATT_KEYS - CHUNK, D_KV), BF16),
        pltpu.VMEM((2, VT_ROWS, T + WINDOW), BF16),
        pltpu.VMEM((POOL_HALO + T, D_POOL), F32),
        pltpu.VMEM((CONV_HALO, 2 * d_ff), F32),
        pltpu.VMEM((2, CONV_HALO + T, 2 * FF_CHUNK), F32),
        pltpu.VMEM((T, D_ATTN + D_POOL), BF16),
        pltpu.VMEM((T, d_ff), BF16),
    ]
    return pl.pallas_call(
        _prompt_kernel,
        out_shape=out_shape,
        grid=(B, n_tiles),
        in_specs=in_specs,
        out_specs=out_specs,
        scratch_shapes=scratch,
        name="prompt_layer",
        compiler_params=pltpu.CompilerParams(
            dimension_semantics=("arbitrary", "arbitrary"),
            vmem_limit_bytes=VMEM_LIMIT_BYTES),
    )(w["sinks"], x, tabs[0], tabs[1], tabs[2],
      w["g_mix"], w["w_in"], w["q_gain"], w["k_gain"], w["bd"], w["eye"], w["w_pool"], w["pool_scale"],
      w["w_out"], w["g_ffn"], w["w_up"], w["conv_w"], w["conv_b"], w["w_down"])


def _sample_kernel(sinks_ref, x_ref, cos_ref, sa_ref, sb_ref, ck_ref, cv_ref, pprev_ref,
                   c1_ref, c2_ref,
                   gmix_ref, win_ref, qg_ref, kg_ref, bd_ref, wpool_ref, pscale_ref, wout_ref,
                   gffn_ref, wup_ref, cw_ref, cb_ref, wdown_ref,
                   y_ref, kout_ref, vout_ref, uout_ref, hout_ref,
                   ufull, mix_buf):
    M = x_ref.shape[0]
    n_streams = ck_ref.shape[0]
    n_new = M // n_streams
    past = ck_ref.shape[1]
    d_ff = wdown_ref.shape[0]

    x = x_ref[...]
    q, k, v, u = _mixer_inputs(x, gmix_ref[...], win_ref[...], qg_ref[...], kg_ref[...],
                               bd_ref[...], cos_ref[...], sa_ref[...], sb_ref[...])
    q = q.astype(BF16)
    uout_ref[...] = u

    for s_ in range(n_streams):
        rows = slice(s_ * n_new, (s_ + 1) * n_new)
        keys = jnp.concatenate([ck_ref[s_], k[rows]], axis=0)
        vals = jnp.concatenate([cv_ref[s_], v[rows]], axis=0)
        kout_ref[s_] = keys[past + n_new - WINDOW:, :]
        vout_ref[s_] = vals[past + n_new - WINDOW:, :]
        keys16 = keys.astype(BF16)
        vals16 = vals.astype(BF16)
        outs = []
        for g in range(N_KV_HEADS):
            sink_col = jnp.concatenate(
                [jnp.full((n_new, 1), sinks_ref[g * Q_PER_KV + hh], F32) for hh in range(Q_PER_KV)],
                axis=0)
            qs = jnp.concatenate(
                [q[rows, hh * LANES + g * HEAD_DIM:hh * LANES + (g + 1) * HEAD_DIM]
                 for hh in range(Q_PER_KV)], axis=0)
            sc = _dot_nt(qs, keys16[:, g * HEAD_DIM:(g + 1) * HEAD_DIM])
            m = jnp.maximum(jnp.max(sc, axis=-1, keepdims=True), sink_col)
            e = jnp.exp(sc - m)
            den = jnp.sum(e, axis=-1, keepdims=True) + jnp.exp(sink_col - m)
            outs.append(_dot(e.astype(BF16), vals16[:, g * HEAD_DIM:(g + 1) * HEAD_DIM]) * (1.0 / den))
        for hh in range(Q_PER_KV):
            pair = jnp.concatenate([o[hh * n_new:(hh + 1) * n_new] for o in outs], axis=1)
            mix_buf[rows, hh * LANES:(hh + 1) * LANES] = pair.astype(BF16)
        ufull[s_ * 2 * n_new:s_ * 2 * n_new + n_new, :] = pprev_ref[s_]
        ufull[s_ * 2 * n_new + n_new:(s_ + 1) * 2 * n_new, :] = u[rows]

    uf = ufull[...]
    for gi, w in enumerate(POOL_WINDOWS):
        sl = slice(gi * POOL_GROUP, (gi + 1) * POOL_GROUP)
        ws = _window_sum(uf[:, sl], w)
        tsum = jnp.concatenate(
            [ws[s_ * 2 * n_new + n_new:(s_ + 1) * 2 * n_new] for s_ in range(n_streams)], axis=0)
        d = (tsum / float(w) - u[:, sl]).astype(BF16)
        pool = _dot(d, wpool_ref[gi]) * pscale_ref[:, sl]
        mix_buf[:, D_ATTN + gi * POOL_GROUP:D_ATTN + (gi + 1) * POOL_GROUP] = pool.astype(BF16)

    x1 = x + _dot(mix_buf[...], wout_ref[...])
    xn2 = _rms_rows(x1, gffn_ref[...]).astype(BF16)

    t_in_stream = lax.broadcasted_iota(jnp.int32, (M, 1), 0) % n_new

    def taps(h, cs):
        return (h, jnp.where(t_in_stream >= 1, pltpu.roll(h, 1, 0), c1_ref[:, cs]),
                jnp.where(t_in_stream >= 2, pltpu.roll(h, 2, 0), c2_ref[:, cs]))

    acc = x1
    for c in range(d_ff // FF_CHUNK):
        cg = slice(c * FF_CHUNK, (c + 1) * FF_CHUNK)
        cv = slice(d_ff + c * FF_CHUNK, d_ff + (c + 1) * FF_CHUNK)
        hg = _dot(xn2, wup_ref[:, cg])
        hv = _dot(xn2, wup_ref[:, cv])
        hout_ref[:, cg] = hg
        hout_ref[:, cv] = hv
        act = _conv_gate(taps(hg, cg), taps(hv, cv), cw_ref[:, cg], cw_ref[:, cv],
                         cb_ref[:, cg], cb_ref[:, cv])
        acc = acc + _dot(act, wdown_ref[cg, :])
    y_ref[...] = acc


def _sample_layer(x, tabs, cache_k, cache_v, pool_prev, conv_prev, w):
    S, n_new, D = x.shape
    M = S * n_new
    d_ff = w["w_down"].shape[0]
    past = cache_k.shape[1]
    pprev = jnp.pad(pool_prev, ((0, 0), (n_new - pool_prev.shape[1], 0), (0, 0)))
    c1 = jnp.pad(conv_prev[:, 1:2], ((0, 0), (0, n_new - 1), (0, 0))).reshape(M, 2 * d_ff)
    c2 = jnp.pad(conv_prev, ((0, 0), (0, n_new - 2), (0, 0))).reshape(M, 2 * d_ff)
    vm = pl.BlockSpec(memory_space=pltpu.VMEM)
    out_shape = (
        jax.ShapeDtypeStruct((M, D), F32),
        jax.ShapeDtypeStruct((S, WINDOW, D_KV), F32),
        jax.ShapeDtypeStruct((S, WINDOW, D_KV), F32),
        jax.ShapeDtypeStruct((M, D_POOL), F32),
        jax.ShapeDtypeStruct((M, 2 * d_ff), F32),
    )
    y, ko, vo, uo, ho = pl.pallas_call(
        _sample_kernel,
        out_shape=out_shape,
        in_specs=[pl.BlockSpec(memory_space=pltpu.SMEM)] + [vm] * 22,
        out_specs=(vm,) * 5,
        scratch_shapes=[pltpu.VMEM((2 * M, D_POOL), F32), pltpu.VMEM((M, D_ATTN + D_POOL), BF16)],
        name="sample_layer",
        compiler_params=pltpu.CompilerParams(vmem_limit_bytes=VMEM_LIMIT_BYTES),
    )(w["sinks"], x.reshape(M, D), tabs[0], tabs[1], tabs[2],
      cache_k.reshape(S, past, D_KV), cache_v.reshape(S, past, D_KV), pprev, c1, c2,
      w["g_mix"], w["w_in"], w["q_gain"], w["k_gain"], w["bd"], w["w_pool"], w["pool_scale"],
      w["w_out"], w["g_ffn"], w["w_up"], w["conv_w"], w["conv_b"], w["w_down"])
    return (y.reshape(S, n_new, D), ko, vo, uo.reshape(S, n_new, D_POOL),
            ho.reshape(S, n_new, 2 * d_ff))


def _rope_tables(pos, reps=1):
    half = ROT_DIM // 2
    inv = ROPE_THETA ** (-jnp.arange(0, ROT_DIM, 2, dtype=F32) / ROT_DIM)
    ang = pos.astype(F32)[:, None] * inv[None, :]
    cos, sin = jnp.cos(ang), jnp.sin(ang)
    n = pos.shape[0]
    ones = jnp.ones((n, HEAD_DIM - ROT_DIM), F32)
    zeros_h = jnp.zeros((n, half), F32)
    zeros_r = jnp.zeros((n, HEAD_DIM - ROT_DIM), F32)
    c = jnp.concatenate([cos, cos, ones], axis=1)
    sa = jnp.concatenate([-sin, zeros_h, zeros_r], axis=1)
    sb = jnp.concatenate([zeros_h, sin, zeros_r], axis=1)
    tile = lambda t: jnp.tile(t, (reps, LANES // HEAD_DIM))
    return tile(c), tile(sa), tile(sb)


def kernel(x_prompt, x_sample, cache_k, cache_v, state_pool, state_conv, norm_mix, w_in, q_norm,
           k_norm, attn_sinks, w_pool, pool_scale, w_out, norm_ffn, w_up, conv_w, conv_b, w_down):
    depth = w_in.shape[0]
    B, L, D = x_prompt.shape
    S, n_new, _ = x_sample.shape
    past_len = L
    assert L % PROMPT_TILE == 0 and PROMPT_TILE % CHUNK == 0 and PROMPT_TILE >= WINDOW
    assert w_down.shape[1] % FF_CHUNK == 0

    tabs_p = _rope_tables(jnp.arange(L))
    tabs_s = _rope_tables(past_len + jnp.arange(n_new), reps=S)
    head_id = jnp.arange(MXU_COLS) // HEAD_DIM
    bd = (head_id[:, None] == head_id[None, :]).astype(BF16)
    hh_, g_, d_ = jnp.meshgrid(jnp.arange(Q_PER_KV), jnp.arange(N_KV_HEADS), jnp.arange(HEAD_DIM),
                               indexing="ij")
    head_perm = ((g_ * Q_PER_KV + hh_) * HEAD_DIM + d_).reshape(-1)

    yp, ys = x_prompt, x_sample
    outs = [[] for _ in range(8)]
    for i in range(depth):
        w_in_i = jnp.concatenate([w_in[i][:, head_perm], w_in[i][:, D_ATTN:]], axis=1)
        w_out_i = jnp.concatenate([w_out[i][head_perm, :], w_out[i][D_ATTN:, :]], axis=0)
        w = dict(
            sinks=attn_sinks[i],
            g_mix=norm_mix[i][None, :], w_in=w_in_i.astype(BF16),
            q_gain=jnp.tile(q_norm[i], N_Q_HEADS)[None, :], k_gain=jnp.tile(k_norm[i], N_KV_HEADS)[None, :],
            bd=bd, eye=jnp.eye(MXU_COLS, dtype=BF16), w_pool=w_pool[i].astype(BF16), pool_scale=pool_scale[i][None, :],
            w_out=w_out_i.astype(BF16), g_ffn=norm_ffn[i][None, :], w_up=w_up[i].astype(BF16),
            conv_w=conv_w[i], conv_b=conv_b[i][None, :], w_down=w_down[i].astype(BF16),
        )
        yp, k1, v1, u1, h1 = _prompt_layer(yp, tabs_p, w)
        ys, k2, v2, u2, h2 = _sample_layer(ys, tabs_s, cache_k[i], cache_v[i], state_pool[i],
                                           state_conv[i], w)
        n_pool = state_pool.shape[2]
        n_conv = state_conv.shape[2]
        outs[0].append(k1.reshape(B, WINDOW, N_KV_HEADS, HEAD_DIM))
        outs[1].append(v1.reshape(B, WINDOW, N_KV_HEADS, HEAD_DIM))
        outs[2].append(u1[:, POOL_HALO - n_pool:])
        outs[3].append(h1[:, CONV_HALO - n_conv:])
        outs[4].append(k2.reshape(S, WINDOW, N_KV_HEADS, HEAD_DIM))
        outs[5].append(v2.reshape(S, WINDOW, N_KV_HEADS, HEAD_DIM))
        outs[6].append(u2[:, n_new - n_pool:])
        outs[7].append(h2[:, n_new - n_conv:])
    return (yp, ys) + tuple(jnp.stack(o) for o in outs)
```

```python
import jax
import jax.numpy as jnp
from jax import lax
from jax.experimental import pallas as pl
from jax.experimental.pallas import tpu as pltpu

F32 = jnp.float32
BF16 = jnp.bfloat16

CHUNK = 64
HEAD_DIM = 64
N_Q_HEADS = 8
N_KV_HEADS = 2
Q_PER_KV = N_Q_HEADS // N_KV_HEADS
D_ATTN = N_Q_HEADS * HEAD_DIM
D_KV = N_KV_HEADS * HEAD_DIM
WINDOW = 128
ROT_DIM = 16
ROPE_THETA = 500000.0
POOL_WINDOWS = (2, 4, 8, 16)
POOL_GROUP = 128
D_POOL = POOL_GROUP * len(POOL_WINDOWS)
POOL_HALO = 16
CONV_W = 3
CONV_HALO = 8
EPS = 1e-6
NEG_INF = -1e30

LANES = 128
MXU_COLS = 256
ATT_KEYS = 256
VT_ROWS = D_KV + 16

PROMPT_TILE = 512
FF_CHUNK = 256
DOWN_GROUP = 3
DOWN_LAG = 2
VMEM_LIMIT_BYTES = 56 * 1024 * 1024


def _dot(a, b):
    return jnp.dot(a, b, preferred_element_type=F32)


def _dot_nt(a, b):
    return lax.dot_general(a, b, (((1,), (1,)), ((), ())), preferred_element_type=F32)


def _dot_tn(a, b):
    return lax.dot_general(a, b, (((0,), (0,)), ((), ())), preferred_element_type=F32)


def _rms_rows(x, gain):
    ms = jnp.mean(x * x, axis=-1, keepdims=True)
    return x * lax.rsqrt(ms + EPS) * gain


def _head_rms(t, bd, gain):
    ss = _dot((t * t).astype(BF16), bd)
    return t * lax.rsqrt(ss * (1.0 / HEAD_DIM) + EPS) * gain


def _rope(t, cos, sin_a, sin_b):
    return (t * cos + pltpu.roll(t, LANES - ROT_DIM // 2, 1) * sin_a
            + pltpu.roll(t, ROT_DIM // 2, 1) * sin_b)


def _mixer_inputs(x, gmix, w_in, qg, kg, bd, cos, sin_a, sin_b):
    xn = _rms_rows(x, gmix).astype(BF16)
    h = _dot(xn, w_in)
    q_parts = []
    for j in range(D_ATTN // MXU_COLS):
        qb = _head_rms(h[:, j * MXU_COLS:(j + 1) * MXU_COLS], bd,
                       qg[:, j * MXU_COLS:(j + 1) * MXU_COLS])
        for l in range(MXU_COLS // LANES):
            q_parts.append(_rope(qb[:, l * LANES:(l + 1) * LANES], cos, sin_a, sin_b))
    q = jnp.concatenate(q_parts, axis=1) * (HEAD_DIM ** -0.5)
    k = _head_rms(h[:, D_ATTN:D_ATTN + D_KV], bd[:D_KV, :D_KV], kg)
    k = _rope(k, cos, sin_a, sin_b)
    v = h[:, D_ATTN + D_KV:D_ATTN + 2 * D_KV]
    u = h[:, D_ATTN + 2 * D_KV:]
    return q, k, v, u


def _window_sum(a, w):
    s = 1
    while s < w:
        a = a + pltpu.roll(a, s, 0)
        s *= 2
    return a


def _conv_gate(hg, hv, cw_g, cw_v, cb_g, cb_v):
    cg = cb_g + hg[2] * cw_g[0:1] + hg[1] * cw_g[1:2] + hg[0] * cw_g[2:3]
    cv = cb_v + hv[2] * cw_v[0:1] + hv[1] * cw_v[1:2] + hv[0] * cw_v[2:3]
    return ((cg / (1.0 + jnp.exp(-cg))) * cv).astype(BF16)


def _prompt_kernel(sinks_ref, x_ref, cos_ref, sa_ref, sb_ref,
                   gmix_ref, win_ref, qg_ref, kg_ref, bd_ref, eye_ref, wpool_ref, pscale_ref, wout_ref,
                   gffn_ref, wup_ref, cw_ref, cb_ref, wdown_ref,
                   y_ref, klast_ref, vlast_ref, ulast_ref, hlast_ref,
                   kfull, vfull, vt, ufull, hprev, hbuf, mix_buf, act_buf):
    i = pl.program_id(1)
    T = x_ref.shape[0]
    d_ff = wdown_ref.shape[0]
    n_chunks = T // CHUNK

    @pl.when(i == 0)
    def _():
        kfull[0:WINDOW, :] = jnp.zeros((WINDOW, D_KV), BF16)
        kfull[WINDOW + T:, :] = jnp.zeros((ATT_KEYS - WINDOW - CHUNK, D_KV), BF16)
        vfull[0:WINDOW, :] = jnp.zeros((WINDOW, D_KV), BF16)
        vfull[WINDOW + T:, :] = jnp.zeros((ATT_KEYS - WINDOW - CHUNK, D_KV), BF16)
        for c in range(2):
            vt[c, D_KV:, :] = jnp.ones((VT_ROWS - D_KV, vt.shape[2]), BF16)
        ufull[0:POOL_HALO, :] = jnp.zeros((POOL_HALO, D_POOL), F32)
        hprev[...] = jnp.zeros(hprev.shape, F32)

    x = x_ref[...]
    q, k, v, u = _mixer_inputs(x, gmix_ref[...], win_ref[...], qg_ref[...], kg_ref[...],
                               bd_ref[...], cos_ref[...], sa_ref[...], sb_ref[...])
    klast_ref[...] = k[T - WINDOW:, :]
    vlast_ref[...] = v[T - WINDOW:, :]
    ulast_ref[...] = u[T - POOL_HALO:, :]
    kfull[WINDOW:WINDOW + T, :] = k.astype(BF16)
    vfull[WINDOW:WINDOW + T, :] = v.astype(BF16)
    ufull[POOL_HALO:POOL_HALO + T, :] = u

    eye = eye_ref[...]
    for c in range(2):
        vrows = vfull[c * CHUNK:c * CHUNK + T + WINDOW, :]
        vt[c, 0:D_KV, :] = _dot_nt(eye[:D_KV, :D_KV], vrows).astype(BF16)
    band = WINDOW + CHUNK
    n_q = N_Q_HEADS * CHUNK
    lane = lax.broadcasted_iota(jnp.int32, (1, D_ATTN), 1)
    head_of_col = lax.broadcasted_iota(jnp.int32, (1, n_q), 1) // CHUNK
    key_row = lax.broadcasted_iota(jnp.int32, (band, 1), 0)
    q_of_group = []
    for g in range(N_KV_HEADS):
        in_g = ((lane // HEAD_DIM) % N_KV_HEADS == g).astype(F32)
        q_of_group.append((q * in_g).astype(BF16))
    sink_row = jnp.full((1, n_q), sinks_ref[0], F32)
    for h8 in range(1, N_Q_HEADS):
        sink_row = jnp.where(head_of_col == h8, sinks_ref[h8], sink_row)
    half = Q_PER_KV * CHUNK
    scores = []
    for j in range(n_chunks):
        r0 = j * CHUNK
        qs = jnp.concatenate(
            [q_of_group[g][r0:r0 + CHUNK, hh * LANES:(hh + 1) * LANES]
             for g in range(N_KV_HEADS) for hh in range(Q_PER_KV)], axis=0)
        scores.append(_dot_nt(kfull[r0:r0 + band, :], qs))
    probs = []
    for j, s in enumerate(scores):
        r0 = j * CHUNK
        if r0 < WINDOW:
            first_valid = jnp.where(i == 0, WINDOW - r0, 0)
            s = jnp.where(key_row >= first_valid, s, NEG_INF)
        m = jnp.maximum(jnp.max(s, axis=0, keepdims=True), sink_row)
        probs.append((jnp.exp(s - m).astype(BF16), jnp.exp(sink_row - m)))
    outs_t = []
    for j, (e, sink_e) in enumerate(probs):
        vtb = vt[j % 2, :, (j // 2) * LANES:(j // 2) * LANES + band]
        oa = _dot(vtb, e)
        inv_den = 1.0 / (oa[D_KV:D_KV + 1] + sink_e)
        outs_t.append(jnp.concatenate(
            [oa[g * HEAD_DIM:(g + 1) * HEAD_DIM, g * half:(g + 1) * half]
             * inv_den[:, g * half:(g + 1) * half] for g in range(N_KV_HEADS)],
            axis=0).astype(BF16))
    for j, ot in enumerate(outs_t):
        r0 = j * CHUNK
        o = _dot_nt(eye, ot)
        for hh in range(Q_PER_KV):
            mix_buf[r0:r0 + CHUNK, hh * LANES:(hh + 1) * LANES] = (
                o[hh * CHUNK:(hh + 1) * CHUNK].astype(BF16))

    uf = ufull[...]
    pos = i * T + lax.broadcasted_iota(jnp.int32, (T, 1), 0)
    for gi, w in enumerate(POOL_WINDOWS):
        sl = slice(gi * POOL_GROUP, (gi + 1) * POOL_GROUP)
        tsum = _window_sum(uf[:, sl], w)[POOL_HALO:]
        cnt = jnp.minimum(pos + 1, w).astype(F32)
        d = (tsum / cnt - u[:, sl]).astype(BF16)
        pool = _dot(d, wpool_ref[gi]) * pscale_ref[:, sl]
        mix_buf[:, D_ATTN + gi * POOL_GROUP:D_ATTN + (gi + 1) * POOL_GROUP] = pool.astype(BF16)

    kfull[0:WINDOW, :] = kfull[T:T + WINDOW, :]
    vfull[0:WINDOW, :] = vfull[T:T + WINDOW, :]
    ufull[0:POOL_HALO, :] = u[T - POOL_HALO:, :]

    x1 = x + _dot(mix_buf[...], wout_ref[...])

    xn2 = _rms_rows(x1, gffn_ref[...]).astype(BF16)
    F = FF_CHUNK
    y, k_done = x1, 0
    for c in range(d_ff // F):
        cols = (slice(c * F, (c + 1) * F), slice(d_ff + c * F, d_ff + (c + 1) * F))
        hb = hbuf.at[c % 2]
        taps = []
        for half, cs in enumerate(cols):
            hs = slice(half * F, (half + 1) * F)
            h = _dot(xn2, wup_ref[:, cs])
            hb[0:CONV_HALO, hs] = hprev[:, cs]
            hb[CONV_HALO:CONV_HALO + T, hs] = h
            hlast_ref[:, cs] = h[T - CONV_HALO:, :]
            hprev[:, cs] = h[T - CONV_HALO:, :]
            taps.append((h, hb[CONV_HALO - 1:CONV_HALO - 1 + T, hs],
                         hb[CONV_HALO - 2:CONV_HALO - 2 + T, hs]))
        act_buf[:, cols[0]] = _conv_gate(taps[0], taps[1], cw_ref[:, cols[0]], cw_ref[:, cols[1]],
                                         cb_ref[:, cols[0]], cb_ref[:, cols[1]])
        c_done = c - DOWN_LAG
        if c_done >= 0 and (c_done + 1) % DOWN_GROUP == 0:
            ks = slice((c_done + 1 - DOWN_GROUP) * F, (c_done + 1) * F)
            y = y + _dot(act_buf[:, ks], wdown_ref[ks, :])
            k_done = (c_done + 1) * F
    y_ref[...] = y + _dot(act_buf[:, k_done:], wdown_ref[k_done:, :])


def _const_spec(shape):
    nd = len(shape)
    return pl.BlockSpec(shape, lambda *_: (0,) * nd)


def _layer_spec(stacked, layer):
    return pl.BlockSpec((None,) + stacked.shape[1:], lambda *_: (layer, 0, 0),
                        pipeline_mode=pl.Buffered(1))


def _weight_specs(w, D, d_ff):
    layer = w["layer"]
    return [
        _const_spec((1, D)), _layer_spec(w["w_in"], layer),
        _const_spec((1, D_ATTN)), _const_spec((1, D_KV)), _const_spec((MXU_COLS, MXU_COLS)),
        _const_spec((MXU_COLS, MXU_COLS)),
        _const_spec(w["w_pool"].shape), _const_spec((1, D_POOL)), _layer_spec(w["w_out"], layer),
        _const_spec((1, D)), _layer_spec(w["w_up"], layer),
        _const_spec((CONV_W, 2 * d_ff)), _const_spec((1, 2 * d_ff)), _layer_spec(w["w_down"], layer),
    ]


def _weight_args(w):
    return (w["g_mix"], w["w_in"], w["q_gain"], w["k_gain"], w["bd"], w["eye"], w["w_pool"],
            w["pool_scale"], w["w_out"], w["g_ffn"], w["w_up"], w["conv_w"], w["conv_b"], w["w_down"])


def _prompt_layer(x, tabs, w):
    B, L, D = x.shape
    T = PROMPT_TILE
    d_ff = w["w_down"].shape[1]
    n_tiles = L // T
    tile_spec = pl.BlockSpec((None, T, D), lambda b, i: (b, i, 0))
    tab_spec = pl.BlockSpec((T, LANES), lambda b, i: (i, 0))

    def last_spec(rows, cols):
        return pl.BlockSpec((None, rows, cols), lambda b, i: (b, 0, 0))

    in_specs = [
        pl.BlockSpec(memory_space=pltpu.SMEM),
        tile_spec, tab_spec, tab_spec, tab_spec,
    ] + _weight_specs(w, D, d_ff)
    out_shape = (
        jax.ShapeDtypeStruct((B, L, D), F32),
        jax.ShapeDtypeStruct((B, WINDOW, D_KV), F32),
        jax.ShapeDtypeStruct((B, WINDOW, D_KV), F32),
        jax.ShapeDtypeStruct((B, POOL_HALO, D_POOL), F32),
        jax.ShapeDtypeStruct((B, CONV_HALO, 2 * d_ff), F32),
    )
    out_specs = (tile_spec, last_spec(WINDOW, D_KV), last_spec(WINDOW, D_KV),
                 last_spec(POOL_HALO, D_POOL), last_spec(CONV_HALO, 2 * d_ff))
    scratch = [
        pltpu.VMEM((T + ATT_KEYS - CHUNK, D_KV), BF16),
        pltpu.VMEM((T + ATT_KEYS - CHUNK, D_KV), BF16),
        pltpu.VMEM((2, VT_ROWS, T + WINDOW), BF16),
        pltpu.VMEM((POOL_HALO + T, D_POOL), F32),
        pltpu.VMEM((CONV_HALO, 2 * d_ff), F32),
        pltpu.VMEM((2, CONV_HALO + T, 2 * FF_CHUNK), F32),
        pltpu.VMEM((T, D_ATTN + D_POOL), BF16),
        pltpu.VMEM((T, d_ff), BF16),
    ]
    return pl.pallas_call(
        _prompt_kernel,
        out_shape=out_shape,
        grid=(B, n_tiles),
        in_specs=in_specs,
        out_specs=out_specs,
        scratch_shapes=scratch,
        name="prompt_layer",
        compiler_params=pltpu.CompilerParams(
            dimension_semantics=("arbitrary", "arbitrary"),
            vmem_limit_bytes=VMEM_LIMIT_BYTES),
    )(w["sinks"], x, tabs[0], tabs[1], tabs[2], *_weight_args(w))


def _sample_kernel(sinks_ref, x_ref, cos_ref, sa_ref, sb_ref, ck_ref, cv_ref, pprev_ref,
                   c1_ref, c2_ref,
                   gmix_ref, win_ref, qg_ref, kg_ref, bd_ref, eye_ref, wpool_ref, pscale_ref, wout_ref,
                   gffn_ref, wup_ref, cw_ref, cb_ref, wdown_ref,
                   y_ref, kout_ref, vout_ref, uout_ref, hout_ref,
                   ufull, mix_buf):
    M = x_ref.shape[0]
    n_streams = ck_ref.shape[0]
    n_new = M // n_streams
    past = ck_ref.shape[1]
    d_ff = wdown_ref.shape[0]

    x = x_ref[...]
    q, k, v, u = _mixer_inputs(x, gmix_ref[...], win_ref[...], qg_ref[...], kg_ref[...],
                               bd_ref[...], cos_ref[...], sa_ref[...], sb_ref[...])
    q = q.astype(BF16)
    uout_ref[...] = u

    for s_ in range(n_streams):
        rows = slice(s_ * n_new, (s_ + 1) * n_new)
        keys = jnp.concatenate([ck_ref[s_], k[rows]], axis=0)
        vals = jnp.concatenate([cv_ref[s_], v[rows]], axis=0)
        kout_ref[s_] = keys[past + n_new - WINDOW:, :]
        vout_ref[s_] = vals[past + n_new - WINDOW:, :]
        keys16 = keys.astype(BF16)
        vals16 = vals.astype(BF16)
        outs = []
        for g in range(N_KV_HEADS):
            sink_col = jnp.concatenate(
                [jnp.full((n_new, 1), sinks_ref[g * Q_PER_KV + hh], F32) for hh in range(Q_PER_KV)],
                axis=0)
            qs = jnp.concatenate(
                [q[rows, hh * LANES + g * HEAD_DIM:hh * LANES + (g + 1) * HEAD_DIM]
                 for hh in range(Q_PER_KV)], axis=0)
            sc = _dot_nt(qs, keys16[:, g * HEAD_DIM:(g + 1) * HEAD_DIM])
            m = jnp.maximum(jnp.max(sc, axis=-1, keepdims=True), sink_col)
            e = jnp.exp(sc - m)
            den = jnp.sum(e, axis=-1, keepdims=True) + jnp.exp(sink_col - m)
            outs.append(_dot(e.astype(BF16), vals16[:, g * HEAD_DIM:(g + 1) * HEAD_DIM]) * (1.0 / den))
        for hh in range(Q_PER_KV):
            pair = jnp.concatenate([o[hh * n_new:(hh + 1) * n_new] for o in outs], axis=1)
            mix_buf[rows, hh * LANES:(hh + 1) * LANES] = pair.astype(BF16)
        ufull[s_ * 2 * n_new:s_ * 2 * n_new + n_new, :] = pprev_ref[s_]
        ufull[s_ * 2 * n_new + n_new:(s_ + 1) * 2 * n_new, :] = u[rows]

    uf = ufull[...]
    for gi, w in enumerate(POOL_WINDOWS):
        sl = slice(gi * POOL_GROUP, (gi + 1) * POOL_GROUP)
        ws = _window_sum(uf[:, sl], w)
        tsum = jnp.concatenate(
            [ws[s_ * 2 * n_new + n_new:(s_ + 1) * 2 * n_new] for s_ in range(n_streams)], axis=0)
        d = (tsum / float(w) - u[:, sl]).astype(BF16)
        pool = _dot(d, wpool_ref[gi]) * pscale_ref[:, sl]
        mix_buf[:, D_ATTN + gi * POOL_GROUP:D_ATTN + (gi + 1) * POOL_GROUP] = pool.astype(BF16)

    x1 = x + _dot(mix_buf[...], wout_ref[...])
    xn2 = _rms_rows(x1, gffn_ref[...]).astype(BF16)

    t_in_stream = lax.broadcasted_iota(jnp.int32, (M, 1), 0) % n_new

    def taps(h, cs):
        return (h, jnp.where(t_in_stream >= 1, pltpu.roll(h, 1, 0), c1_ref[:, cs]),
                jnp.where(t_in_stream >= 2, pltpu.roll(h, 2, 0), c2_ref[:, cs]))

    acc = x1
    for c in range(d_ff // FF_CHUNK):
        cg = slice(c * FF_CHUNK, (c + 1) * FF_CHUNK)
        cv = slice(d_ff + c * FF_CHUNK, d_ff + (c + 1) * FF_CHUNK)
        hg = _dot(xn2, wup_ref[:, cg])
        hv = _dot(xn2, wup_ref[:, cv])
        hout_ref[:, cg] = hg
        hout_ref[:, cv] = hv
        act = _conv_gate(taps(hg, cg), taps(hv, cv), cw_ref[:, cg], cw_ref[:, cv],
                         cb_ref[:, cg], cb_ref[:, cv])
        acc = acc + _dot(act, wdown_ref[cg, :])
    y_ref[...] = acc


def _sample_layer(x, tabs, cache_k, cache_v, pool_prev, conv_prev, w):
    S, n_new, D = x.shape
    M = S * n_new
    d_ff = w["w_down"].shape[1]
    past = cache_k.shape[1]
    pprev = jnp.pad(pool_prev, ((0, 0), (n_new - pool_prev.shape[1], 0), (0, 0)))
    c1 = jnp.pad(conv_prev[:, 1:2], ((0, 0), (0, n_new - 1), (0, 0))).reshape(M, 2 * d_ff)
    c2 = jnp.pad(conv_prev, ((0, 0), (0, n_new - 2), (0, 0))).reshape(M, 2 * d_ff)
    acts = (x.reshape(M, D), tabs[0], tabs[1], tabs[2],
            cache_k.reshape(S, past, D_KV), cache_v.reshape(S, past, D_KV), pprev, c1, c2)
    out_shape = (
        jax.ShapeDtypeStruct((M, D), F32),
        jax.ShapeDtypeStruct((S, WINDOW, D_KV), F32),
        jax.ShapeDtypeStruct((S, WINDOW, D_KV), F32),
        jax.ShapeDtypeStruct((M, D_POOL), F32),
        jax.ShapeDtypeStruct((M, 2 * d_ff), F32),
    )
    y, ko, vo, uo, ho = pl.pallas_call(
        _sample_kernel,
        out_shape=out_shape,
        grid=(1,),
        in_specs=([pl.BlockSpec(memory_space=pltpu.SMEM)] + [_const_spec(a.shape) for a in acts]
                  + _weight_specs(w, D, d_ff)),
        out_specs=tuple(_const_spec(o.shape) for o in out_shape),
        scratch_shapes=[pltpu.VMEM((2 * M, D_POOL), F32), pltpu.VMEM((M, D_ATTN + D_POOL), BF16)],
        name="sample_layer",
        compiler_params=pltpu.CompilerParams(
            dimension_semantics=("arbitrary",), vmem_limit_bytes=VMEM_LIMIT_BYTES),
    )(w["sinks"], *acts, *_weight_args(w))
    return (y.reshape(S, n_new, D), ko, vo, uo.reshape(S, n_new, D_POOL),
            ho.reshape(S, n_new, 2 * d_ff))


def _rope_tables(pos, reps=1):
    half = ROT_DIM // 2
    inv = ROPE_THETA ** (-jnp.arange(0, ROT_DIM, 2, dtype=F32) / ROT_DIM)
    ang = pos.astype(F32)[:, None] * inv[None, :]
    cos, sin = jnp.cos(ang), jnp.sin(ang)
    n = pos.shape[0]
    ones = jnp.ones((n, HEAD_DIM - ROT_DIM), F32)
    zeros_h = jnp.zeros((n, half), F32)
    zeros_r = jnp.zeros((n, HEAD_DIM - ROT_DIM), F32)
    c = jnp.concatenate([cos, cos, ones], axis=1)
    sa = jnp.concatenate([-sin, zeros_h, zeros_r], axis=1)
    sb = jnp.concatenate([zeros_h, sin, zeros_r], axis=1)
    tile = lambda t: jnp.tile(t, (reps, LANES // HEAD_DIM))
    return tile(c), tile(sa), tile(sb)


def kernel(x_prompt, x_sample, cache_k, cache_v, state_pool, state_conv, norm_mix, w_in, q_norm,
           k_norm, attn_sinks, w_pool, pool_scale, w_out, norm_ffn, w_up, conv_w, conv_b, w_down):
    depth = w_in.shape[0]
    B, L, D = x_prompt.shape
    S, n_new, _ = x_sample.shape
    past_len = L
    assert L % PROMPT_TILE == 0 and PROMPT_TILE % CHUNK == 0 and PROMPT_TILE >= WINDOW
    assert w_down.shape[1] % FF_CHUNK == 0

    tabs_p = _rope_tables(jnp.arange(L))
    tabs_s = _rope_tables(past_len + jnp.arange(n_new), reps=S)
    head_id = jnp.arange(MXU_COLS) // HEAD_DIM
    bd = (head_id[:, None] == head_id[None, :]).astype(BF16)
    hh_, g_, d_ = jnp.meshgrid(jnp.arange(Q_PER_KV), jnp.arange(N_KV_HEADS), jnp.arange(HEAD_DIM),
                               indexing="ij")
    head_perm = ((g_ * Q_PER_KV + hh_) * HEAD_DIM + d_).reshape(-1)

    col_perm = jnp.concatenate([head_perm, jnp.arange(D_ATTN, w_in.shape[2])])
    row_perm = jnp.concatenate([head_perm, jnp.arange(D_ATTN, w_out.shape[1])])
    w_in_b = w_in.astype(BF16)[:, :, col_perm]
    w_out_b = w_out.astype(BF16)[:, row_perm, :]
    w_up_b = w_up.astype(BF16)
    w_down_b = w_down.astype(BF16)
    eye = jnp.eye(MXU_COLS, dtype=BF16)

    yp, ys = x_prompt, x_sample
    outs = [[] for _ in range(8)]
    for i in range(depth):
        w = dict(
            layer=i, sinks=attn_sinks[i],
            g_mix=norm_mix[i][None, :], w_in=w_in_b,
            q_gain=jnp.tile(q_norm[i], N_Q_HEADS)[None, :], k_gain=jnp.tile(k_norm[i], N_KV_HEADS)[None, :],
            bd=bd, eye=eye, w_pool=w_pool[i].astype(BF16), pool_scale=pool_scale[i][None, :],
            w_out=w_out_b, g_ffn=norm_ffn[i][None, :], w_up=w_up_b,
            conv_w=conv_w[i], conv_b=conv_b[i][None, :], w_down=w_down_b,
        )
        yp, k1, v1, u1, h1 = _prompt_layer(yp, tabs_p, w)
        ys, k2, v2, u2, h2 = _sample_layer(ys, tabs_s, cache_k[i], cache_v[i], state_pool[i],
                                           state_conv[i], w)
        n_pool = state_pool.shape[2]
        n_conv = state_conv.shape[2]
        outs[0].append(k1.reshape(B, WINDOW, N_KV_HEADS, HEAD_DIM))
        outs[1].append(v1.reshape(B, WINDOW, N_KV_HEADS, HEAD_DIM))
        outs[2].append(u1[:, POOL_HALO - n_pool:])
        outs[3].append(h1[:, CONV_HALO - n_conv:])
        outs[4].append(k2.reshape(S, WINDOW, N_KV_HEADS, HEAD_DIM))
        outs[5].append(v2.reshape(S, WINDOW, N_KV_HEADS, HEAD_DIM))
        outs[6].append(u2[:, n_new - n_pool:])
        outs[7].append(h2[:, n_new - n_conv:])
    return (yp, ys) + tuple(jnp.stack(o) for o in outs)
```

```python
import functools

import jax
import jax.numpy as jnp
from jax import lax
from jax.experimental import pallas as pl
from jax.experimental.pallas import tpu as pltpu

F32 = jnp.float32
BF16 = jnp.bfloat16

CHUNK = 64
HEAD_DIM = 64
N_Q_HEADS = 8
N_KV_HEADS = 2
Q_PER_KV = N_Q_HEADS // N_KV_HEADS
D_ATTN = N_Q_HEADS * HEAD_DIM
D_KV = N_KV_HEADS * HEAD_DIM
WINDOW = 128
ROT_DIM = 16
ROPE_THETA = 500000.0
POOL_WINDOWS = (2, 4, 8, 16)
POOL_GROUP = 128
D_POOL = POOL_GROUP * len(POOL_WINDOWS)
POOL_HALO = 16
CONV_W = 3
CONV_HALO = 8
EPS = 1e-6
NEG_INF = -1e30

LANES = 128
MXU_COLS = 256
ATT_KEYS = 256
VT_ROWS = D_KV + 16

PROMPT_TILE = 512
FF_CHUNK = 256
DOWN_GROUP = 3
DOWN_LAG = 2
N_PREP = 16
VMEM_LIMIT_BYTES = 56 * 1024 * 1024


def _dot(a, b):
    return jnp.dot(a, b, preferred_element_type=F32)


def _dot_nt(a, b):
    return lax.dot_general(a, b, (((1,), (1,)), ((), ())), preferred_element_type=F32)


def _dot_tn(a, b):
    return lax.dot_general(a, b, (((0,), (0,)), ((), ())), preferred_element_type=F32)


def _rms_rows(x, gain):
    ms = jnp.mean(x * x, axis=-1, keepdims=True)
    return x * lax.rsqrt(ms + EPS) * gain


def _head_rms(t, bd, gain):
    ss = _dot((t * t).astype(BF16), bd)
    return t * lax.rsqrt(ss * (1.0 / HEAD_DIM) + EPS) * gain


def _rope(t, cos, sin_a, sin_b):
    return (t * cos + pltpu.roll(t, LANES - ROT_DIM // 2, 1) * sin_a
            + pltpu.roll(t, ROT_DIM // 2, 1) * sin_b)


def _mixer_inputs(x, gmix, w_in, qg, kg, bd, cos, sin_a, sin_b):
    xn = _rms_rows(x, gmix).astype(BF16)
    h = _dot(xn, w_in)
    q_parts = []
    for j in range(D_ATTN // MXU_COLS):
        qb = _head_rms(h[:, j * MXU_COLS:(j + 1) * MXU_COLS], bd,
                       qg[:, j * MXU_COLS:(j + 1) * MXU_COLS])
        for l in range(MXU_COLS // LANES):
            q_parts.append(_rope(qb[:, l * LANES:(l + 1) * LANES], cos, sin_a, sin_b))
    q = jnp.concatenate(q_parts, axis=1) * (HEAD_DIM ** -0.5)
    k = _head_rms(h[:, D_ATTN:D_ATTN + D_KV], bd[:D_KV, :D_KV], kg)
    k = _rope(k, cos, sin_a, sin_b)
    v = h[:, D_ATTN + D_KV:D_ATTN + 2 * D_KV]
    u = h[:, D_ATTN + 2 * D_KV:]
    return q, k, v, u


def _window_sum(a, w):
    s = 1
    while s < w:
        a = a + pltpu.roll(a, s, 0)
        s *= 2
    return a


def _conv_gate(hg, hv, cw_g, cw_v, cb_g, cb_v):
    cg = cb_g + hg[2] * cw_g[0:1] + hg[1] * cw_g[1:2] + hg[0] * cw_g[2:3]
    cv = cb_v + hv[2] * cw_v[0:1] + hv[1] * cw_v[1:2] + hv[0] * cw_v[2:3]
    return ((cg / (1.0 + jnp.exp(-cg))) * cv).astype(BF16)


def _pair_heads(qcols):
    lane = lax.broadcasted_iota(jnp.int32, (1, LANES), 1)
    blocks = [qcols[:, b * LANES:(b + 1) * LANES] for b in range(D_ATTN // LANES)]
    per_block = LANES // HEAD_DIM
    out = []
    for hh in range(Q_PER_KV):
        a = blocks[hh // per_block]
        b = blocks[Q_PER_KV // per_block + hh // per_block]
        if hh % per_block == 0:
            out.append(jnp.where(lane < HEAD_DIM, a, pltpu.roll(b, HEAD_DIM, 1)))
        else:
            out.append(jnp.where(lane < HEAD_DIM, pltpu.roll(a, HEAD_DIM, 1), b))
    return jnp.concatenate(out, axis=1)


def _stage_weights(s, win_c, wout_c, wup_c, wdown_c, w_in_s, w_out_s, w_up_s, w_down_s):
    r_in = win_c.shape[0]
    r0 = pl.multiple_of(s * r_in, r_in)
    c = win_c[...]
    w_in_s[pl.ds(r0, r_in), :] = jnp.concatenate(
        [_pair_heads(c[:, :D_ATTN]), c[:, D_ATTN:]], axis=1).astype(BF16)
    w_up_s[pl.ds(r0, r_in), :] = wup_c[...].astype(BF16)
    r_dn = wdown_c.shape[0]
    w_down_s[pl.ds(pl.multiple_of(s * r_dn, r_dn), r_dn), :] = wdown_c[...].astype(BF16)
    dst = jnp.where(s < N_Q_HEADS, (s % Q_PER_KV) * N_KV_HEADS + s // Q_PER_KV, s)
    w_out_s[pl.ds(pl.multiple_of(dst * HEAD_DIM, HEAD_DIM), HEAD_DIM), :] = wout_c[...].astype(BF16)


def _prompt_kernel(sinks_ref, x_ref, cos_ref, sa_ref, sb_ref,
                   gmix_ref, win_c, qg_ref, kg_ref, bd_ref, eye_ref, wpool_ref, pscale_ref, wout_c,
                   gffn_ref, wup_c, cw_ref, cb_ref, wdown_c,
                   y_ref, klast_ref, vlast_ref, ulast_ref, hlast_ref,
                   w_in_s, w_out_s, w_up_s, w_down_s, *scratch, n_tiles):
    s = pl.program_id(0)

    @pl.when(s < N_PREP)
    def _():
        _stage_weights(s, win_c, wout_c, wup_c, wdown_c, w_in_s, w_out_s, w_up_s, w_down_s)

    @pl.when(s >= N_PREP)
    def _():
        _prompt_body((s - N_PREP) % n_tiles, sinks_ref, x_ref, cos_ref, sa_ref, sb_ref,
                     gmix_ref, w_in_s, qg_ref, kg_ref, bd_ref, eye_ref, wpool_ref, pscale_ref, w_out_s,
                     gffn_ref, w_up_s, cw_ref, cb_ref, w_down_s,
                     y_ref, klast_ref, vlast_ref, ulast_ref, hlast_ref, *scratch)


def _prompt_body(i, sinks_ref, x_ref, cos_ref, sa_ref, sb_ref,
                 gmix_ref, win_ref, qg_ref, kg_ref, bd_ref, eye_ref, wpool_ref, pscale_ref, wout_ref,
                 gffn_ref, wup_ref, cw_ref, cb_ref, wdown_ref,
                 y_ref, klast_ref, vlast_ref, ulast_ref, hlast_ref,
                 kfull, vfull, vt, ufull, hprev, hbuf, mix_buf, act_buf):
    T = x_ref.shape[0]
    d_ff = wdown_ref.shape[0]
    n_chunks = T // CHUNK

    @pl.when(i == 0)
    def _():
        kfull[0:WINDOW, :] = jnp.zeros((WINDOW, D_KV), BF16)
        kfull[WINDOW + T:, :] = jnp.zeros((ATT_KEYS - WINDOW - CHUNK, D_KV), BF16)
        vfull[0:WINDOW, :] = jnp.zeros((WINDOW, D_KV), BF16)
        vfull[WINDOW + T:, :] = jnp.zeros((ATT_KEYS - WINDOW - CHUNK, D_KV), BF16)
        for c in range(2):
            vt[c, D_KV:, :] = jnp.ones((VT_ROWS - D_KV, vt.shape[2]), BF16)
        ufull[0:POOL_HALO, :] = jnp.zeros((POOL_HALO, D_POOL), F32)
        hprev[...] = jnp.zeros(hprev.shape, F32)

    x = x_ref[...]
    q, k, v, u = _mixer_inputs(x, gmix_ref[...], win_ref[...], qg_ref[...], kg_ref[...],
                               bd_ref[...], cos_ref[...], sa_ref[...], sb_ref[...])
    klast_ref[...] = k[T - WINDOW:, :]
    vlast_ref[...] = v[T - WINDOW:, :]
    ulast_ref[...] = u[T - POOL_HALO:, :]
    kfull[WINDOW:WINDOW + T, :] = k.astype(BF16)
    vfull[WINDOW:WINDOW + T, :] = v.astype(BF16)
    ufull[POOL_HALO:POOL_HALO + T, :] = u

    eye = eye_ref[...]
    for c in range(2):
        vrows = vfull[c * CHUNK:c * CHUNK + T + WINDOW, :]
        vt[c, 0:D_KV, :] = _dot_nt(eye[:D_KV, :D_KV], vrows).astype(BF16)
    band = WINDOW + CHUNK
    n_q = N_Q_HEADS * CHUNK
    lane = lax.broadcasted_iota(jnp.int32, (1, D_ATTN), 1)
    head_of_col = lax.broadcasted_iota(jnp.int32, (1, n_q), 1) // CHUNK
    key_row = lax.broadcasted_iota(jnp.int32, (band, 1), 0)
    q_of_group = []
    for g in range(N_KV_HEADS):
        in_g = ((lane // HEAD_DIM) % N_KV_HEADS == g).astype(F32)
        q_of_group.append((q * in_g).astype(BF16))
    sink_row = jnp.full((1, n_q), sinks_ref[0], F32)
    for h8 in range(1, N_Q_HEADS):
        sink_row = jnp.where(head_of_col == h8, sinks_ref[h8], sink_row)
    half = Q_PER_KV * CHUNK
    scores = []
    for j in range(n_chunks):
        r0 = j * CHUNK
        qs = jnp.concatenate(
            [q_of_group[g][r0:r0 + CHUNK, hh * LANES:(hh + 1) * LANES]
             for g in range(N_KV_HEADS) for hh in range(Q_PER_KV)], axis=0)
        scores.append(_dot_nt(kfull[r0:r0 + band, :], qs))
    probs = []
    for j, s in enumerate(scores):
        r0 = j * CHUNK
        if r0 < WINDOW:
            first_valid = jnp.where(i == 0, WINDOW - r0, 0)
            s = jnp.where(key_row >= first_valid, s, NEG_INF)
        m = jnp.maximum(jnp.max(s, axis=0, keepdims=True), sink_row)
        probs.append((jnp.exp(s - m).astype(BF16), jnp.exp(sink_row - m)))
    outs_t = []
    for j, (e, sink_e) in enumerate(probs):
        vtb = vt[j % 2, :, (j // 2) * LANES:(j // 2) * LANES + band]
        oa = _dot(vtb, e)
        inv_den = 1.0 / (oa[D_KV:D_KV + 1] + sink_e)
        outs_t.append(jnp.concatenate(
            [oa[g * HEAD_DIM:(g + 1) * HEAD_DIM, g * half:(g + 1) * half]
             * inv_den[:, g * half:(g + 1) * half] for g in range(N_KV_HEADS)],
            axis=0).astype(BF16))
    for j, ot in enumerate(outs_t):
        r0 = j * CHUNK
        o = _dot_nt(eye, ot)
        for hh in range(Q_PER_KV):
            mix_buf[r0:r0 + CHUNK, hh * LANES:(hh + 1) * LANES] = (
                o[hh * CHUNK:(hh + 1) * CHUNK].astype(BF16))

    uf = ufull[...]
    pos = i * T + lax.broadcasted_iota(jnp.int32, (T, 1), 0)
    for gi, w in enumerate(POOL_WINDOWS):
        sl = slice(gi * POOL_GROUP, (gi + 1) * POOL_GROUP)
        tsum = _window_sum(uf[:, sl], w)[POOL_HALO:]
        cnt = jnp.minimum(pos + 1, w).astype(F32)
        d = (tsum / cnt - u[:, sl]).astype(BF16)
        pool = _dot(d, wpool_ref[gi].astype(BF16)) * pscale_ref[:, sl]
        mix_buf[:, D_ATTN + gi * POOL_GROUP:D_ATTN + (gi + 1) * POOL_GROUP] = pool.astype(BF16)

    kfull[0:WINDOW, :] = kfull[T:T + WINDOW, :]
    vfull[0:WINDOW, :] = vfull[T:T + WINDOW, :]
    ufull[0:POOL_HALO, :] = u[T - POOL_HALO:, :]

    x1 = x + _dot(mix_buf[...], wout_ref[...])

    xn2 = _rms_rows(x1, gffn_ref[...]).astype(BF16)
    F = FF_CHUNK
    y, k_done = x1, 0
    for c in range(d_ff // F):
        cols = (slice(c * F, (c + 1) * F), slice(d_ff + c * F, d_ff + (c + 1) * F))
        hb = hbuf.at[c % 2]
        taps = []
        for half, cs in enumerate(cols):
            hs = slice(half * F, (half + 1) * F)
            h = _dot(xn2, wup_ref[:, cs])
            hb[0:CONV_HALO, hs] = hprev[:, cs]
            hb[CONV_HALO:CONV_HALO + T, hs] = h
            hlast_ref[:, cs] = h[T - CONV_HALO:, :]
            hprev[:, cs] = h[T - CONV_HALO:, :]
            taps.append((h, hb[CONV_HALO - 1:CONV_HALO - 1 + T, hs],
                         hb[CONV_HALO - 2:CONV_HALO - 2 + T, hs]))
        act_buf[:, cols[0]] = _conv_gate(taps[0], taps[1], cw_ref[:, cols[0]], cw_ref[:, cols[1]],
                                         cb_ref[:, cols[0]], cb_ref[:, cols[1]])
        c_done = c - DOWN_LAG
        if c_done >= 0 and (c_done + 1) % DOWN_GROUP == 0:
            ks = slice((c_done + 1 - DOWN_GROUP) * F, (c_done + 1) * F)
            y = y + _dot(act_buf[:, ks], wdown_ref[ks, :])
            k_done = (c_done + 1) * F
    y_ref[...] = y + _dot(act_buf[:, k_done:], wdown_ref[k_done:, :])


def _const_spec(shape):
    nd = len(shape)
    return pl.BlockSpec(shape, lambda *_: (0,) * nd)


def _chunk_spec(stacked, layer):
    rows = stacked.shape[1] // N_PREP
    return pl.BlockSpec((None, rows, stacked.shape[2]),
                        lambda s: (layer, jnp.minimum(s, N_PREP - 1), 0))


def _weight_specs(w, D, d_ff):
    layer = w["layer"]
    return [
        _const_spec((1, D)), _chunk_spec(w["w_in"], layer),
        _const_spec((1, D_ATTN)), _const_spec((1, D_KV)), _const_spec((MXU_COLS, MXU_COLS)),
        _const_spec((MXU_COLS, MXU_COLS)),
        _const_spec(w["w_pool"].shape), _const_spec((1, D_POOL)), _chunk_spec(w["w_out"], layer),
        _const_spec((1, D)), _chunk_spec(w["w_up"], layer),
        _const_spec((CONV_W, 2 * d_ff)), _const_spec((1, 2 * d_ff)), _chunk_spec(w["w_down"], layer),
    ]


def _weight_args(w):
    return (w["g_mix"], w["w_in"], w["q_gain"], w["k_gain"], w["bd"], w["eye"], w["w_pool"],
            w["pool_scale"], w["w_out"], w["g_ffn"], w["w_up"], w["conv_w"], w["conv_b"], w["w_down"])


def _weight_scratch(w):
    return [pltpu.VMEM(w[name].shape[1:], BF16) for name in ("w_in", "w_out", "w_up", "w_down")]


def _prompt_layer(x, tabs, w):
    B, L, D = x.shape
    T = PROMPT_TILE
    d_ff = w["w_down"].shape[1]
    n_tiles = L // T
    tile_of = lambda s: jnp.maximum(s - N_PREP, 0)
    tile_spec = pl.BlockSpec((None, T, D), lambda s: (tile_of(s) // n_tiles, tile_of(s) % n_tiles, 0))
    tab_spec = pl.BlockSpec((T, LANES), lambda s: (tile_of(s) % n_tiles, 0))

    def last_spec(rows, cols):
        return pl.BlockSpec((None, rows, cols), lambda s: (tile_of(s) // n_tiles, 0, 0))

    in_specs = [
        pl.BlockSpec(memory_space=pltpu.SMEM),
        tile_spec, tab_spec, tab_spec, tab_spec,
    ] + _weight_specs(w, D, d_ff)
    out_shape = (
        jax.ShapeDtypeStruct((B, L, D), F32),
        jax.ShapeDtypeStruct((B, WINDOW, D_KV), F32),
        jax.ShapeDtypeStruct((B, WINDOW, D_KV), F32),
        jax.ShapeDtypeStruct((B, POOL_HALO, D_POOL), F32),
        jax.ShapeDtypeStruct((B, CONV_HALO, 2 * d_ff), F32),
    )
    out_specs = (tile_spec, last_spec(WINDOW, D_KV), last_spec(WINDOW, D_KV),
                 last_spec(POOL_HALO, D_POOL), last_spec(CONV_HALO, 2 * d_ff))
    scratch = _weight_scratch(w) + [
        pltpu.VMEM((T + ATT_KEYS - CHUNK, D_KV), BF16),
        pltpu.VMEM((T + ATT_KEYS - CHUNK, D_KV), BF16),
        pltpu.VMEM((2, VT_ROWS, T + WINDOW), BF16),
        pltpu.VMEM((POOL_HALO + T, D_POOL), F32),
        pltpu.VMEM((CONV_HALO, 2 * d_ff), F32),
        pltpu.VMEM((2, CONV_HALO + T, 2 * FF_CHUNK), F32),
        pltpu.VMEM((T, D_ATTN + D_POOL), BF16),
        pltpu.VMEM((T, d_ff), BF16),
    ]
    return pl.pallas_call(
        functools.partial(_prompt_kernel, n_tiles=n_tiles),
        out_shape=out_shape,
        grid=(N_PREP + B * n_tiles,),
        in_specs=in_specs,
        out_specs=out_specs,
        scratch_shapes=scratch,
        name="prompt_layer",
        compiler_params=pltpu.CompilerParams(
            dimension_semantics=("arbitrary",),
            vmem_limit_bytes=VMEM_LIMIT_BYTES),
    )(w["sinks"], x, tabs[0], tabs[1], tabs[2], *_weight_args(w))


def _sample_kernel(sinks_ref, x_ref, cos_ref, sa_ref, sb_ref, ck_ref, cv_ref, pprev_ref,
                   c1_ref, c2_ref,
                   gmix_ref, win_c, qg_ref, kg_ref, bd_ref, eye_ref, wpool_ref, pscale_ref, wout_c,
                   gffn_ref, wup_c, cw_ref, cb_ref, wdown_c,
                   y_ref, kout_ref, vout_ref, uout_ref, hout_ref,
                   w_in_s, w_out_s, w_up_s, w_down_s, *scratch):
    s = pl.program_id(0)

    @pl.when(s < N_PREP)
    def _():
        _stage_weights(s, win_c, wout_c, wup_c, wdown_c, w_in_s, w_out_s, w_up_s, w_down_s)

    @pl.when(s >= N_PREP)
    def _():
        _sample_body(sinks_ref, x_ref, cos_ref, sa_ref, sb_ref, ck_ref, cv_ref, pprev_ref,
                     c1_ref, c2_ref,
                     gmix_ref, w_in_s, qg_ref, kg_ref, bd_ref, eye_ref, wpool_ref, pscale_ref, w_out_s,
                     gffn_ref, w_up_s, cw_ref, cb_ref, w_down_s,
                     y_ref, kout_ref, vout_ref, uout_ref, hout_ref, *scratch)


def _sample_body(sinks_ref, x_ref, cos_ref, sa_ref, sb_ref, ck_ref, cv_ref, pprev_ref,
                 c1_ref, c2_ref,
                 gmix_ref, win_ref, qg_ref, kg_ref, bd_ref, eye_ref, wpool_ref, pscale_ref, wout_ref,
                 gffn_ref, wup_ref, cw_ref, cb_ref, wdown_ref,
                 y_ref, kout_ref, vout_ref, uout_ref, hout_ref,
                 ufull, mix_buf):
    M = x_ref.shape[0]
    n_streams = ck_ref.shape[0]
    n_new = M // n_streams
    past = ck_ref.shape[1]
    d_ff = wdown_ref.shape[0]

    x = x_ref[...]
    q, k, v, u = _mixer_inputs(x, gmix_ref[...], win_ref[...], qg_ref[...], kg_ref[...],
                               bd_ref[...], cos_ref[...], sa_ref[...], sb_ref[...])
    q = q.astype(BF16)
    uout_ref[...] = u

    for s_ in range(n_streams):
        rows = slice(s_ * n_new, (s_ + 1) * n_new)
        keys = jnp.concatenate([ck_ref[s_], k[rows]], axis=0)
        vals = jnp.concatenate([cv_ref[s_], v[rows]], axis=0)
        kout_ref[s_] = keys[past + n_new - WINDOW:, :]
        vout_ref[s_] = vals[past + n_new - WINDOW:, :]
        keys16 = keys.astype(BF16)
        vals16 = vals.astype(BF16)
        outs = []
        for g in range(N_KV_HEADS):
            sink_col = jnp.concatenate(
                [jnp.full((n_new, 1), sinks_ref[g * Q_PER_KV + hh], F32) for hh in range(Q_PER_KV)],
                axis=0)
            qs = jnp.concatenate(
                [q[rows, hh * LANES + g * HEAD_DIM:hh * LANES + (g + 1) * HEAD_DIM]
                 for hh in range(Q_PER_KV)], axis=0)
            sc = _dot_nt(qs, keys16[:, g * HEAD_DIM:(g + 1) * HEAD_DIM])
            m = jnp.maximum(jnp.max(sc, axis=-1, keepdims=True), sink_col)
            e = jnp.exp(sc - m)
            den = jnp.sum(e, axis=-1, keepdims=True) + jnp.exp(sink_col - m)
            outs.append(_dot(e.astype(BF16), vals16[:, g * HEAD_DIM:(g + 1) * HEAD_DIM]) * (1.0 / den))
        for hh in range(Q_PER_KV):
            pair = jnp.concatenate([o[hh * n_new:(hh + 1) * n_new] for o in outs], axis=1)
            mix_buf[rows, hh * LANES:(hh + 1) * LANES] = pair.astype(BF16)
        ufull[s_ * 2 * n_new:s_ * 2 * n_new + n_new, :] = pprev_ref[s_]
        ufull[s_ * 2 * n_new + n_new:(s_ + 1) * 2 * n_new, :] = u[rows]

    uf = ufull[...]
    for gi, w in enumerate(POOL_WINDOWS):
        sl = slice(gi * POOL_GROUP, (gi + 1) * POOL_GROUP)
        ws = _window_sum(uf[:, sl], w)
        tsum = jnp.concatenate(
            [ws[s_ * 2 * n_new + n_new:(s_ + 1) * 2 * n_new] for s_ in range(n_streams)], axis=0)
        d = (tsum / float(w) - u[:, sl]).astype(BF16)
        pool = _dot(d, wpool_ref[gi].astype(BF16)) * pscale_ref[:, sl]
        mix_buf[:, D_ATTN + gi * POOL_GROUP:D_ATTN + (gi + 1) * POOL_GROUP] = pool.astype(BF16)

    x1 = x + _dot(mix_buf[...], wout_ref[...])
    xn2 = _rms_rows(x1, gffn_ref[...]).astype(BF16)

    t_in_stream = lax.broadcasted_iota(jnp.int32, (M, 1), 0) % n_new

    def taps(h, cs):
        return (h, jnp.where(t_in_stream >= 1, pltpu.roll(h, 1, 0), c1_ref[:, cs]),
                jnp.where(t_in_stream >= 2, pltpu.roll(h, 2, 0), c2_ref[:, cs]))

    acc = x1
    for c in range(d_ff // FF_CHUNK):
        cg = slice(c * FF_CHUNK, (c + 1) * FF_CHUNK)
        cv = slice(d_ff + c * FF_CHUNK, d_ff + (c + 1) * FF_CHUNK)
        hg = _dot(xn2, wup_ref[:, cg])
        hv = _dot(xn2, wup_ref[:, cv])
        hout_ref[:, cg] = hg
        hout_ref[:, cv] = hv
        act = _conv_gate(taps(hg, cg), taps(hv, cv), cw_ref[:, cg], cw_ref[:, cv],
                         cb_ref[:, cg], cb_ref[:, cv])
        acc = acc + _dot(act, wdown_ref[cg, :])
    y_ref[...] = acc


def _sample_layer(x, tabs, cache_k, cache_v, pool_prev, conv_prev, w):
    S, n_new, D = x.shape
    M = S * n_new
    d_ff = w["w_down"].shape[1]
    past = cache_k.shape[1]
    pprev = jnp.pad(pool_prev, ((0, 0), (n_new - pool_prev.shape[1], 0), (0, 0)))
    c1 = jnp.pad(conv_prev[:, 1:2], ((0, 0), (0, n_new - 1), (0, 0))).reshape(M, 2 * d_ff)
    c2 = jnp.pad(conv_prev, ((0, 0), (0, n_new - 2), (0, 0))).reshape(M, 2 * d_ff)
    acts = (x.reshape(M, D), tabs[0], tabs[1], tabs[2],
            cache_k.reshape(S, past, D_KV), cache_v.reshape(S, past, D_KV), pprev, c1, c2)
    out_shape = (
        jax.ShapeDtypeStruct((M, D), F32),
        jax.ShapeDtypeStruct((S, WINDOW, D_KV), F32),
        jax.ShapeDtypeStruct((S, WINDOW, D_KV), F32),
        jax.ShapeDtypeStruct((M, D_POOL), F32),
        jax.ShapeDtypeStruct((M, 2 * d_ff), F32),
    )
    y, ko, vo, uo, ho = pl.pallas_call(
        _sample_kernel,
        out_shape=out_shape,
        grid=(N_PREP + 1,),
        in_specs=([pl.BlockSpec(memory_space=pltpu.SMEM)] + [_const_spec(a.shape) for a in acts]
                  + _weight_specs(w, D, d_ff)),
        out_specs=tuple(_const_spec(o.shape) for o in out_shape),
        scratch_shapes=_weight_scratch(w) + [pltpu.VMEM((2 * M, D_POOL), F32),
                                             pltpu.VMEM((M, D_ATTN + D_POOL), BF16)],
        name="sample_layer",
        compiler_params=pltpu.CompilerParams(
            dimension_semantics=("arbitrary",), vmem_limit_bytes=VMEM_LIMIT_BYTES),
    )(w["sinks"], *acts, *_weight_args(w))
    return (y.reshape(S, n_new, D), ko, vo, uo.reshape(S, n_new, D_POOL),
            ho.reshape(S, n_new, 2 * d_ff))


def _rope_tables(pos, reps=1):
    half = ROT_DIM // 2
    inv = ROPE_THETA ** (-jnp.arange(0, ROT_DIM, 2, dtype=F32) / ROT_DIM)
    ang = pos.astype(F32)[:, None] * inv[None, :]
    cos, sin = jnp.cos(ang), jnp.sin(ang)
    n = pos.shape[0]
    ones = jnp.ones((n, HEAD_DIM - ROT_DIM), F32)
    zeros_h = jnp.zeros((n, half), F32)
    zeros_r = jnp.zeros((n, HEAD_DIM - ROT_DIM), F32)
    c = jnp.concatenate([cos, cos, ones], axis=1)
    sa = jnp.concatenate([-sin, zeros_h, zeros_r], axis=1)
    sb = jnp.concatenate([zeros_h, sin, zeros_r], axis=1)
    tile = lambda t: jnp.tile(t, (reps, LANES // HEAD_DIM))
    return tile(c), tile(sa), tile(sb)


def kernel(x_prompt, x_sample, cache_k, cache_v, state_pool, state_conv, norm_mix, w_in, q_norm,
           k_norm, attn_sinks, w_pool, pool_scale, w_out, norm_ffn, w_up, conv_w, conv_b, w_down):
    depth = w_in.shape[0]
    B, L, D = x_prompt.shape
    S, n_new, _ = x_sample.shape
    past_len = L
    assert L % PROMPT_TILE == 0 and PROMPT_TILE % CHUNK == 0 and PROMPT_TILE >= WINDOW
    assert w_down.shape[1] % FF_CHUNK == 0
    assert all(m.shape[1] % (16 * N_PREP) == 0 for m in (w_in, w_up, w_down))
    assert w_out.shape[1] == N_PREP * HEAD_DIM and LANES == N_KV_HEADS * HEAD_DIM

    tabs_p = _rope_tables(jnp.arange(L))
    tabs_s = _rope_tables(past_len + jnp.arange(n_new), reps=S)
    head_id = jnp.arange(MXU_COLS) // HEAD_DIM
    bd = (head_id[:, None] == head_id[None, :]).astype(BF16)

    eye = jnp.eye(MXU_COLS, dtype=BF16)

    yp, ys = x_prompt, x_sample
    outs = [[] for _ in range(8)]
    for i in range(depth):
        w = dict(
            layer=i, sinks=attn_sinks[i],
            g_mix=norm_mix[i][None, :], w_in=w_in,
            q_gain=jnp.tile(q_norm[i], N_Q_HEADS)[None, :], k_gain=jnp.tile(k_norm[i], N_KV_HEADS)[None, :],
            bd=bd, eye=eye, w_pool=w_pool[i], pool_scale=pool_scale[i][None, :],
            w_out=w_out, g_ffn=norm_ffn[i][None, :], w_up=w_up,
            conv_w=conv_w[i], conv_b=conv_b[i][None, :], w_down=w_down,
        )
        yp, k1, v1, u1, h1 = _prompt_layer(yp, tabs_p, w)
        ys, k2, v2, u2, h2 = _sample_layer(ys, tabs_s, cache_k[i], cache_v[i], state_pool[i],
                                           state_conv[i], w)
        n_pool = state_pool.shape[2]
        n_conv = state_conv.shape[2]
        outs[0].append(k1.reshape(B, WINDOW, N_KV_HEADS, HEAD_DIM))
        outs[1].append(v1.reshape(B, WINDOW, N_KV_HEADS, HEAD_DIM))
        outs[2].append(u1[:, POOL_HALO - n_pool:])
        outs[3].append(h1[:, CONV_HALO - n_conv:])
        outs[4].append(k2.reshape(S, WINDOW, N_KV_HEADS, HEAD_DIM))
        outs[5].append(v2.reshape(S, WINDOW, N_KV_HEADS, HEAD_DIM))
        outs[6].append(u2[:, n_new - n_pool:])
        outs[7].append(h2[:, n_new - n_conv:])
    return (yp, ys) + tuple(jnp.stack(o) for o in outs)
```

```python
import functools

import jax
import jax.numpy as jnp
from jax import lax
from jax.experimental import pallas as pl
from jax.experimental.pallas import tpu as pltpu

F32 = jnp.float32
BF16 = jnp.bfloat16

CHUNK = 64
HEAD_DIM = 64
N_Q_HEADS = 8
N_KV_HEADS = 2
Q_PER_KV = N_Q_HEADS // N_KV_HEADS
D_ATTN = N_Q_HEADS * HEAD_DIM
D_KV = N_KV_HEADS * HEAD_DIM
WINDOW = 128
ROT_DIM = 16
ROPE_THETA = 500000.0
POOL_WINDOWS = (2, 4, 8, 16)
POOL_GROUP = 128
D_POOL = POOL_GROUP * len(POOL_WINDOWS)
POOL_HALO = 16
CONV_W = 3
CONV_HALO = 8
EPS = 1e-6
NEG_INF = -1e30
NEG_LOG2_E = -1.4426950408889634

LANES = 128
MXU_COLS = 256
ATT_KEYS = 256
VT_ROWS = D_KV + 16

PROMPT_TILE = 512
FF_CHUNK = 256
DOWN_GROUP = 3
DOWN_LAG = 2
N_PREP = 16
VMEM_LIMIT_BYTES = 56 * 1024 * 1024


def _dot(a, b):
    return jnp.dot(a, b, preferred_element_type=F32)


def _dot_nt(a, b):
    return lax.dot_general(a, b, (((1,), (1,)), ((), ())), preferred_element_type=F32)


def _dot_tn(a, b):
    return lax.dot_general(a, b, (((0,), (0,)), ((), ())), preferred_element_type=F32)


def _rms_rows(x, gain):
    ms = jnp.mean(x * x, axis=-1, keepdims=True)
    return x * lax.rsqrt(ms + EPS) * gain


def _head_rms(t, bd, gain):
    ms = _dot((t * t).astype(BF16), bd)
    return t * lax.rsqrt(ms + EPS) * gain


def _rope(t, cos, sin_a, sin_b):
    return (t * cos + pltpu.roll(t, LANES - ROT_DIM // 2, 1) * sin_a
            + pltpu.roll(t, ROT_DIM // 2, 1) * sin_b)


def _mixer_inputs(x, gmix, w_in, qg, kg, bd, cos, sin_a, sin_b):
    xn = _rms_rows(x, gmix).astype(BF16)
    h = _dot(xn, w_in)
    q_parts = []
    for j in range(D_ATTN // MXU_COLS):
        qb = _head_rms(h[:, j * MXU_COLS:(j + 1) * MXU_COLS], bd,
                       qg[:, j * MXU_COLS:(j + 1) * MXU_COLS])
        for l in range(MXU_COLS // LANES):
            q_parts.append(_rope(qb[:, l * LANES:(l + 1) * LANES], cos, sin_a, sin_b))
    q = jnp.concatenate(q_parts, axis=1)
    k = _head_rms(h[:, D_ATTN:D_ATTN + D_KV], bd[:D_KV, :D_KV], kg)
    k = _rope(k, cos, sin_a, sin_b)
    v = h[:, D_ATTN + D_KV:D_ATTN + 2 * D_KV]
    u = h[:, D_ATTN + 2 * D_KV:]
    return q, k, v, u


def _window_sum(a, w):
    s = 1
    while s < w:
        a = a + pltpu.roll(a, s, 0)
        s *= 2
    return a


def _conv_gate(hg, hv, cw_g, cw_v, cb_g, cb_v):
    cg = cb_g + hg[2] * cw_g[0:1] + hg[1] * cw_g[1:2] + hg[0] * cw_g[2:3]
    cv = cb_v + hv[2] * cw_v[0:1] + hv[1] * cw_v[1:2] + hv[0] * cw_v[2:3]
    return ((cg / (1.0 + jnp.exp2(cg * NEG_LOG2_E))) * cv).astype(BF16)


def _pair_heads(qcols):
    lane = lax.broadcasted_iota(jnp.int32, (1, LANES), 1)
    blocks = [qcols[:, b * LANES:(b + 1) * LANES] for b in range(D_ATTN // LANES)]
    per_block = LANES // HEAD_DIM
    out = []
    for hh in range(Q_PER_KV):
        a = blocks[hh // per_block]
        b = blocks[Q_PER_KV // per_block + hh // per_block]
        if hh % per_block == 0:
            out.append(jnp.where(lane < HEAD_DIM, a, pltpu.roll(b, HEAD_DIM, 1)))
        else:
            out.append(jnp.where(lane < HEAD_DIM, pltpu.roll(a, HEAD_DIM, 1), b))
    return jnp.concatenate(out, axis=1)


def _stage_weights(s, win_c, wout_c, wup_c, wdown_c, w_in_s, w_out_s, w_up_s, w_down_s):
    r_in = win_c.shape[0]
    r0 = pl.multiple_of(s * r_in, r_in)
    c = win_c[...]
    w_in_s[pl.ds(r0, r_in), :] = jnp.concatenate(
        [_pair_heads(c[:, :D_ATTN]), c[:, D_ATTN:]], axis=1).astype(BF16)
    w_up_s[pl.ds(r0, r_in), :] = wup_c[...].astype(BF16)
    r_dn = wdown_c.shape[0]
    w_down_s[pl.ds(pl.multiple_of(s * r_dn, r_dn), r_dn), :] = wdown_c[...].astype(BF16)
    dst = jnp.where(s < N_Q_HEADS, (s % Q_PER_KV) * N_KV_HEADS + s // Q_PER_KV, s)
    w_out_s[pl.ds(pl.multiple_of(dst * HEAD_DIM, HEAD_DIM), HEAD_DIM), :] = wout_c[...].astype(BF16)


def _prompt_kernel(sinks_ref, x_ref, cos_ref, sa_ref, sb_ref,
                   gmix_ref, win_c, qg_ref, kg_ref, bd_ref, eye_ref, wpool_ref, pscale_ref, wout_c,
                   gffn_ref, wup_c, cw_ref, cb_ref, wdown_c,
                   y_ref, klast_ref, vlast_ref, ulast_ref, hlast_ref,
                   w_in_s, w_out_s, w_up_s, w_down_s, *scratch, n_tiles):
    s = pl.program_id(0)

    @pl.when(s < N_PREP)
    def _():
        _stage_weights(s, win_c, wout_c, wup_c, wdown_c, w_in_s, w_out_s, w_up_s, w_down_s)

    @pl.when(s >= N_PREP)
    def _():
        _prompt_body((s - N_PREP) % n_tiles, sinks_ref, x_ref, cos_ref, sa_ref, sb_ref,
                     gmix_ref, w_in_s, qg_ref, kg_ref, bd_ref, eye_ref, wpool_ref, pscale_ref, w_out_s,
                     gffn_ref, w_up_s, cw_ref, cb_ref, w_down_s,
                     y_ref, klast_ref, vlast_ref, ulast_ref, hlast_ref, *scratch)


def _prompt_body(i, sinks_ref, x_ref, cos_ref, sa_ref, sb_ref,
                 gmix_ref, win_ref, qg_ref, kg_ref, bd_ref, eye_ref, wpool_ref, pscale_ref, wout_ref,
                 gffn_ref, wup_ref, cw_ref, cb_ref, wdown_ref,
                 y_ref, klast_ref, vlast_ref, ulast_ref, hlast_ref,
                 kfull, vfull, vt, ufull, hprev, mix_buf, act_buf):
    T = x_ref.shape[0]
    d_ff = wdown_ref.shape[0]
    n_chunks = T // CHUNK

    @pl.when(i == 0)
    def _():
        kfull[0:WINDOW, :] = jnp.zeros((WINDOW, D_KV), BF16)
        kfull[WINDOW + T:, :] = jnp.zeros((ATT_KEYS - WINDOW - CHUNK, D_KV), BF16)
        vfull[0:WINDOW, :] = jnp.zeros((WINDOW, D_KV), BF16)
        vfull[WINDOW + T:, :] = jnp.zeros((ATT_KEYS - WINDOW - CHUNK, D_KV), BF16)
        for c in range(2):
            vt[c, D_KV:, :] = jnp.ones((VT_ROWS - D_KV, vt.shape[2]), BF16)
        ufull[0:POOL_HALO, :] = jnp.zeros((POOL_HALO, D_POOL), F32)
        hprev[...] = jnp.zeros(hprev.shape, F32)

    x = x_ref[...]
    q, k, v, u = _mixer_inputs(x, gmix_ref[...], win_ref[...], qg_ref[...], kg_ref[...],
                               bd_ref[...], cos_ref[...], sa_ref[...], sb_ref[...])
    klast_ref[...] = k[T - WINDOW:, :]
    vlast_ref[...] = v[T - WINDOW:, :]
    ulast_ref[...] = u[T - POOL_HALO:, :]
    kfull[WINDOW:WINDOW + T, :] = k.astype(BF16)
    vfull[WINDOW:WINDOW + T, :] = v.astype(BF16)
    ufull[POOL_HALO:POOL_HALO + T, :] = u

    eye = eye_ref[...]
    for c in range(2):
        vrows = vfull[c * CHUNK:c * CHUNK + T + WINDOW, :]
        vt[c, 0:D_KV, :] = _dot_nt(eye[:D_KV, :D_KV], vrows).astype(BF16)
    band = WINDOW + CHUNK
    n_q = N_Q_HEADS * CHUNK
    lane = lax.broadcasted_iota(jnp.int32, (1, D_ATTN), 1)
    head_of_col = lax.broadcasted_iota(jnp.int32, (1, n_q), 1) // CHUNK
    key_row = lax.broadcasted_iota(jnp.int32, (band, 1), 0)
    q_of_group = []
    for g in range(N_KV_HEADS):
        in_g = ((lane // HEAD_DIM) % N_KV_HEADS == g).astype(F32)
        q_of_group.append((q * in_g).astype(BF16))
    sink_row = jnp.full((1, n_q), sinks_ref[0], F32)
    for h8 in range(1, N_Q_HEADS):
        sink_row = jnp.where(head_of_col == h8, sinks_ref[h8], sink_row)
    half = Q_PER_KV * CHUNK
    scores = []
    for j in range(n_chunks):
        r0 = j * CHUNK
        qs = jnp.concatenate(
            [q_of_group[g][r0:r0 + CHUNK, hh * LANES:(hh + 1) * LANES]
             for g in range(N_KV_HEADS) for hh in range(Q_PER_KV)], axis=0)
        scores.append(_dot_nt(kfull[r0:r0 + band, :], qs))
    probs = []
    for j, s in enumerate(scores):
        r0 = j * CHUNK
        if r0 < WINDOW:
            first_valid = jnp.where(i == 0, WINDOW - r0, 0)
            s = jnp.where(key_row >= first_valid, s, NEG_INF)
        m = jnp.maximum(jnp.max(s, axis=0, keepdims=True), sink_row)
        probs.append((jnp.exp(s - m).astype(BF16), jnp.exp(sink_row - m)))
    outs_t = []
    for j, (e, sink_e) in enumerate(probs):
        vtb = vt[j % 2, :, (j // 2) * LANES:(j // 2) * LANES + band]
        oa = _dot(vtb, e)
        inv_den = 1.0 / (oa[D_KV:D_KV + 1] + sink_e)
        outs_t.append(jnp.concatenate(
            [oa[g * HEAD_DIM:(g + 1) * HEAD_DIM, g * half:(g + 1) * half]
             * inv_den[:, g * half:(g + 1) * half] for g in range(N_KV_HEADS)],
            axis=0).astype(BF16))
    for j, ot in enumerate(outs_t):
        r0 = j * CHUNK
        o = _dot_nt(eye, ot)
        for hh in range(Q_PER_KV):
            mix_buf[r0:r0 + CHUNK, hh * LANES:(hh + 1) * LANES] = (
                o[hh * CHUNK:(hh + 1) * CHUNK].astype(BF16))

    uf = ufull[...]
    pos = i * T + lax.broadcasted_iota(jnp.int32, (T, 1), 0)
    for gi, w in enumerate(POOL_WINDOWS):
        sl = slice(gi * POOL_GROUP, (gi + 1) * POOL_GROUP)
        tsum = _window_sum(uf[:, sl], w)[POOL_HALO:]
        cnt = jnp.minimum(pos + 1, w).astype(F32)
        d = (tsum / cnt - u[:, sl]).astype(BF16)
        pool = _dot(d, wpool_ref[gi].astype(BF16)) * pscale_ref[:, sl]
        mix_buf[:, D_ATTN + gi * POOL_GROUP:D_ATTN + (gi + 1) * POOL_GROUP] = pool.astype(BF16)

    kfull[0:WINDOW, :] = kfull[T:T + WINDOW, :]
    vfull[0:WINDOW, :] = vfull[T:T + WINDOW, :]
    ufull[0:POOL_HALO, :] = u[T - POOL_HALO:, :]

    x1 = x + _dot(mix_buf[...], wout_ref[...])

    xn2 = _rms_rows(x1, gffn_ref[...]).astype(BF16)
    F = FF_CHUNK
    y, k_done = x1, 0
    row_in_group = lax.broadcasted_iota(jnp.int32, (1, CONV_HALO, 1), 1)
    for c in range(d_ff // F):
        cols = (slice(c * F, (c + 1) * F), slice(d_ff + c * F, d_ff + (c + 1) * F))
        taps = []
        for half, cs in enumerate(cols):
            h = _dot(xn2, wup_ref[:, cs])
            groups = jnp.concatenate([hprev[:, cs], h], axis=0).reshape(T // CONV_HALO + 1, CONV_HALO, F)
            shifted = []
            for sh in (1, 2):
                r = pltpu.roll(groups, sh, 1)
                shifted.append(jnp.where(row_in_group < sh, r[:-1], r[1:]).reshape(T, F))
            hlast_ref[:, cs] = h[T - CONV_HALO:, :]
            hprev[:, cs] = h[T - CONV_HALO:, :]
            taps.append((h, shifted[0], shifted[1]))
        act_buf[:, cols[0]] = _conv_gate(taps[0], taps[1], cw_ref[:, cols[0]], cw_ref[:, cols[1]],
                                         cb_ref[:, cols[0]], cb_ref[:, cols[1]])
        c_done = c - DOWN_LAG
        if c_done >= 0 and (c_done + 1) % DOWN_GROUP == 0:
            ks = slice((c_done + 1 - DOWN_GROUP) * F, (c_done + 1) * F)
            y = y + _dot(act_buf[:, ks], wdown_ref[ks, :])
            k_done = (c_done + 1) * F
    y_ref[...] = y + _dot(act_buf[:, k_done:], wdown_ref[k_done:, :])


def _const_spec(shape):
    nd = len(shape)
    return pl.BlockSpec(shape, lambda *_: (0,) * nd)


def _chunk_spec(stacked, layer):
    rows = stacked.shape[1] // N_PREP
    return pl.BlockSpec((None, rows, stacked.shape[2]),
                        lambda s: (layer, jnp.minimum(s, N_PREP - 1), 0))


def _weight_specs(w, D, d_ff):
    layer = w["layer"]
    return [
        _const_spec((1, D)), _chunk_spec(w["w_in"], layer),
        _const_spec((1, D_ATTN)), _const_spec((1, D_KV)), _const_spec((MXU_COLS, MXU_COLS)),
        _const_spec((MXU_COLS, MXU_COLS)),
        _const_spec(w["w_pool"].shape), _const_spec((1, D_POOL)), _chunk_spec(w["w_out"], layer),
        _const_spec((1, D)), _chunk_spec(w["w_up"], layer),
        _const_spec((CONV_W, 2 * d_ff)), _const_spec((1, 2 * d_ff)), _chunk_spec(w["w_down"], layer),
    ]


def _weight_args(w):
    return (w["g_mix"], w["w_in"], w["q_gain"], w["k_gain"], w["bd"], w["eye"], w["w_pool"],
            w["pool_scale"], w["w_out"], w["g_ffn"], w["w_up"], w["conv_w"], w["conv_b"], w["w_down"])


def _weight_scratch(w):
    return [pltpu.VMEM(w[name].shape[1:], BF16) for name in ("w_in", "w_out", "w_up", "w_down")]


def _prompt_layer(x, tabs, w):
    B, L, D = x.shape
    T = PROMPT_TILE
    d_ff = w["w_down"].shape[1]
    n_tiles = L // T
    tile_of = lambda s: jnp.maximum(s - N_PREP, 0)
    tile_spec = pl.BlockSpec((None, T, D), lambda s: (tile_of(s) // n_tiles, tile_of(s) % n_tiles, 0))
    tab_spec = pl.BlockSpec((T, LANES), lambda s: (tile_of(s) % n_tiles, 0))

    def last_spec(rows, cols):
        return pl.BlockSpec((None, rows, cols), lambda s: (tile_of(s) // n_tiles, 0, 0))

    in_specs = [
        pl.BlockSpec(memory_space=pltpu.SMEM),
        tile_spec, tab_spec, tab_spec, tab_spec,
    ] + _weight_specs(w, D, d_ff)
    out_shape = (
        jax.ShapeDtypeStruct((B, L, D), F32),
        jax.ShapeDtypeStruct((B, WINDOW, D_KV), F32),
        jax.ShapeDtypeStruct((B, WINDOW, D_KV), F32),
        jax.ShapeDtypeStruct((B, POOL_HALO, D_POOL), F32),
        jax.ShapeDtypeStruct((B, CONV_HALO, 2 * d_ff), F32),
    )
    out_specs = (tile_spec, last_spec(WINDOW, D_KV), last_spec(WINDOW, D_KV),
                 last_spec(POOL_HALO, D_POOL), last_spec(CONV_HALO, 2 * d_ff))
    scratch = _weight_scratch(w) + [
        pltpu.VMEM((T + ATT_KEYS - CHUNK, D_KV), BF16),
        pltpu.VMEM((T + ATT_KEYS - CHUNK, D_KV), BF16),
        pltpu.VMEM((2, VT_ROWS, T + WINDOW), BF16),
        pltpu.VMEM((POOL_HALO + T, D_POOL), F32),
        pltpu.VMEM((CONV_HALO, 2 * d_ff), F32),
        pltpu.VMEM((T, D_ATTN + D_POOL), BF16),
        pltpu.VMEM((T, d_ff), BF16),
    ]
    return pl.pallas_call(
        functools.partial(_prompt_kernel, n_tiles=n_tiles),
        out_shape=out_shape,
        grid=(N_PREP + B * n_tiles,),
        in_specs=in_specs,
        out_specs=out_specs,
        scratch_shapes=scratch,
        name="prompt_layer",
        compiler_params=pltpu.CompilerParams(
            dimension_semantics=("arbitrary",),
            vmem_limit_bytes=VMEM_LIMIT_BYTES),
    )(w["sinks"], x, tabs[0], tabs[1], tabs[2], *_weight_args(w))


def _sample_kernel(sinks_ref, x_ref, cos_ref, sa_ref, sb_ref, ck_ref, cv_ref, pprev_ref,
                   c1_ref, c2_ref,
                   gmix_ref, win_c, qg_ref, kg_ref, bd_ref, eye_ref, wpool_ref, pscale_ref, wout_c,
                   gffn_ref, wup_c, cw_ref, cb_ref, wdown_c,
                   y_ref, kout_ref, vout_ref, uout_ref, hout_ref,
                   w_in_s, w_out_s, w_up_s, w_down_s, *scratch):
    s = pl.program_id(0)

    @pl.when(s < N_PREP)
    def _():
        _stage_weights(s, win_c, wout_c, wup_c, wdown_c, w_in_s, w_out_s, w_up_s, w_down_s)

    @pl.when(s >= N_PREP)
    def _():
        _sample_body(sinks_ref, x_ref, cos_ref, sa_ref, sb_ref, ck_ref, cv_ref, pprev_ref,
                     c1_ref, c2_ref,
                     gmix_ref, w_in_s, qg_ref, kg_ref, bd_ref, eye_ref, wpool_ref, pscale_ref, w_out_s,
                     gffn_ref, w_up_s, cw_ref, cb_ref, w_down_s,
                     y_ref, kout_ref, vout_ref, uout_ref, hout_ref, *scratch)


def _sample_body(sinks_ref, x_ref, cos_ref, sa_ref, sb_ref, ck_ref, cv_ref, pprev_ref,
                 c1_ref, c2_ref,
                 gmix_ref, win_ref, qg_ref, kg_ref, bd_ref, eye_ref, wpool_ref, pscale_ref, wout_ref,
                 gffn_ref, wup_ref, cw_ref, cb_ref, wdown_ref,
                 y_ref, kout_ref, vout_ref, uout_ref, hout_ref,
                 ufull, mix_buf):
    M = x_ref.shape[0]
    n_streams = ck_ref.shape[0]
    n_new = M // n_streams
    past = ck_ref.shape[1]
    d_ff = wdown_ref.shape[0]

    x = x_ref[...]
    q, k, v, u = _mixer_inputs(x, gmix_ref[...], win_ref[...], qg_ref[...], kg_ref[...],
                               bd_ref[...], cos_ref[...], sa_ref[...], sb_ref[...])
    q = q.astype(BF16)
    uout_ref[...] = u

    for s_ in range(n_streams):
        rows = slice(s_ * n_new, (s_ + 1) * n_new)
        keys = jnp.concatenate([ck_ref[s_], k[rows]], axis=0)
        vals = jnp.concatenate([cv_ref[s_], v[rows]], axis=0)
        kout_ref[s_] = keys[past + n_new - WINDOW:, :]
        vout_ref[s_] = vals[past + n_new - WINDOW:, :]
        keys16 = keys.astype(BF16)
        vals16 = vals.astype(BF16)
        outs = []
        for g in range(N_KV_HEADS):
            sink_col = jnp.concatenate(
                [jnp.full((n_new, 1), sinks_ref[g * Q_PER_KV + hh], F32) for hh in range(Q_PER_KV)],
                axis=0)
            qs = jnp.concatenate(
                [q[rows, hh * LANES + g * HEAD_DIM:hh * LANES + (g + 1) * HEAD_DIM]
                 for hh in range(Q_PER_KV)], axis=0)
            sc = _dot_nt(qs, keys16[:, g * HEAD_DIM:(g + 1) * HEAD_DIM])
            m = jnp.maximum(jnp.max(sc, axis=-1, keepdims=True), sink_col)
            e = jnp.exp(sc - m)
            den = jnp.sum(e, axis=-1, keepdims=True) + jnp.exp(sink_col - m)
            outs.append(_dot(e.astype(BF16), vals16[:, g * HEAD_DIM:(g + 1) * HEAD_DIM]) * (1.0 / den))
        for hh in range(Q_PER_KV):
            pair = jnp.concatenate([o[hh * n_new:(hh + 1) * n_new] for o in outs], axis=1)
            mix_buf[rows, hh * LANES:(hh + 1) * LANES] = pair.astype(BF16)
        ufull[s_ * 2 * n_new:s_ * 2 * n_new + n_new, :] = pprev_ref[s_]
        ufull[s_ * 2 * n_new + n_new:(s_ + 1) * 2 * n_new, :] = u[rows]

    uf = ufull[...]
    for gi, w in enumerate(POOL_WINDOWS):
        sl = slice(gi * POOL_GROUP, (gi + 1) * POOL_GROUP)
        ws = _window_sum(uf[:, sl], w)
        tsum = jnp.concatenate(
            [ws[s_ * 2 * n_new + n_new:(s_ + 1) * 2 * n_new] for s_ in range(n_streams)], axis=0)
        d = (tsum / float(w) - u[:, sl]).astype(BF16)
        pool = _dot(d, wpool_ref[gi].astype(BF16)) * pscale_ref[:, sl]
        mix_buf[:, D_ATTN + gi * POOL_GROUP:D_ATTN + (gi + 1) * POOL_GROUP] = pool.astype(BF16)

    x1 = x + _dot(mix_buf[...], wout_ref[...])
    xn2 = _rms_rows(x1, gffn_ref[...]).astype(BF16)

    t_in_stream = lax.broadcasted_iota(jnp.int32, (M, 1), 0) % n_new

    def taps(h, cs):
        return (h, jnp.where(t_in_stream >= 1, pltpu.roll(h, 1, 0), c1_ref[:, cs]),
                jnp.where(t_in_stream >= 2, pltpu.roll(h, 2, 0), c2_ref[:, cs]))

    acc = x1
    for c in range(d_ff // FF_CHUNK):
        cg = slice(c * FF_CHUNK, (c + 1) * FF_CHUNK)
        cv = slice(d_ff + c * FF_CHUNK, d_ff + (c + 1) * FF_CHUNK)
        hg = _dot(xn2, wup_ref[:, cg])
        hv = _dot(xn2, wup_ref[:, cv])
        hout_ref[:, cg] = hg
        hout_ref[:, cv] = hv
        act = _conv_gate(taps(hg, cg), taps(hv, cv), cw_ref[:, cg], cw_ref[:, cv],
                         cb_ref[:, cg], cb_ref[:, cv])
        acc = acc + _dot(act, wdown_ref[cg, :])
    y_ref[...] = acc


def _sample_layer(x, tabs, cache_k, cache_v, pool_prev, conv_prev, w):
    S, n_new, D = x.shape
    M = S * n_new
    d_ff = w["w_down"].shape[1]
    past = cache_k.shape[1]
    pprev = jnp.pad(pool_prev, ((0, 0), (n_new - pool_prev.shape[1], 0), (0, 0)))
    c1 = jnp.pad(conv_prev[:, 1:2], ((0, 0), (0, n_new - 1), (0, 0))).reshape(M, 2 * d_ff)
    c2 = jnp.pad(conv_prev, ((0, 0), (0, n_new - 2), (0, 0))).reshape(M, 2 * d_ff)
    acts = (x.reshape(M, D), tabs[0], tabs[1], tabs[2],
            cache_k.reshape(S, past, D_KV), cache_v.reshape(S, past, D_KV), pprev, c1, c2)
    out_shape = (
        jax.ShapeDtypeStruct((M, D), F32),
        jax.ShapeDtypeStruct((S, WINDOW, D_KV), F32),
        jax.ShapeDtypeStruct((S, WINDOW, D_KV), F32),
        jax.ShapeDtypeStruct((M, D_POOL), F32),
        jax.ShapeDtypeStruct((M, 2 * d_ff), F32),
    )
    y, ko, vo, uo, ho = pl.pallas_call(
        _sample_kernel,
        out_shape=out_shape,
        grid=(N_PREP + 1,),
        in_specs=([pl.BlockSpec(memory_space=pltpu.SMEM)] + [_const_spec(a.shape) for a in acts]
                  + _weight_specs(w, D, d_ff)),
        out_specs=tuple(_const_spec(o.shape) for o in out_shape),
        scratch_shapes=_weight_scratch(w) + [pltpu.VMEM((2 * M, D_POOL), F32),
                                             pltpu.VMEM((M, D_ATTN + D_POOL), BF16)],
        name="sample_layer",
        compiler_params=pltpu.CompilerParams(
            dimension_semantics=("arbitrary",), vmem_limit_bytes=VMEM_LIMIT_BYTES),
    )(w["sinks"], *acts, *_weight_args(w))
    return (y.reshape(S, n_new, D), ko, vo, uo.reshape(S, n_new, D_POOL),
            ho.reshape(S, n_new, 2 * d_ff))


def _rope_tables(pos, reps=1):
    half = ROT_DIM // 2
    inv = ROPE_THETA ** (-jnp.arange(0, ROT_DIM, 2, dtype=F32) / ROT_DIM)
    ang = pos.astype(F32)[:, None] * inv[None, :]
    cos, sin = jnp.cos(ang), jnp.sin(ang)
    n = pos.shape[0]
    ones = jnp.ones((n, HEAD_DIM - ROT_DIM), F32)
    zeros_h = jnp.zeros((n, half), F32)
    zeros_r = jnp.zeros((n, HEAD_DIM - ROT_DIM), F32)
    c = jnp.concatenate([cos, cos, ones], axis=1)
    sa = jnp.concatenate([-sin, zeros_h, zeros_r], axis=1)
    sb = jnp.concatenate([zeros_h, sin, zeros_r], axis=1)
    tile = lambda t: jnp.tile(t, (reps, LANES // HEAD_DIM))
    return tile(c), tile(sa), tile(sb)


def kernel(x_prompt, x_sample, cache_k, cache_v, state_pool, state_conv, norm_mix, w_in, q_norm,
           k_norm, attn_sinks, w_pool, pool_scale, w_out, norm_ffn, w_up, conv_w, conv_b, w_down):
    depth = w_in.shape[0]
    B, L, D = x_prompt.shape
    S, n_new, _ = x_sample.shape
    past_len = L
    assert L % PROMPT_TILE == 0 and PROMPT_TILE % CHUNK == 0 and PROMPT_TILE >= WINDOW
    assert w_down.shape[1] % FF_CHUNK == 0
    assert all(m.shape[1] % (16 * N_PREP) == 0 for m in (w_in, w_up, w_down))
    assert w_out.shape[1] == N_PREP * HEAD_DIM and LANES == N_KV_HEADS * HEAD_DIM

    tabs_p = _rope_tables(jnp.arange(L))
    tabs_s = _rope_tables(past_len + jnp.arange(n_new), reps=S)
    head_id = jnp.arange(MXU_COLS) // HEAD_DIM
    bd = jnp.where(head_id[:, None] == head_id[None, :], 1.0 / HEAD_DIM, 0.0).astype(BF16)
    q_scale = HEAD_DIM ** -0.5

    eye = jnp.eye(MXU_COLS, dtype=BF16)

    yp, ys = x_prompt, x_sample
    outs = [[] for _ in range(8)]
    for i in range(depth):
        w = dict(
            layer=i, sinks=attn_sinks[i],
            g_mix=norm_mix[i][None, :], w_in=w_in,
            q_gain=jnp.tile(q_norm[i] * q_scale, N_Q_HEADS)[None, :], k_gain=jnp.tile(k_norm[i], N_KV_HEADS)[None, :],
            bd=bd, eye=eye, w_pool=w_pool[i], pool_scale=pool_scale[i][None, :],
            w_out=w_out, g_ffn=norm_ffn[i][None, :], w_up=w_up,
            conv_w=conv_w[i], conv_b=conv_b[i][None, :], w_down=w_down,
        )
        yp, k1, v1, u1, h1 = _prompt_layer(yp, tabs_p, w)
        ys, k2, v2, u2, h2 = _sample_layer(ys, tabs_s, cache_k[i], cache_v[i], state_pool[i],
                                           state_conv[i], w)
        n_pool = state_pool.shape[2]
        n_conv = state_conv.shape[2]
        outs[0].append(k1.reshape(B, WINDOW, N_KV_HEADS, HEAD_DIM))
        outs[1].append(v1.reshape(B, WINDOW, N_KV_HEADS, HEAD_DIM))
        outs[2].append(u1[:, POOL_HALO - n_pool:])
        outs[3].append(h1[:, CONV_HALO - n_conv:])
        outs[4].append(k2.reshape(S, WINDOW, N_KV_HEADS, HEAD_DIM))
        outs[5].append(v2.reshape(S, WINDOW, N_KV_HEADS, HEAD_DIM))
        outs[6].append(u2[:, n_new - n_pool:])
        outs[7].append(h2[:, n_new - n_conv:])
    return (yp, ys) + tuple(jnp.stack(o) for o in outs)
```

```python
import functools

import jax
import jax.numpy as jnp
from jax import lax
from jax.experimental import pallas as pl
from jax.experimental.pallas import tpu as pltpu

F32 = jnp.float32
BF16 = jnp.bfloat16

CHUNK = 64
HEAD_DIM = 64
N_Q_HEADS = 8
N_KV_HEADS = 2
Q_PER_KV = N_Q_HEADS // N_KV_HEADS
D_ATTN = N_Q_HEADS * HEAD_DIM
D_KV = N_KV_HEADS * HEAD_DIM
WINDOW = 128
ROT_DIM = 16
ROPE_THETA = 500000.0
POOL_WINDOWS = (2, 4, 8, 16)
POOL_GROUP = 128
D_POOL = POOL_GROUP * len(POOL_WINDOWS)
POOL_HALO = 16
CONV_W = 3
CONV_HALO = 8
EPS = 1e-6
NEG_INF = -1e30
NEG_LOG2_E = -1.4426950408889634

LANES = 128
MXU_COLS = 256
ATT_KEYS = 256
VT_ROWS = D_KV + 16

PROMPT_TILE = 512
ATT_GROUP = 4
FF_CHUNK = 256
DOWN_GROUP = 3
DOWN_LAG = 2
N_PREP = 16
VMEM_LIMIT_BYTES = 56 * 1024 * 1024


def _dot(a, b):
    return jnp.dot(a, b, preferred_element_type=F32)


def _dot_nt(a, b):
    return lax.dot_general(a, b, (((1,), (1,)), ((), ())), preferred_element_type=F32)


def _dot_tn(a, b):
    return lax.dot_general(a, b, (((0,), (0,)), ((), ())), preferred_element_type=F32)


def _rms_rows(x, gain):
    ms = jnp.mean(x * x, axis=-1, keepdims=True)
    return x * lax.rsqrt(ms + EPS) * gain


def _head_rms(t, bd, gain):
    ms = _dot((t * t).astype(BF16), bd)
    return t * lax.rsqrt(ms + EPS) * gain


def _rope(t, cos, sin_a, sin_b):
    return (t * cos + pltpu.roll(t, LANES - ROT_DIM // 2, 1) * sin_a
            + pltpu.roll(t, ROT_DIM // 2, 1) * sin_b)


def _mixer_inputs(x, gmix, w_in, qg, kg, bd, cos, sin_a, sin_b):
    xn = _rms_rows(x, gmix).astype(BF16)
    h = _dot(xn, w_in)
    q_parts = []
    for j in range(D_ATTN // MXU_COLS):
        qb = _head_rms(h[:, j * MXU_COLS:(j + 1) * MXU_COLS], bd,
                       qg[:, j * MXU_COLS:(j + 1) * MXU_COLS])
        for l in range(MXU_COLS // LANES):
            q_parts.append(_rope(qb[:, l * LANES:(l + 1) * LANES], cos, sin_a, sin_b))
    q = jnp.concatenate(q_parts, axis=1)
    k = _head_rms(h[:, D_ATTN:D_ATTN + D_KV], bd[:D_KV, :D_KV], kg)
    k = _rope(k, cos, sin_a, sin_b)
    v = h[:, D_ATTN + D_KV:D_ATTN + 2 * D_KV]
    u = h[:, D_ATTN + 2 * D_KV:]
    return q, k, v, u


def _window_sum(a, w):
    s = 1
    while s < w:
        a = a + pltpu.roll(a, s, 0)
        s *= 2
    return a


def _conv_gate(hg, hv, cw_g, cw_v, cb_g, cb_v):
    cg = cb_g + hg[2] * cw_g[0:1] + hg[1] * cw_g[1:2] + hg[0] * cw_g[2:3]
    cv = cb_v + hv[2] * cw_v[0:1] + hv[1] * cw_v[1:2] + hv[0] * cw_v[2:3]
    return ((cg / (1.0 + jnp.exp2(cg * NEG_LOG2_E))) * cv).astype(BF16)


def _pair_heads(qcols):
    lane = lax.broadcasted_iota(jnp.int32, (1, LANES), 1)
    blocks = [qcols[:, b * LANES:(b + 1) * LANES] for b in range(D_ATTN // LANES)]
    per_block = LANES // HEAD_DIM
    out = []
    for hh in range(Q_PER_KV):
        a = blocks[hh // per_block]
        b = blocks[Q_PER_KV // per_block + hh // per_block]
        if hh % per_block == 0:
            out.append(jnp.where(lane < HEAD_DIM, a, pltpu.roll(b, HEAD_DIM, 1)))
        else:
            out.append(jnp.where(lane < HEAD_DIM, pltpu.roll(a, HEAD_DIM, 1), b))
    return jnp.concatenate(out, axis=1)


def _stage_weights(s, win_c, wout_c, wup_c, wdown_c, w_in_s, w_out_s, w_up_s, w_down_s):
    r_in = win_c.shape[0]
    r0 = pl.multiple_of(s * r_in, r_in)
    c = win_c[...]
    w_in_s[pl.ds(r0, r_in), :] = jnp.concatenate(
        [_pair_heads(c[:, :D_ATTN]), c[:, D_ATTN:]], axis=1).astype(BF16)
    w_up_s[pl.ds(r0, r_in), :] = wup_c[...].astype(BF16)
    r_dn = wdown_c.shape[0]
    w_down_s[pl.ds(pl.multiple_of(s * r_dn, r_dn), r_dn), :] = wdown_c[...].astype(BF16)
    dst = jnp.where(s < N_Q_HEADS, (s % Q_PER_KV) * N_KV_HEADS + s // Q_PER_KV, s)
    w_out_s[pl.ds(pl.multiple_of(dst * HEAD_DIM, HEAD_DIM), HEAD_DIM), :] = wout_c[...].astype(BF16)


def _prompt_kernel(sinks_ref, x_ref, cos_ref, sa_ref, sb_ref,
                   gmix_ref, win_c, qg_ref, kg_ref, bd_ref, eye_ref, wpool_ref, pscale_ref, wout_c,
                   gffn_ref, wup_c, cw_ref, cb_ref, wdown_c,
                   xs_ref, cos_s_ref, sa_s_ref, sb_s_ref, ck_ref, cv_ref, pprev_ref, cprev_ref,
                   y_ref, klast_ref, vlast_ref, ulast_ref, hlast_ref,
                   ys_ref, kout_ref, vout_ref, uout_ref, hout_ref,
                   w_in_s, w_out_s, w_up_s, w_down_s,
                   kfull, vfull, vt, ufull, hprev, mix_buf, act_buf, ufull_s, mix_s,
                   *, n_tiles, n_total):
    s = pl.program_id(0)

    @pl.when(s < N_PREP)
    def _():
        _stage_weights(s, win_c, wout_c, wup_c, wdown_c, w_in_s, w_out_s, w_up_s, w_down_s)

    @pl.when((s >= N_PREP) & (s < N_PREP + n_total))
    def _():
        _prompt_body((s - N_PREP) % n_tiles, sinks_ref, x_ref, cos_ref, sa_ref, sb_ref,
                     gmix_ref, w_in_s, qg_ref, kg_ref, bd_ref, eye_ref, wpool_ref, pscale_ref, w_out_s,
                     gffn_ref, w_up_s, cw_ref, cb_ref, w_down_s,
                     y_ref, klast_ref, vlast_ref, ulast_ref, hlast_ref,
                     kfull, vfull, vt, ufull, hprev, mix_buf, act_buf)

    @pl.when(s == N_PREP + n_total)
    def _():
        _sample_body(sinks_ref, xs_ref, cos_s_ref, sa_s_ref, sb_s_ref, ck_ref, cv_ref, pprev_ref,
                     cprev_ref,
                     gmix_ref, w_in_s, qg_ref, kg_ref, bd_ref, eye_ref, wpool_ref, pscale_ref, w_out_s,
                     gffn_ref, w_up_s, cw_ref, cb_ref, w_down_s,
                     ys_ref, kout_ref, vout_ref, uout_ref, hout_ref, ufull_s, mix_s)


def _prompt_body(i, sinks_ref, x_ref, cos_ref, sa_ref, sb_ref,
                 gmix_ref, win_ref, qg_ref, kg_ref, bd_ref, eye_ref, wpool_ref, pscale_ref, wout_ref,
                 gffn_ref, wup_ref, cw_ref, cb_ref, wdown_ref,
                 y_ref, klast_ref, vlast_ref, ulast_ref, hlast_ref,
                 kfull, vfull, vt, ufull, hprev, mix_buf, act_buf):
    T = x_ref.shape[0]
    d_ff = wdown_ref.shape[0]
    n_chunks = T // CHUNK

    @pl.when(i == 0)
    def _():
        kfull[0:WINDOW, :] = jnp.zeros((WINDOW, D_KV), BF16)
        kfull[WINDOW + T:, :] = jnp.zeros((ATT_KEYS - WINDOW - CHUNK, D_KV), BF16)
        vfull[0:WINDOW, :] = jnp.zeros((WINDOW, D_KV), BF16)
        vfull[WINDOW + T:, :] = jnp.zeros((ATT_KEYS - WINDOW - CHUNK, D_KV), BF16)
        for c in range(2):
            vt[c, D_KV:, :] = jnp.ones((VT_ROWS - D_KV, vt.shape[2]), BF16)
        ufull[0:POOL_HALO, :] = jnp.zeros((POOL_HALO, D_POOL), F32)
        hprev[...] = jnp.zeros(hprev.shape, F32)

    x = x_ref[...]
    q, k, v, u = _mixer_inputs(x, gmix_ref[...], win_ref[...], qg_ref[...], kg_ref[...],
                               bd_ref[...], cos_ref[...], sa_ref[...], sb_ref[...])
    klast_ref[...] = k[T - WINDOW:, :]
    vlast_ref[...] = v[T - WINDOW:, :]
    ulast_ref[...] = u[T - POOL_HALO:, :]
    kfull[WINDOW:WINDOW + T, :] = k.astype(BF16)
    vfull[WINDOW:WINDOW + T, :] = v.astype(BF16)
    ufull[POOL_HALO:POOL_HALO + T, :] = u

    eye = eye_ref[...]
    for c in range(2):
        vrows = vfull[c * CHUNK:c * CHUNK + T + WINDOW, :]
        vt[c, 0:D_KV, :] = _dot_nt(eye[:D_KV, :D_KV], vrows).astype(BF16)
    band = WINDOW + CHUNK
    n_q = N_Q_HEADS * CHUNK
    lane = lax.broadcasted_iota(jnp.int32, (1, D_ATTN), 1)
    head_of_col = lax.broadcasted_iota(jnp.int32, (1, n_q), 1) // CHUNK
    key_row = lax.broadcasted_iota(jnp.int32, (band, 1), 0)
    q_of_group = []
    for g in range(N_KV_HEADS):
        in_g = ((lane // HEAD_DIM) % N_KV_HEADS == g).astype(F32)
        q_of_group.append((q * in_g).astype(BF16))
    sink_row = jnp.full((1, n_q), sinks_ref[0], F32)
    for h8 in range(1, N_Q_HEADS):
        sink_row = jnp.where(head_of_col == h8, sinks_ref[h8], sink_row)
    half = Q_PER_KV * CHUNK
    for j0 in range(0, n_chunks, ATT_GROUP):
        chunk_ids = range(j0, min(j0 + ATT_GROUP, n_chunks))
        scores = []
        for j in chunk_ids:
            r0 = j * CHUNK
            qs = jnp.concatenate(
                [q_of_group[g][r0:r0 + CHUNK, hh * LANES:(hh + 1) * LANES]
                 for g in range(N_KV_HEADS) for hh in range(Q_PER_KV)], axis=0)
            scores.append(_dot_nt(kfull[r0:r0 + band, :], qs))
        probs = []
        for j, s in zip(chunk_ids, scores):
            r0 = j * CHUNK
            if r0 < WINDOW:
                first_valid = jnp.where(i == 0, WINDOW - r0, 0)
                s = jnp.where(key_row >= first_valid, s, NEG_INF)
            m = jnp.maximum(jnp.max(s, axis=0, keepdims=True), sink_row)
            probs.append((jnp.exp(s - m).astype(BF16), jnp.exp(sink_row - m)))
        outs_t = []
        for j, (e, sink_e) in zip(chunk_ids, probs):
            vtb = vt[j % 2, :, (j // 2) * LANES:(j // 2) * LANES + band]
            oa = _dot(vtb, e)
            inv_den = 1.0 / (oa[D_KV:D_KV + 1] + sink_e)
            outs_t.append(jnp.concatenate(
                [oa[g * HEAD_DIM:(g + 1) * HEAD_DIM, g * half:(g + 1) * half]
                 * inv_den[:, g * half:(g + 1) * half] for g in range(N_KV_HEADS)],
                axis=0).astype(BF16))
        for j, ot in zip(chunk_ids, outs_t):
            r0 = j * CHUNK
            o = _dot_nt(eye, ot)
            for hh in range(Q_PER_KV):
                mix_buf[r0:r0 + CHUNK, hh * LANES:(hh + 1) * LANES] = (
                    o[hh * CHUNK:(hh + 1) * CHUNK].astype(BF16))

    uf = ufull[...]
    pos = i * T + lax.broadcasted_iota(jnp.int32, (T, 1), 0)
    for gi, w in enumerate(POOL_WINDOWS):
        sl = slice(gi * POOL_GROUP, (gi + 1) * POOL_GROUP)
        tsum = _window_sum(uf[:, sl], w)[POOL_HALO:]
        cnt = jnp.minimum(pos + 1, w).astype(F32)
        d = (tsum / cnt - u[:, sl]).astype(BF16)
        pool = _dot(d, wpool_ref[gi].astype(BF16)) * pscale_ref[:, sl]
        mix_buf[:, D_ATTN + gi * POOL_GROUP:D_ATTN + (gi + 1) * POOL_GROUP] = pool.astype(BF16)

    kfull[0:WINDOW, :] = kfull[T:T + WINDOW, :]
    vfull[0:WINDOW, :] = vfull[T:T + WINDOW, :]
    ufull[0:POOL_HALO, :] = u[T - POOL_HALO:, :]

    x1 = x + _dot(mix_buf[...], wout_ref[...])

    xn2 = _rms_rows(x1, gffn_ref[...]).astype(BF16)
    F = FF_CHUNK
    y, k_done, acts = x1, 0, []
    row_in_group = lax.broadcasted_iota(jnp.int32, (1, CONV_HALO, 1), 1)
    for c in range(d_ff // F):
        cols = (slice(c * F, (c + 1) * F), slice(d_ff + c * F, d_ff + (c + 1) * F))
        taps = []
        for half, cs in enumerate(cols):
            h = _dot(xn2, wup_ref[:, cs])
            groups = jnp.concatenate([hprev[:, cs], h], axis=0).reshape(T // CONV_HALO + 1, CONV_HALO, F)
            shifted = []
            for sh in (1, 2):
                r = pltpu.roll(groups, sh, 1)
                shifted.append(jnp.where(row_in_group < sh, r[:-1], r[1:]).reshape(T, F))
            hlast_ref[:, cs] = h[T - CONV_HALO:, :]
            hprev[:, cs] = h[T - CONV_HALO:, :]
            taps.append((h, shifted[0], shifted[1]))
        acts.append(_conv_gate(taps[0], taps[1], cw_ref[:, cols[0]], cw_ref[:, cols[1]],
                               cb_ref[:, cols[0]], cb_ref[:, cols[1]]))
        c_done = c - DOWN_LAG
        if c_done >= 0 and (c_done + 1) % DOWN_GROUP == 0:
            ks = slice((c_done + 1 - DOWN_GROUP) * F, (c_done + 1) * F)
            group = jnp.concatenate(acts[c_done + 1 - DOWN_GROUP:c_done + 1], axis=1)
            y = y + _dot(group, wdown_ref[ks, :])
            k_done = (c_done + 1) * F
    y_ref[...] = y + _dot(jnp.concatenate(acts[k_done // F:], axis=1), wdown_ref[k_done:, :])


def _const_spec(shape):
    nd = len(shape)
    return pl.BlockSpec(shape, lambda *_: (0,) * nd)


def _chunk_spec(stacked, layer):
    rows = stacked.shape[1] // N_PREP
    return pl.BlockSpec((None, rows, stacked.shape[2]),
                        lambda s: (layer, jnp.minimum(s, N_PREP - 1), 0))


def _weight_specs(w, D, d_ff):
    layer = w["layer"]
    return [
        _const_spec((1, D)), _chunk_spec(w["w_in"], layer),
        _const_spec((1, D_ATTN)), _const_spec((1, D_KV)), _const_spec((MXU_COLS, MXU_COLS)),
        _const_spec((MXU_COLS, MXU_COLS)),
        _const_spec(w["w_pool"].shape), _const_spec((1, D_POOL)), _chunk_spec(w["w_out"], layer),
        _const_spec((1, D)), _chunk_spec(w["w_up"], layer),
        _const_spec((CONV_W, 2 * d_ff)), _const_spec((1, 2 * d_ff)), _chunk_spec(w["w_down"], layer),
    ]


def _weight_args(w):
    return (w["g_mix"], w["w_in"], w["q_gain"], w["k_gain"], w["bd"], w["eye"], w["w_pool"],
            w["pool_scale"], w["w_out"], w["g_ffn"], w["w_up"], w["conv_w"], w["conv_b"], w["w_down"])


def _weight_scratch(w):
    return [pltpu.VMEM(w[name].shape[1:], BF16) for name in ("w_in", "w_out", "w_up", "w_down")]


def _resident_spec(shape):
    nd = len(shape)
    return pl.BlockSpec(shape, lambda *_: (0,) * nd, pipeline_mode=pl.Buffered(1))


def _layer_call(x, tabs, xs, tabs_s, cache_k, cache_v, pool_prev, conv_prev, w):
    B, L, D = x.shape
    S, n_new, _ = xs.shape
    M = S * n_new
    T = PROMPT_TILE
    d_ff = w["w_down"].shape[1]
    n_tiles = L // T
    n_total = B * n_tiles
    past = cache_k.shape[1]
    tile_of = lambda s: jnp.clip(s - N_PREP, 0, n_total - 1)
    tile_spec = pl.BlockSpec((None, T, D), lambda s: (tile_of(s) // n_tiles, tile_of(s) % n_tiles, 0))
    tab_spec = pl.BlockSpec((T, LANES), lambda s: (tile_of(s) % n_tiles, 0))

    def last_spec(rows, cols):
        return pl.BlockSpec((None, rows, cols), lambda s: (tile_of(s) // n_tiles, 0, 0))

    pprev = jnp.pad(pool_prev, ((0, 0), (n_new - pool_prev.shape[1], 0), (0, 0)))
    sample_in = (xs.reshape(M, D), tabs_s[0], tabs_s[1], tabs_s[2],
                 cache_k.reshape(S, past, D_KV), cache_v.reshape(S, past, D_KV), pprev, conv_prev)
    in_specs = ([pl.BlockSpec(memory_space=pltpu.SMEM), tile_spec, tab_spec, tab_spec, tab_spec]
                + _weight_specs(w, D, d_ff) + [_resident_spec(a.shape) for a in sample_in])
    sample_out = (
        jax.ShapeDtypeStruct((M, D), F32),
        jax.ShapeDtypeStruct((S, WINDOW, D_KV), F32),
        jax.ShapeDtypeStruct((S, WINDOW, D_KV), F32),
        jax.ShapeDtypeStruct((M, D_POOL), F32),
        jax.ShapeDtypeStruct((S, CONV_HALO, 2 * d_ff), F32),
    )
    out_shape = (
        jax.ShapeDtypeStruct((B, L, D), F32),
        jax.ShapeDtypeStruct((B, WINDOW, D_KV), F32),
        jax.ShapeDtypeStruct((B, WINDOW, D_KV), F32),
        jax.ShapeDtypeStruct((B, POOL_HALO, D_POOL), F32),
        jax.ShapeDtypeStruct((B, CONV_HALO, 2 * d_ff), F32),
    ) + sample_out
    out_specs = (tile_spec, last_spec(WINDOW, D_KV), last_spec(WINDOW, D_KV),
                 last_spec(POOL_HALO, D_POOL), last_spec(CONV_HALO, 2 * d_ff)
                 ) + tuple(_resident_spec(o.shape) for o in sample_out)
    scratch = _weight_scratch(w) + [
        pltpu.VMEM((T + ATT_KEYS - CHUNK, D_KV), BF16),
        pltpu.VMEM((T + ATT_KEYS - CHUNK, D_KV), BF16),
        pltpu.VMEM((2, VT_ROWS, T + WINDOW), BF16),
        pltpu.VMEM((POOL_HALO + T, D_POOL), F32),
        pltpu.VMEM((CONV_HALO, 2 * d_ff), F32),
        pltpu.VMEM((T, D_ATTN + D_POOL), BF16),
        pltpu.VMEM((T, d_ff), BF16),
        pltpu.VMEM((2 * M, D_POOL), F32),
        pltpu.VMEM((M, D_ATTN + D_POOL), BF16),
    ]
    outs = pl.pallas_call(
        functools.partial(_prompt_kernel, n_tiles=n_tiles, n_total=n_total),
        out_shape=out_shape,
        grid=(N_PREP + n_total + 1,),
        in_specs=in_specs,
        out_specs=out_specs,
        scratch_shapes=scratch,
        name="layer",
        compiler_params=pltpu.CompilerParams(
            dimension_semantics=("arbitrary",),
            vmem_limit_bytes=VMEM_LIMIT_BYTES),
    )(w["sinks"], x, tabs[0], tabs[1], tabs[2], *_weight_args(w), *sample_in)
    ys, ko, vo, uo, ho = outs[5:]
    return outs[:5], (ys.reshape(S, n_new, D), ko, vo, uo.reshape(S, n_new, D_POOL), ho)


def _sample_body(sinks_ref, x_ref, cos_ref, sa_ref, sb_ref, ck_ref, cv_ref, pprev_ref,
                 cprev_ref,
                 gmix_ref, win_ref, qg_ref, kg_ref, bd_ref, eye_ref, wpool_ref, pscale_ref, wout_ref,
                 gffn_ref, wup_ref, cw_ref, cb_ref, wdown_ref,
                 y_ref, kout_ref, vout_ref, uout_ref, hout_ref,
                 ufull, mix_buf):
    M = x_ref.shape[0]
    n_streams = ck_ref.shape[0]
    n_new = M // n_streams
    past = ck_ref.shape[1]
    d_ff = wdown_ref.shape[0]

    x = x_ref[...]
    q, k, v, u = _mixer_inputs(x, gmix_ref[...], win_ref[...], qg_ref[...], kg_ref[...],
                               bd_ref[...], cos_ref[...], sa_ref[...], sb_ref[...])
    q = q.astype(BF16)
    uout_ref[...] = u

    for s_ in range(n_streams):
        rows = slice(s_ * n_new, (s_ + 1) * n_new)
        keys = jnp.concatenate([ck_ref[s_], k[rows]], axis=0)
        vals = jnp.concatenate([cv_ref[s_], v[rows]], axis=0)
        kout_ref[s_] = keys[past + n_new - WINDOW:, :]
        vout_ref[s_] = vals[past + n_new - WINDOW:, :]
        keys16 = keys.astype(BF16)
        vals16 = vals.astype(BF16)
        outs = []
        for g in range(N_KV_HEADS):
            sink_col = jnp.concatenate(
                [jnp.full((n_new, 1), sinks_ref[g * Q_PER_KV + hh], F32) for hh in range(Q_PER_KV)],
                axis=0)
            qs = jnp.concatenate(
                [q[rows, hh * LANES + g * HEAD_DIM:hh * LANES + (g + 1) * HEAD_DIM]
                 for hh in range(Q_PER_KV)], axis=0)
            sc = _dot_nt(qs, keys16[:, g * HEAD_DIM:(g + 1) * HEAD_DIM])
            m = jnp.maximum(jnp.max(sc, axis=-1, keepdims=True), sink_col)
            e = jnp.exp(sc - m)
            den = jnp.sum(e, axis=-1, keepdims=True) + jnp.exp(sink_col - m)
            outs.append(_dot(e.astype(BF16), vals16[:, g * HEAD_DIM:(g + 1) * HEAD_DIM]) * (1.0 / den))
        for hh in range(Q_PER_KV):
            pair = jnp.concatenate([o[hh * n_new:(hh + 1) * n_new] for o in outs], axis=1)
            mix_buf[rows, hh * LANES:(hh + 1) * LANES] = pair.astype(BF16)
        ufull[s_ * 2 * n_new:s_ * 2 * n_new + n_new, :] = pprev_ref[s_]
        ufull[s_ * 2 * n_new + n_new:(s_ + 1) * 2 * n_new, :] = u[rows]

    uf = ufull[...]
    for gi, w in enumerate(POOL_WINDOWS):
        sl = slice(gi * POOL_GROUP, (gi + 1) * POOL_GROUP)
        ws = _window_sum(uf[:, sl], w)
        tsum = jnp.concatenate(
            [ws[s_ * 2 * n_new + n_new:(s_ + 1) * 2 * n_new] for s_ in range(n_streams)], axis=0)
        d = (tsum / float(w) - u[:, sl]).astype(BF16)
        pool = _dot(d, wpool_ref[gi].astype(BF16)) * pscale_ref[:, sl]
        mix_buf[:, D_ATTN + gi * POOL_GROUP:D_ATTN + (gi + 1) * POOL_GROUP] = pool.astype(BF16)

    x1 = x + _dot(mix_buf[...], wout_ref[...])
    xn2 = _rms_rows(x1, gffn_ref[...]).astype(BF16)

    t_in_stream = lax.broadcasted_iota(jnp.int32, (M, 1), 0) % n_new

    n_conv = cprev_ref.shape[1]
    keep = hout_ref.shape[1]

    def taps(h, cs):
        prev = [jnp.broadcast_to(cprev_ref[:, j:j + 1, cs], (n_streams, n_new, h.shape[1]))
                .reshape(M, h.shape[1]) for j in range(n_conv)]
        return (h, jnp.where(t_in_stream >= 1, pltpu.roll(h, 1, 0), prev[1]),
                jnp.where(t_in_stream >= 2, pltpu.roll(h, 2, 0),
                          jnp.where(t_in_stream == 0, prev[0], prev[1])))

    def newest(h):
        return h.reshape(n_streams, n_new, h.shape[1])[:, n_new - keep:, :]

    acc = x1
    for c in range(d_ff // FF_CHUNK):
        cg = slice(c * FF_CHUNK, (c + 1) * FF_CHUNK)
        cv = slice(d_ff + c * FF_CHUNK, d_ff + (c + 1) * FF_CHUNK)
        hg = _dot(xn2, wup_ref[:, cg])
        hv = _dot(xn2, wup_ref[:, cv])
        hout_ref[:, :, cg] = newest(hg)
        hout_ref[:, :, cv] = newest(hv)
        act = _conv_gate(taps(hg, cg), taps(hv, cv), cw_ref[:, cg], cw_ref[:, cv],
                         cb_ref[:, cg], cb_ref[:, cv])
        acc = acc + _dot(act, wdown_ref[cg, :])
    y_ref[...] = acc


def _rope_tables(pos, reps=1):
    half = ROT_DIM // 2
    inv = ROPE_THETA ** (-jnp.arange(0, ROT_DIM, 2, dtype=F32) / ROT_DIM)
    ang = pos.astype(F32)[:, None] * inv[None, :]
    cos, sin = jnp.cos(ang), jnp.sin(ang)
    n = pos.shape[0]
    ones = jnp.ones((n, HEAD_DIM - ROT_DIM), F32)
    zeros_h = jnp.zeros((n, half), F32)
    zeros_r = jnp.zeros((n, HEAD_DIM - ROT_DIM), F32)
    c = jnp.concatenate([cos, cos, ones], axis=1)
    sa = jnp.concatenate([-sin, zeros_h, zeros_r], axis=1)
    sb = jnp.concatenate([zeros_h, sin, zeros_r], axis=1)
    tile = lambda t: jnp.tile(t, (reps, LANES // HEAD_DIM))
    return tile(c), tile(sa), tile(sb)


def kernel(x_prompt, x_sample, cache_k, cache_v, state_pool, state_conv, norm_mix, w_in, q_norm,
           k_norm, attn_sinks, w_pool, pool_scale, w_out, norm_ffn, w_up, conv_w, conv_b, w_down):
    depth = w_in.shape[0]
    B, L, D = x_prompt.shape
    S, n_new, _ = x_sample.shape
    past_len = L
    assert L % PROMPT_TILE == 0 and PROMPT_TILE % CHUNK == 0 and PROMPT_TILE >= WINDOW
    assert w_down.shape[1] % FF_CHUNK == 0
    assert all(m.shape[1] % (16 * N_PREP) == 0 for m in (w_in, w_up, w_down))
    assert w_out.shape[1] == N_PREP * HEAD_DIM and LANES == N_KV_HEADS * HEAD_DIM

    tabs_p = _rope_tables(jnp.arange(L))
    tabs_s = _rope_tables(past_len + jnp.arange(n_new), reps=S)
    head_id = jnp.arange(MXU_COLS) // HEAD_DIM
    bd = jnp.where(head_id[:, None] == head_id[None, :], 1.0 / HEAD_DIM, 0.0).astype(BF16)
    q_scale = HEAD_DIM ** -0.5

    eye = jnp.eye(MXU_COLS, dtype=BF16)

    yp, ys = x_prompt, x_sample
    outs = [[] for _ in range(8)]
    for i in range(depth):
        w = dict(
            layer=i, sinks=attn_sinks[i],
            g_mix=norm_mix[i][None, :], w_in=w_in,
            q_gain=jnp.tile(q_norm[i] * q_scale, N_Q_HEADS)[None, :], k_gain=jnp.tile(k_norm[i], N_KV_HEADS)[None, :],
            bd=bd, eye=eye, w_pool=w_pool[i], pool_scale=pool_scale[i][None, :],
            w_out=w_out, g_ffn=norm_ffn[i][None, :], w_up=w_up,
            conv_w=conv_w[i], conv_b=conv_b[i][None, :], w_down=w_down,
        )
        (yp, k1, v1, u1, h1), (ys, k2, v2, u2, h2) = _layer_call(
            yp, tabs_p, ys, tabs_s, cache_k[i], cache_v[i], state_pool[i], state_conv[i], w)
        n_pool = state_pool.shape[2]
        n_conv = state_conv.shape[2]
        outs[0].append(k1.reshape(B, WINDOW, N_KV_HEADS, HEAD_DIM))
        outs[1].append(v1.reshape(B, WINDOW, N_KV_HEADS, HEAD_DIM))
        outs[2].append(u1[:, POOL_HALO - n_pool:])
        outs[3].append(h1[:, CONV_HALO - n_conv:])
        outs[4].append(k2.reshape(S, WINDOW, N_KV_HEADS, HEAD_DIM))
        outs[5].append(v2.reshape(S, WINDOW, N_KV_HEADS, HEAD_DIM))
        outs[6].append(u2[:, n_new - n_pool:])
        outs[7].append(h2[:, CONV_HALO - n_conv:])
    return (yp, ys) + tuple(jnp.stack(o) for o in outs)
```

```python
import functools
from typing import Any, NamedTuple

import jax
import jax.numpy as jnp
from jax import lax
from jax.experimental import pallas as pl
from jax.experimental.pallas import tpu as pltpu

F32 = jnp.float32
BF16 = jnp.bfloat16

CHUNK = 64
HEAD_DIM = 64
N_Q_HEADS = 8
N_KV_HEADS = 2
Q_PER_KV = N_Q_HEADS // N_KV_HEADS
D_ATTN = N_Q_HEADS * HEAD_DIM
D_KV = N_KV_HEADS * HEAD_DIM
WINDOW = 128
ROT_DIM = 16
ROPE_THETA = 500000.0
POOL_WINDOWS = (2, 4, 8, 16)
POOL_GROUP = 128
D_POOL = POOL_GROUP * len(POOL_WINDOWS)
POOL_HALO = 16
CONV_W = 3
CONV_HALO = 8
EPS = 1e-6
NEG_INF = -1e30
NEG_LOG2_E = -1.4426950408889634

LANES = 128
MXU_COLS = 256
ATT_KEYS = 256
VT_ROWS = D_KV + 16

PROMPT_TILE = 512
ATT_GROUP = 4
FF_CHUNK = 256
DOWN_GROUP = 3
DOWN_LAG = 2
N_PREP = 16
VMEM_LIMIT_BYTES = 56 * 1024 * 1024


def _dot(a, b):
    return jnp.dot(a, b, preferred_element_type=F32)


def _dot_nt(a, b):
    return lax.dot_general(a, b, (((1,), (1,)), ((), ())), preferred_element_type=F32)


def _dot_tn(a, b):
    return lax.dot_general(a, b, (((0,), (0,)), ((), ())), preferred_element_type=F32)


def _rms_rows(x, gain):
    ms = jnp.mean(x * x, axis=-1, keepdims=True)
    return x * lax.rsqrt(ms + EPS) * gain


def _head_rms(t, bd, gain):
    ms = _dot((t * t).astype(BF16), bd)
    return t * lax.rsqrt(ms + EPS) * gain


def _rope(t, cos, sin_a, sin_b):
    return (t * cos + pltpu.roll(t, LANES - ROT_DIM // 2, 1) * sin_a
            + pltpu.roll(t, ROT_DIM // 2, 1) * sin_b)


def _mixer_inputs(x, gmix, w_in, qg, kg, bd, cos, sin_a, sin_b):
    xn = _rms_rows(x, gmix).astype(BF16)
    h = _dot(xn, w_in)
    q_parts = []
    for j in range(D_ATTN // MXU_COLS):
        qb = _head_rms(h[:, j * MXU_COLS:(j + 1) * MXU_COLS], bd,
                       qg[:, j * MXU_COLS:(j + 1) * MXU_COLS])
        for l in range(MXU_COLS // LANES):
            q_parts.append(_rope(qb[:, l * LANES:(l + 1) * LANES], cos, sin_a, sin_b))
    q = jnp.concatenate(q_parts, axis=1)
    k = _head_rms(h[:, D_ATTN:D_ATTN + D_KV], bd[:D_KV, :D_KV], kg)
    k = _rope(k, cos, sin_a, sin_b)
    v = h[:, D_ATTN + D_KV:D_ATTN + 2 * D_KV]
    u = h[:, D_ATTN + 2 * D_KV:]
    return q, k, v, u


def _window_sum(a, w):
    s = 1
    while s < w:
        a = a + pltpu.roll(a, s, 0)
        s *= 2
    return a


def _conv_gate(hg, hv, cw_g, cw_v, cb_g, cb_v):
    cg = cb_g + hg[2] * cw_g[0:1] + hg[1] * cw_g[1:2] + hg[0] * cw_g[2:3]
    cv = cb_v + hv[2] * cw_v[0:1] + hv[1] * cw_v[1:2] + hv[0] * cw_v[2:3]
    return ((cg / (1.0 + jnp.exp2(cg * NEG_LOG2_E))) * cv).astype(BF16)


def _pair_heads(qcols):
    lane = lax.broadcasted_iota(jnp.int32, (1, LANES), 1)
    blocks = [qcols[:, b * LANES:(b + 1) * LANES] for b in range(D_ATTN // LANES)]
    per_block = LANES // HEAD_DIM
    out = []
    for hh in range(Q_PER_KV):
        a = blocks[hh // per_block]
        b = blocks[Q_PER_KV // per_block + hh // per_block]
        if hh % per_block == 0:
            out.append(jnp.where(lane < HEAD_DIM, a, pltpu.roll(b, HEAD_DIM, 1)))
        else:
            out.append(jnp.where(lane < HEAD_DIM, pltpu.roll(a, HEAD_DIM, 1), b))
    return jnp.concatenate(out, axis=1)


def _stage_weights(s, win_c, wout_c, wup_c, wdown_c, w_in_s, w_out_s, w_up_s, w_down_s):
    r_in = win_c.shape[0]
    r0 = pl.multiple_of(s * r_in, r_in)
    c = win_c[...]
    w_in_s[pl.ds(r0, r_in), :] = jnp.concatenate(
        [_pair_heads(c[:, :D_ATTN]), c[:, D_ATTN:]], axis=1).astype(BF16)
    w_up_s[pl.ds(r0, r_in), :] = wup_c[...].astype(BF16)
    r_dn = wdown_c.shape[0]
    w_down_s[pl.ds(pl.multiple_of(s * r_dn, r_dn), r_dn), :] = wdown_c[...].astype(BF16)
    dst = jnp.where(s < N_Q_HEADS, (s % Q_PER_KV) * N_KV_HEADS + s // Q_PER_KV, s)
    w_out_s[pl.ds(pl.multiple_of(dst * HEAD_DIM, HEAD_DIM), HEAD_DIM), :] = wout_c[...].astype(BF16)


def _prompt_kernel(sinks_ref, x_ref, cos_ref, sa_ref, sb_ref,
                   gmix_ref, win_c, qg_ref, kg_ref, bd_ref, eye_ref, wpool_ref, pscale_ref, wout_c,
                   gffn_ref, wup_c, cw_ref, cb_ref, wdown_c,
                   xs_ref, cos_s_ref, sa_s_ref, sb_s_ref, ck_ref, cv_ref, pprev_ref, cprev_ref,
                   y_ref, klast_ref, vlast_ref, ulast_ref, hlast_ref,
                   ys_ref, kout_ref, vout_ref, uout_ref, hout_ref,
                   w_in_s, w_out_s, w_up_s, w_down_s,
                   kfull, vfull, vt, ufull, hprev, mix_buf, act_buf, ufull_s, mix_s,
                   *, n_tiles, n_total):
    s = pl.program_id(0)

    @pl.when(s < N_PREP)
    def _():
        _stage_weights(s, win_c, wout_c, wup_c, wdown_c, w_in_s, w_out_s, w_up_s, w_down_s)

    def tile_body(sample):
        _prompt_body((s - N_PREP) % n_tiles, sinks_ref, x_ref, cos_ref, sa_ref, sb_ref,
                     gmix_ref, w_in_s, qg_ref, kg_ref, bd_ref, eye_ref, wpool_ref, pscale_ref, w_out_s,
                     gffn_ref, w_up_s, cw_ref, cb_ref, w_down_s,
                     y_ref, klast_ref, vlast_ref, ulast_ref, hlast_ref,
                     kfull, vfull, vt, ufull, hprev, mix_buf, act_buf, sample=sample)

    @pl.when((s >= N_PREP) & (s < N_PREP + n_total - 1))
    def _():
        tile_body(None)

    @pl.when(s == N_PREP + n_total - 1)
    def _():
        tile_body(_SampleRefs(
            xs_ref, (cos_s_ref, sa_s_ref, sb_s_ref), ck_ref, cv_ref, pprev_ref, cprev_ref,
            ys_ref, kout_ref, vout_ref, uout_ref, hout_ref, ufull_s, mix_s))


def _prompt_body(i, sinks_ref, x_ref, cos_ref, sa_ref, sb_ref,
                 gmix_ref, win_ref, qg_ref, kg_ref, bd_ref, eye_ref, wpool_ref, pscale_ref, wout_ref,
                 gffn_ref, wup_ref, cw_ref, cb_ref, wdown_ref,
                 y_ref, klast_ref, vlast_ref, ulast_ref, hlast_ref,
                 kfull, vfull, vt, ufull, hprev, mix_buf, act_buf, sample=None):
    T = x_ref.shape[0]
    d_ff = wdown_ref.shape[0]
    n_chunks = T // CHUNK

    @pl.when(i == 0)
    def _():
        kfull[0:WINDOW, :] = jnp.zeros((WINDOW, D_KV), BF16)
        kfull[WINDOW + T:, :] = jnp.zeros((ATT_KEYS - WINDOW - CHUNK, D_KV), BF16)
        vfull[0:WINDOW, :] = jnp.zeros((WINDOW, D_KV), BF16)
        vfull[WINDOW + T:, :] = jnp.zeros((ATT_KEYS - WINDOW - CHUNK, D_KV), BF16)
        for c in range(2):
            vt[c, D_KV:, :] = jnp.ones((VT_ROWS - D_KV, vt.shape[2]), BF16)
        ufull[0:POOL_HALO, :] = jnp.zeros((POOL_HALO, D_POOL), F32)
        hprev[...] = jnp.zeros(hprev.shape, F32)

    x = x_ref[...]
    tables = [cos_ref[...], sa_ref[...], sb_ref[...]]
    if sample is not None:
        x = jnp.concatenate([x, sample.x_ref[...]], axis=0)
        tables = [jnp.concatenate([t, ts[...]], axis=0) for t, ts in zip(tables, sample.tables)]
    q, k, v, u = _mixer_inputs(x, gmix_ref[...], win_ref[...], qg_ref[...], kg_ref[...],
                               bd_ref[...], *tables)
    if sample is not None:
        _sample_mixer(q[T:].astype(BF16), k[T:], v[T:], u[T:], sinks_ref, wpool_ref, pscale_ref, sample)
        q, k, v, u = q[:T], k[:T], v[:T], u[:T]
    klast_ref[...] = k[T - WINDOW:, :]
    vlast_ref[...] = v[T - WINDOW:, :]
    ulast_ref[...] = u[T - POOL_HALO:, :]
    kfull[WINDOW:WINDOW + T, :] = k.astype(BF16)
    vfull[WINDOW:WINDOW + T, :] = v.astype(BF16)
    ufull[POOL_HALO:POOL_HALO + T, :] = u

    eye = eye_ref[...]
    for c in range(2):
        vrows = vfull[c * CHUNK:c * CHUNK + T + WINDOW, :]
        vt[c, 0:D_KV, :] = _dot_nt(eye[:D_KV, :D_KV], vrows).astype(BF16)
    band = WINDOW + CHUNK
    n_q = N_Q_HEADS * CHUNK
    lane = lax.broadcasted_iota(jnp.int32, (1, D_ATTN), 1)
    head_of_col = lax.broadcasted_iota(jnp.int32, (1, n_q), 1) // CHUNK
    key_row = lax.broadcasted_iota(jnp.int32, (band, 1), 0)
    q_of_group = []
    for g in range(N_KV_HEADS):
        in_g = ((lane // HEAD_DIM) % N_KV_HEADS == g).astype(F32)
        q_of_group.append((q * in_g).astype(BF16))
    sink_row = jnp.full((1, n_q), sinks_ref[0], F32)
    for h8 in range(1, N_Q_HEADS):
        sink_row = jnp.where(head_of_col == h8, sinks_ref[h8], sink_row)
    half = Q_PER_KV * CHUNK
    for j0 in range(0, n_chunks, ATT_GROUP):
        chunk_ids = range(j0, min(j0 + ATT_GROUP, n_chunks))
        scores = []
        for j in chunk_ids:
            r0 = j * CHUNK
            qs = jnp.concatenate(
                [q_of_group[g][r0:r0 + CHUNK, hh * LANES:(hh + 1) * LANES]
                 for g in range(N_KV_HEADS) for hh in range(Q_PER_KV)], axis=0)
            scores.append(_dot_nt(kfull[r0:r0 + band, :], qs))
        probs = []
        for j, s in zip(chunk_ids, scores):
            r0 = j * CHUNK
            if r0 < WINDOW:
                first_valid = jnp.where(i == 0, WINDOW - r0, 0)
                s = jnp.where(key_row >= first_valid, s, NEG_INF)
            m = jnp.maximum(jnp.max(s, axis=0, keepdims=True), sink_row)
            probs.append((jnp.exp(s - m).astype(BF16), jnp.exp(sink_row - m)))
        outs_t = []
        for j, (e, sink_e) in zip(chunk_ids, probs):
            vtb = vt[j % 2, :, (j // 2) * LANES:(j // 2) * LANES + band]
            oa = _dot(vtb, e)
            inv_den = 1.0 / (oa[D_KV:D_KV + 1] + sink_e)
            outs_t.append(jnp.concatenate(
                [oa[g * HEAD_DIM:(g + 1) * HEAD_DIM, g * half:(g + 1) * half]
                 * inv_den[:, g * half:(g + 1) * half] for g in range(N_KV_HEADS)],
                axis=0).astype(BF16))
        for j, ot in zip(chunk_ids, outs_t):
            r0 = j * CHUNK
            o = _dot_nt(eye, ot)
            for hh in range(Q_PER_KV):
                mix_buf[r0:r0 + CHUNK, hh * LANES:(hh + 1) * LANES] = (
                    o[hh * CHUNK:(hh + 1) * CHUNK].astype(BF16))

    uf = ufull[...]
    pos = i * T + lax.broadcasted_iota(jnp.int32, (T, 1), 0)
    for gi, w in enumerate(POOL_WINDOWS):
        sl = slice(gi * POOL_GROUP, (gi + 1) * POOL_GROUP)
        tsum = _window_sum(uf[:, sl], w)[POOL_HALO:]
        cnt = jnp.minimum(pos + 1, w).astype(F32)
        d = (tsum / cnt - u[:, sl]).astype(BF16)
        pool = _dot(d, wpool_ref[gi].astype(BF16)) * pscale_ref[:, sl]
        mix_buf[:, D_ATTN + gi * POOL_GROUP:D_ATTN + (gi + 1) * POOL_GROUP] = pool.astype(BF16)

    kfull[0:WINDOW, :] = kfull[T:T + WINDOW, :]
    vfull[0:WINDOW, :] = vfull[T:T + WINDOW, :]
    ufull[0:POOL_HALO, :] = u[T - POOL_HALO:, :]

    mix = mix_buf[...]
    if sample is not None:
        mix = jnp.concatenate([mix, sample.mix_buf[...]], axis=0)
    x1 = x + _dot(mix, wout_ref[...])

    xn2 = _rms_rows(x1, gffn_ref[...]).astype(BF16)
    F = FF_CHUNK
    y, k_done, acts = x1, 0, []
    row_in_group = lax.broadcasted_iota(jnp.int32, (1, CONV_HALO, 1), 1)
    for c in range(d_ff // F):
        cols = (slice(c * F, (c + 1) * F), slice(d_ff + c * F, d_ff + (c + 1) * F))
        taps = []
        for half, cs in enumerate(cols):
            h_all = _dot(xn2, wup_ref[:, cs])
            h = h_all[:T]
            groups = jnp.concatenate([hprev[:, cs], h], axis=0).reshape(T // CONV_HALO + 1, CONV_HALO, F)
            shifted = []
            for sh in (1, 2):
                r = pltpu.roll(groups, sh, 1)
                shifted.append(jnp.where(row_in_group < sh, r[:-1], r[1:]).reshape(T, F))
            hlast_ref[:, cs] = h[T - CONV_HALO:, :]
            hprev[:, cs] = h[T - CONV_HALO:, :]
            tap = (h, shifted[0], shifted[1])
            if sample is not None:
                tap = tuple(jnp.concatenate([a, b], axis=0)
                            for a, b in zip(tap, _sample_taps(h_all[T:], cs, sample)))
            taps.append(tap)
        acts.append(_conv_gate(taps[0], taps[1], cw_ref[:, cols[0]], cw_ref[:, cols[1]],
                               cb_ref[:, cols[0]], cb_ref[:, cols[1]]))
        c_done = c - DOWN_LAG
        if c_done >= 0 and (c_done + 1) % DOWN_GROUP == 0:
            ks = slice((c_done + 1 - DOWN_GROUP) * F, (c_done + 1) * F)
            group = jnp.concatenate(acts[c_done + 1 - DOWN_GROUP:c_done + 1], axis=1)
            y = y + _dot(group, wdown_ref[ks, :])
            k_done = (c_done + 1) * F
    y = y + _dot(jnp.concatenate(acts[k_done // F:], axis=1), wdown_ref[k_done:, :])
    y_ref[...] = y[:T]
    if sample is not None:
        sample.y_ref[...] = y[T:]


def _const_spec(shape):
    nd = len(shape)
    return pl.BlockSpec(shape, lambda *_: (0,) * nd)


def _chunk_spec(stacked, layer):
    rows = stacked.shape[1] // N_PREP
    return pl.BlockSpec((None, rows, stacked.shape[2]),
                        lambda s: (layer, jnp.minimum(s, N_PREP - 1), 0))


def _weight_specs(w, D, d_ff):
    layer = w["layer"]
    return [
        _const_spec((1, D)), _chunk_spec(w["w_in"], layer),
        _const_spec((1, D_ATTN)), _const_spec((1, D_KV)), _const_spec((MXU_COLS, MXU_COLS)),
        _const_spec((MXU_COLS, MXU_COLS)),
        _const_spec(w["w_pool"].shape), _const_spec((1, D_POOL)), _chunk_spec(w["w_out"], layer),
        _const_spec((1, D)), _chunk_spec(w["w_up"], layer),
        _const_spec((CONV_W, 2 * d_ff)), _const_spec((1, 2 * d_ff)), _chunk_spec(w["w_down"], layer),
    ]


def _weight_args(w):
    return (w["g_mix"], w["w_in"], w["q_gain"], w["k_gain"], w["bd"], w["eye"], w["w_pool"],
            w["pool_scale"], w["w_out"], w["g_ffn"], w["w_up"], w["conv_w"], w["conv_b"], w["w_down"])


def _weight_scratch(w):
    return [pltpu.VMEM(w[name].shape[1:], BF16) for name in ("w_in", "w_out", "w_up", "w_down")]


def _resident_spec(shape):
    nd = len(shape)
    return pl.BlockSpec(shape, lambda *_: (0,) * nd, pipeline_mode=pl.Buffered(1))


def _layer_call(x, tabs, xs, tabs_s, cache_k, cache_v, pool_prev, conv_prev, w):
    B, L, D = x.shape
    S, n_new, _ = xs.shape
    M = S * n_new
    T = PROMPT_TILE
    d_ff = w["w_down"].shape[1]
    n_tiles = L // T
    n_total = B * n_tiles
    past = cache_k.shape[1]
    tile_of = lambda s: jnp.clip(s - N_PREP, 0, n_total - 1)
    tile_spec = pl.BlockSpec((None, T, D), lambda s: (tile_of(s) // n_tiles, tile_of(s) % n_tiles, 0))
    tab_spec = pl.BlockSpec((T, LANES), lambda s: (tile_of(s) % n_tiles, 0))

    def last_spec(rows, cols):
        return pl.BlockSpec((None, rows, cols), lambda s: (tile_of(s) // n_tiles, 0, 0))

    pprev = jnp.pad(pool_prev, ((0, 0), (n_new - pool_prev.shape[1], 0), (0, 0)))
    sample_in = (xs.reshape(M, D), tabs_s[0], tabs_s[1], tabs_s[2],
                 cache_k.reshape(S, past, D_KV), cache_v.reshape(S, past, D_KV), pprev, conv_prev)
    in_specs = ([pl.BlockSpec(memory_space=pltpu.SMEM), tile_spec, tab_spec, tab_spec, tab_spec]
                + _weight_specs(w, D, d_ff) + [_resident_spec(a.shape) for a in sample_in])
    sample_out = (
        jax.ShapeDtypeStruct((M, D), F32),
        jax.ShapeDtypeStruct((S, WINDOW, D_KV), F32),
        jax.ShapeDtypeStruct((S, WINDOW, D_KV), F32),
        jax.ShapeDtypeStruct((M, D_POOL), F32),
        jax.ShapeDtypeStruct((S, CONV_HALO, 2 * d_ff), F32),
    )
    out_shape = (
        jax.ShapeDtypeStruct((B, L, D), F32),
        jax.ShapeDtypeStruct((B, WINDOW, D_KV), F32),
        jax.ShapeDtypeStruct((B, WINDOW, D_KV), F32),
        jax.ShapeDtypeStruct((B, POOL_HALO, D_POOL), F32),
        jax.ShapeDtypeStruct((B, CONV_HALO, 2 * d_ff), F32),
    ) + sample_out
    out_specs = (tile_spec, last_spec(WINDOW, D_KV), last_spec(WINDOW, D_KV),
                 last_spec(POOL_HALO, D_POOL), last_spec(CONV_HALO, 2 * d_ff)
                 ) + tuple(_resident_spec(o.shape) for o in sample_out)
    scratch = _weight_scratch(w) + [
        pltpu.VMEM((T + ATT_KEYS - CHUNK, D_KV), BF16),
        pltpu.VMEM((T + ATT_KEYS - CHUNK, D_KV), BF16),
        pltpu.VMEM((2, VT_ROWS, T + WINDOW), BF16),
        pltpu.VMEM((POOL_HALO + T, D_POOL), F32),
        pltpu.VMEM((CONV_HALO, 2 * d_ff), F32),
        pltpu.VMEM((T, D_ATTN + D_POOL), BF16),
        pltpu.VMEM((T, d_ff), BF16),
        pltpu.VMEM((2 * M, D_POOL), F32),
        pltpu.VMEM((M, D_ATTN + D_POOL), BF16),
    ]
    outs = pl.pallas_call(
        functools.partial(_prompt_kernel, n_tiles=n_tiles, n_total=n_total),
        out_shape=out_shape,
        grid=(N_PREP + n_total,),
        in_specs=in_specs,
        out_specs=out_specs,
        scratch_shapes=scratch,
        name="layer",
        compiler_params=pltpu.CompilerParams(
            dimension_semantics=("arbitrary",),
            vmem_limit_bytes=VMEM_LIMIT_BYTES),
    )(w["sinks"], x, tabs[0], tabs[1], tabs[2], *_weight_args(w), *sample_in)
    ys, ko, vo, uo, ho = outs[5:]
    return outs[:5], (ys.reshape(S, n_new, D), ko, vo, uo.reshape(S, n_new, D_POOL), ho)


class _SampleRefs(NamedTuple):
    x_ref: Any
    tables: Any
    ck_ref: Any
    cv_ref: Any
    pprev_ref: Any
    cprev_ref: Any
    y_ref: Any
    kout_ref: Any
    vout_ref: Any
    uout_ref: Any
    hout_ref: Any
    ufull: Any
    mix_buf: Any


def _sample_mixer(q, k, v, u, sinks_ref, wpool_ref, pscale_ref, sample):
    ck_ref, cv_ref, mix_buf, ufull = sample.ck_ref, sample.cv_ref, sample.mix_buf, sample.ufull
    M = q.shape[0]
    n_streams, past = ck_ref.shape[0], ck_ref.shape[1]
    n_new = M // n_streams
    sample.uout_ref[...] = u

    for s_ in range(n_streams):
        rows = slice(s_ * n_new, (s_ + 1) * n_new)
        keys = jnp.concatenate([ck_ref[s_], k[rows]], axis=0)
        vals = jnp.concatenate([cv_ref[s_], v[rows]], axis=0)
        sample.kout_ref[s_] = keys[past + n_new - WINDOW:, :]
        sample.vout_ref[s_] = vals[past + n_new - WINDOW:, :]
        keys16 = keys.astype(BF16)
        vals16 = vals.astype(BF16)
        outs = []
        for g in range(N_KV_HEADS):
            sink_col = jnp.concatenate(
                [jnp.full((n_new, 1), sinks_ref[g * Q_PER_KV + hh], F32) for hh in range(Q_PER_KV)],
                axis=0)
            qs = jnp.concatenate(
                [q[rows, hh * LANES + g * HEAD_DIM:hh * LANES + (g + 1) * HEAD_DIM]
                 for hh in range(Q_PER_KV)], axis=0)
            sc = _dot_nt(qs, keys16[:, g * HEAD_DIM:(g + 1) * HEAD_DIM])
            m = jnp.maximum(jnp.max(sc, axis=-1, keepdims=True), sink_col)
            e = jnp.exp(sc - m)
            den = jnp.sum(e, axis=-1, keepdims=True) + jnp.exp(sink_col - m)
            outs.append(_dot(e.astype(BF16), vals16[:, g * HEAD_DIM:(g + 1) * HEAD_DIM]) * (1.0 / den))
        for hh in range(Q_PER_KV):
            pair = jnp.concatenate([o[hh * n_new:(hh + 1) * n_new] for o in outs], axis=1)
            mix_buf[rows, hh * LANES:(hh + 1) * LANES] = pair.astype(BF16)
        ufull[s_ * 2 * n_new:s_ * 2 * n_new + n_new, :] = sample.pprev_ref[s_]
        ufull[s_ * 2 * n_new + n_new:(s_ + 1) * 2 * n_new, :] = u[rows]

    uf = ufull[...]
    for gi, w in enumerate(POOL_WINDOWS):
        sl = slice(gi * POOL_GROUP, (gi + 1) * POOL_GROUP)
        ws = _window_sum(uf[:, sl], w)
        tsum = jnp.concatenate(
            [ws[s_ * 2 * n_new + n_new:(s_ + 1) * 2 * n_new] for s_ in range(n_streams)], axis=0)
        d = (tsum / float(w) - u[:, sl]).astype(BF16)
        pool = _dot(d, wpool_ref[gi].astype(BF16)) * pscale_ref[:, sl]
        mix_buf[:, D_ATTN + gi * POOL_GROUP:D_ATTN + (gi + 1) * POOL_GROUP] = pool.astype(BF16)


def _sample_taps(h, cs, sample):
    cprev_ref, hout_ref = sample.cprev_ref, sample.hout_ref
    M, F = h.shape
    n_streams, n_conv = cprev_ref.shape[0], cprev_ref.shape[1]
    n_new = M // n_streams
    t_in_stream = lax.broadcasted_iota(jnp.int32, (M, 1), 0) % n_new
    hout_ref[:, :, cs] = h.reshape(n_streams, n_new, F)[:, n_new - hout_ref.shape[1]:, :]
    prev = [jnp.broadcast_to(cprev_ref[:, j:j + 1, cs], (n_streams, n_new, F)).reshape(M, F)
            for j in range(n_conv)]
    return (h, jnp.where(t_in_stream >= 1, pltpu.roll(h, 1, 0), prev[1]),
            jnp.where(t_in_stream >= 2, pltpu.roll(h, 2, 0),
                      jnp.where(t_in_stream == 0, prev[0], prev[1])))


def _rope_tables(pos, reps=1):
    half = ROT_DIM // 2
    inv = ROPE_THETA ** (-jnp.arange(0, ROT_DIM, 2, dtype=F32) / ROT_DIM)
    ang = pos.astype(F32)[:, None] * inv[None, :]
    cos, sin = jnp.cos(ang), jnp.sin(ang)
    n = pos.shape[0]
    ones = jnp.ones((n, HEAD_DIM - ROT_DIM), F32)
    zeros_h = jnp.zeros((n, half), F32)
    zeros_r = jnp.zeros((n, HEAD_DIM - ROT_DIM), F32)
    c = jnp.concatenate([cos, cos, ones], axis=1)
    sa = jnp.concatenate([-sin, zeros_h, zeros_r], axis=1)
    sb = jnp.concatenate([zeros_h, sin, zeros_r], axis=1)
    tile = lambda t: jnp.tile(t, (reps, LANES // HEAD_DIM))
    return tile(c), tile(sa), tile(sb)


def kernel(x_prompt, x_sample, cache_k, cache_v, state_pool, state_conv, norm_mix, w_in, q_norm,
           k_norm, attn_sinks, w_pool, pool_scale, w_out, norm_ffn, w_up, conv_w, conv_b, w_down):
    depth = w_in.shape[0]
    B, L, D = x_prompt.shape
    S, n_new, _ = x_sample.shape
    past_len = L
    assert L % PROMPT_TILE == 0 and PROMPT_TILE % CHUNK == 0 and PROMPT_TILE >= WINDOW
    assert w_down.shape[1] % FF_CHUNK == 0
    assert all(m.shape[1] % (16 * N_PREP) == 0 for m in (w_in, w_up, w_down))
    assert w_out.shape[1] == N_PREP * HEAD_DIM and LANES == N_KV_HEADS * HEAD_DIM

    tabs_p = _rope_tables(jnp.arange(L))
    tabs_s = _rope_tables(past_len + jnp.arange(n_new), reps=S)
    head_id = jnp.arange(MXU_COLS) // HEAD_DIM
    bd = jnp.where(head_id[:, None] == head_id[None, :], 1.0 / HEAD_DIM, 0.0).astype(BF16)
    q_scale = HEAD_DIM ** -0.5

    eye = jnp.eye(MXU_COLS, dtype=BF16)

    yp, ys = x_prompt, x_sample
    outs = [[] for _ in range(8)]
    for i in range(depth):
        w = dict(
            layer=i, sinks=attn_sinks[i],
            g_mix=norm_mix[i][None, :], w_in=w_in,
            q_gain=jnp.tile(q_norm[i] * q_scale, N_Q_HEADS)[None, :], k_gain=jnp.tile(k_norm[i], N_KV_HEADS)[None, :],
            bd=bd, eye=eye, w_pool=w_pool[i], pool_scale=pool_scale[i][None, :],
            w_out=w_out, g_ffn=norm_ffn[i][None, :], w_up=w_up,
            conv_w=conv_w[i], conv_b=conv_b[i][None, :], w_down=w_down,
        )
        (yp, k1, v1, u1, h1), (ys, k2, v2, u2, h2) = _layer_call(
            yp, tabs_p, ys, tabs_s, cache_k[i], cache_v[i], state_pool[i], state_conv[i], w)
        n_pool = state_pool.shape[2]
        n_conv = state_conv.shape[2]
        outs[0].append(k1.reshape(B, WINDOW, N_KV_HEADS, HEAD_DIM))
        outs[1].append(v1.reshape(B, WINDOW, N_KV_HEADS, HEAD_DIM))
        outs[2].append(u1[:, POOL_HALO - n_pool:])
        outs[3].append(h1[:, CONV_HALO - n_conv:])
        outs[4].append(k2.reshape(S, WINDOW, N_KV_HEADS, HEAD_DIM))
        outs[5].append(v2.reshape(S, WINDOW, N_KV_HEADS, HEAD_DIM))
        outs[6].append(u2[:, n_new - n_pool:])
        outs[7].append(h2[:, CONV_HALO - n_conv:])
    return (yp, ys) + tuple(jnp.stack(o) for o in outs)
```

```python
import functools
from typing import Any, NamedTuple

import jax
import jax.numpy as jnp
from jax import lax
from jax.experimental import pallas as pl
from jax.experimental.pallas import tpu as pltpu

F32 = jnp.float32
BF16 = jnp.bfloat16

CHUNK = 64
HEAD_DIM = 64
N_Q_HEADS = 8
N_KV_HEADS = 2
Q_PER_KV = N_Q_HEADS // N_KV_HEADS
D_ATTN = N_Q_HEADS * HEAD_DIM
D_KV = N_KV_HEADS * HEAD_DIM
WINDOW = 128
ROT_DIM = 16
ROPE_THETA = 500000.0
POOL_WINDOWS = (2, 4, 8, 16)
POOL_GROUP = 128
D_POOL = POOL_GROUP * len(POOL_WINDOWS)
POOL_HALO = 16
CONV_W = 3
CONV_HALO = 8
EPS = 1e-6
NEG_INF = -1e30
NEG_LOG2_E = -1.4426950408889634

LANES = 128
MXU_COLS = 256
BF16_SUBLANES = 16
KV_PAD = CHUNK
VT_ROWS = D_KV + BF16_SUBLANES

PROMPT_TILE = 512
ATT_GROUP = 4
FF_CHUNK = 256
DOWN_GROUP = 3
DOWN_LAG = 2
N_PREP = 16
VMEM_LIMIT_BYTES = 56 * 1024 * 1024


def _dot(a, b):
    return jnp.dot(a, b, preferred_element_type=F32)


def _dot_nt(a, b):
    return lax.dot_general(a, b, (((1,), (1,)), ((), ())), preferred_element_type=F32)


def _rms_rows(x, gain):
    ms = jnp.mean(x * x, axis=-1, keepdims=True)
    return x * lax.rsqrt(ms + EPS) * gain


def _head_rms(t, bd, gain):
    ms = _dot((t * t).astype(BF16), bd)
    return t * lax.rsqrt(ms + EPS) * gain


def _rope(t, cos, sin_a, sin_b):
    return (t * cos + pltpu.roll(t, LANES - ROT_DIM // 2, 1) * sin_a
            + pltpu.roll(t, ROT_DIM // 2, 1) * sin_b)


def _mixer_inputs(x, gmix, w_in, qg, kg, bd, cos, sin_a, sin_b):
    xn = _rms_rows(x, gmix).astype(BF16)
    h = _dot(xn, w_in)
    q_parts = []
    for j in range(D_ATTN // MXU_COLS):
        qb = _head_rms(h[:, j * MXU_COLS:(j + 1) * MXU_COLS], bd,
                       qg[:, j * MXU_COLS:(j + 1) * MXU_COLS])
        for l in range(MXU_COLS // LANES):
            q_parts.append(_rope(qb[:, l * LANES:(l + 1) * LANES], cos, sin_a, sin_b))
    q = jnp.concatenate(q_parts, axis=1)
    k = _head_rms(h[:, D_ATTN:D_ATTN + D_KV], bd[:D_KV, :D_KV], kg)
    k = _rope(k, cos, sin_a, sin_b)
    v = h[:, D_ATTN + D_KV:D_ATTN + 2 * D_KV]
    u = h[:, D_ATTN + 2 * D_KV:]
    return q, k, v, u


def _window_sum(a, w):
    s = 1
    while s < w:
        a = a + pltpu.roll(a, s, 0)
        s *= 2
    return a


def _conv_gate(hg, hv, cw_g, cw_v, cb_g, cb_v):
    cg = cb_g + hg[2] * cw_g[0:1] + hg[1] * cw_g[1:2] + hg[0] * cw_g[2:3]
    cv = cb_v + hv[2] * cw_v[0:1] + hv[1] * cw_v[1:2] + hv[0] * cw_v[2:3]
    return ((cg / (1.0 + jnp.exp2(cg * NEG_LOG2_E))) * cv).astype(BF16)


def _pair_heads(qcols):
    lane = lax.broadcasted_iota(jnp.int32, (1, LANES), 1)
    blocks = [qcols[:, b * LANES:(b + 1) * LANES] for b in range(D_ATTN // LANES)]
    per_block = LANES // HEAD_DIM
    out = []
    for hh in range(Q_PER_KV):
        a = blocks[hh // per_block]
        b = blocks[Q_PER_KV // per_block + hh // per_block]
        if hh % per_block == 0:
            out.append(jnp.where(lane < HEAD_DIM, a, pltpu.roll(b, HEAD_DIM, 1)))
        else:
            out.append(jnp.where(lane < HEAD_DIM, pltpu.roll(a, HEAD_DIM, 1), b))
    return jnp.concatenate(out, axis=1)


def _stage_weights(s, win_c, wout_c, wup_c, wdown_c, w_in_s, w_out_s, w_up_s, w_down_s):
    r_in = win_c.shape[0]
    r0 = pl.multiple_of(s * r_in, r_in)
    c = win_c[...]
    w_in_s[pl.ds(r0, r_in), :] = jnp.concatenate(
        [_pair_heads(c[:, :D_ATTN]), c[:, D_ATTN:]], axis=1).astype(BF16)
    w_up_s[pl.ds(r0, r_in), :] = wup_c[...].astype(BF16)
    r_dn = wdown_c.shape[0]
    w_down_s[pl.ds(pl.multiple_of(s * r_dn, r_dn), r_dn), :] = wdown_c[...].astype(BF16)
    dst = jnp.where(s < N_Q_HEADS, (s % Q_PER_KV) * N_KV_HEADS + s // Q_PER_KV, s)
    w_out_s[pl.ds(pl.multiple_of(dst * HEAD_DIM, HEAD_DIM), HEAD_DIM), :] = wout_c[...].astype(BF16)


def _prompt_kernel(sinks_ref, x_ref, cos_ref, sa_ref, sb_ref,
                   gmix_ref, win_c, qg_ref, kg_ref, bd_ref, wpool_ref, pscale_ref, wout_c,
                   gffn_ref, wup_c, cw_ref, cb_ref, wdown_c,
                   xs_ref, cos_s_ref, sa_s_ref, sb_s_ref, ck_ref, cv_ref, pprev_ref, cprev_ref,
                   y_ref, klast_ref, vlast_ref, ulast_ref, hlast_ref,
                   ys_ref, kout_ref, vout_ref, uout_ref, hout_ref,
                   w_in_s, w_out_s, w_up_s, w_down_s,
                   kfull, vfull, vt, ufull, hprev, mix_buf, ufull_s, mix_s,
                   *, n_tiles, n_total):
    s = pl.program_id(0)

    @pl.when(s < N_PREP)
    def _():
        _stage_weights(s, win_c, wout_c, wup_c, wdown_c, w_in_s, w_out_s, w_up_s, w_down_s)

    @pl.when((s >= N_PREP) & (s < N_PREP + n_total))
    def _():
        _prompt_body((s - N_PREP) % n_tiles, sinks_ref, x_ref, cos_ref, sa_ref, sb_ref,
                     gmix_ref, w_in_s, qg_ref, kg_ref, bd_ref, wpool_ref, pscale_ref, w_out_s,
                     gffn_ref, w_up_s, cw_ref, cb_ref, w_down_s,
                     y_ref, klast_ref, vlast_ref, ulast_ref, hlast_ref,
                     kfull, vfull, vt, ufull, hprev, mix_buf)

    @pl.when(s == N_PREP + n_total)
    def _():
        _sample_step(sinks_ref, gmix_ref, w_in_s, qg_ref, kg_ref, bd_ref, wpool_ref, pscale_ref,
                     w_out_s, gffn_ref, w_up_s, cw_ref, cb_ref, w_down_s,
                     _SampleRefs(xs_ref, (cos_s_ref, sa_s_ref, sb_s_ref), ck_ref, cv_ref, pprev_ref,
                                 cprev_ref, ys_ref, kout_ref, vout_ref, uout_ref, hout_ref,
                                 ufull_s, mix_s))


def _prompt_body(i, sinks_ref, x_ref, cos_ref, sa_ref, sb_ref,
                 gmix_ref, win_ref, qg_ref, kg_ref, bd_ref, wpool_ref, pscale_ref, wout_ref,
                 gffn_ref, wup_ref, cw_ref, cb_ref, wdown_ref,
                 y_ref, klast_ref, vlast_ref, ulast_ref, hlast_ref,
                 kfull, vfull, vt, ufull, hprev, mix_buf):
    T = x_ref.shape[0]
    d_ff = wdown_ref.shape[0]
    n_chunks = T // CHUNK

    @pl.when(i == 0)
    def _():
        kfull[0:WINDOW, :] = jnp.zeros((WINDOW, D_KV), BF16)
        kfull[WINDOW + T:, :] = jnp.zeros((KV_PAD, D_KV), BF16)
        vfull[0:WINDOW, :] = jnp.zeros((WINDOW, D_KV), BF16)
        vfull[WINDOW + T:, :] = jnp.zeros((KV_PAD, D_KV), BF16)
        for c in range(2):
            vt[c, D_KV:, :] = jnp.ones((VT_ROWS - D_KV, vt.shape[2]), BF16)
        ufull[0:POOL_HALO, :] = jnp.zeros((POOL_HALO, D_POOL), F32)
        hprev[...] = jnp.zeros(hprev.shape, F32)

    x = x_ref[...]
    q, k, v, u = _mixer_inputs(x, gmix_ref[...], win_ref[...], qg_ref[...], kg_ref[...],
                               bd_ref[...], cos_ref[...], sa_ref[...], sb_ref[...])
    klast_ref[...] = k[T - WINDOW:, :]
    vlast_ref[...] = v[T - WINDOW:, :]
    ulast_ref[...] = u[T - ulast_ref.shape[0]:, :]
    kfull[WINDOW:WINDOW + T, :] = k.astype(BF16)
    vfull[WINDOW:WINDOW + T, :] = v.astype(BF16)
    ufull[POOL_HALO:POOL_HALO + T, :] = u

    for c in range(2):
        vrows = vfull[c * CHUNK:c * CHUNK + T + WINDOW, :]
        vt[c, 0:D_KV, :] = vrows.astype(F32).T.astype(BF16)
    band = WINDOW + CHUNK
    n_q = N_Q_HEADS * CHUNK
    lane = lax.broadcasted_iota(jnp.int32, (1, D_ATTN), 1)
    head_of_col = lax.broadcasted_iota(jnp.int32, (1, n_q), 1) // CHUNK
    key_row = lax.broadcasted_iota(jnp.int32, (band, 1), 0)
    q_of_group = []
    for g in range(N_KV_HEADS):
        in_g = ((lane // HEAD_DIM) % N_KV_HEADS == g).astype(F32)
        q_of_group.append((q * in_g).astype(BF16))
    sink_row = jnp.full((1, n_q), sinks_ref[0], F32)
    for h8 in range(1, N_Q_HEADS):
        sink_row = jnp.where(head_of_col == h8, sinks_ref[h8], sink_row)
    half = Q_PER_KV * CHUNK
    for j0 in range(0, n_chunks, ATT_GROUP):
        chunk_ids = range(j0, min(j0 + ATT_GROUP, n_chunks))
        scores = []
        for j in chunk_ids:
            r0 = j * CHUNK
            qs = jnp.concatenate(
                [q_of_group[g][r0:r0 + CHUNK, hh * LANES:(hh + 1) * LANES]
                 for g in range(N_KV_HEADS) for hh in range(Q_PER_KV)], axis=0)
            scores.append(_dot_nt(kfull[r0:r0 + band, :], qs))
        probs = []
        for j, s in zip(chunk_ids, scores):
            r0 = j * CHUNK
            if r0 < WINDOW:
                first_valid = jnp.where(i == 0, WINDOW - r0, 0)
                s = jnp.where(key_row >= first_valid, s, NEG_INF)
            m = jnp.maximum(jnp.max(s, axis=0, keepdims=True), sink_row)
            probs.append((jnp.exp(s - m).astype(BF16), jnp.exp(sink_row - m)))
        outs_t = []
        for j, (e, sink_e) in zip(chunk_ids, probs):
            vtb = vt[j % 2, :, (j // 2) * LANES:(j // 2) * LANES + band]
            oa = _dot(vtb, e)
            inv_den = 1.0 / (oa[D_KV:D_KV + 1] + sink_e)
            outs_t.append(jnp.concatenate(
                [oa[g * HEAD_DIM:(g + 1) * HEAD_DIM, g * half:(g + 1) * half]
                 * inv_den[:, g * half:(g + 1) * half] for g in range(N_KV_HEADS)],
                axis=0))
        for j, ot in zip(chunk_ids, outs_t):
            r0 = j * CHUNK
            o = ot.T
            for hh in range(Q_PER_KV):
                mix_buf[r0:r0 + CHUNK, hh * LANES:(hh + 1) * LANES] = (
                    o[hh * CHUNK:(hh + 1) * CHUNK].astype(BF16))

    uf = ufull[...]
    pos = i * T + lax.broadcasted_iota(jnp.int32, (T, 1), 0)
    for gi, w in enumerate(POOL_WINDOWS):
        sl = slice(gi * POOL_GROUP, (gi + 1) * POOL_GROUP)
        tsum = _window_sum(uf[:, sl], w)[POOL_HALO:]
        cnt = jnp.minimum(pos + 1, w).astype(F32)
        d = (tsum / cnt - u[:, sl]).astype(BF16)
        pool = _dot(d, wpool_ref[gi].astype(BF16)) * pscale_ref[:, sl]
        mix_buf[:, D_ATTN + gi * POOL_GROUP:D_ATTN + (gi + 1) * POOL_GROUP] = pool.astype(BF16)

    kfull[0:WINDOW, :] = kfull[T:T + WINDOW, :]
    vfull[0:WINDOW, :] = vfull[T:T + WINDOW, :]
    ufull[0:POOL_HALO, :] = u[T - POOL_HALO:, :]

    x1 = x + _dot(mix_buf[...], wout_ref[...])

    xn2 = _rms_rows(x1, gffn_ref[...]).astype(BF16)
    F = FF_CHUNK
    y, k_done, acts = x1, 0, []
    row_in_group = lax.broadcasted_iota(jnp.int32, (1, CONV_HALO, 1), 1)
    for c in range(d_ff // F):
        cols = (slice(c * F, (c + 1) * F), slice(d_ff + c * F, d_ff + (c + 1) * F))
        taps = []
        for half, cs in enumerate(cols):
            h = _dot(xn2, wup_ref[:, cs])
            groups = jnp.concatenate([hprev[:, cs], h], axis=0).reshape(T // CONV_HALO + 1, CONV_HALO, F)
            shifted = []
            for sh in (1, 2):
                r = pltpu.roll(groups, sh, 1)
                shifted.append(jnp.where(row_in_group < sh, r[:-1], r[1:]).reshape(T, F))
            hlast_ref[:, cs] = h[T - hlast_ref.shape[0]:, :]
            hprev[:, cs] = h[T - CONV_HALO:, :]
            taps.append((h, shifted[0], shifted[1]))
        acts.append(_conv_gate(taps[0], taps[1], cw_ref[:, cols[0]], cw_ref[:, cols[1]],
                               cb_ref[:, cols[0]], cb_ref[:, cols[1]]))
        c_done = c - DOWN_LAG
        if c_done >= 0 and (c_done + 1) % DOWN_GROUP == 0:
            ks = slice((c_done + 1 - DOWN_GROUP) * F, (c_done + 1) * F)
            group = jnp.concatenate(acts[c_done + 1 - DOWN_GROUP:c_done + 1], axis=1)
            y = y + _dot(group, wdown_ref[ks, :])
            k_done = (c_done + 1) * F
    y_ref[...] = y + _dot(jnp.concatenate(acts[k_done // F:], axis=1), wdown_ref[k_done:, :])


def _const_spec(shape):
    nd = len(shape)
    return pl.BlockSpec(shape, lambda *_: (0,) * nd)


def _chunk_spec(stacked, layer):
    rows = stacked.shape[1] // N_PREP
    return pl.BlockSpec((None, rows, stacked.shape[2]),
                        lambda s: (layer, jnp.minimum(s, N_PREP - 1), 0))


def _layer_spec(stacked, layer, resident=False):
    nd = stacked.ndim - 1
    return pl.BlockSpec((None,) + stacked.shape[1:], lambda *_: (layer,) + (0,) * nd,
                        pipeline_mode=pl.Buffered(1) if resident else None)


def _weight_specs(w):
    layer = w["layer"]
    return [
        _layer_spec(w["g_mix"], layer), _chunk_spec(w["w_in"], layer),
        _layer_spec(w["q_gain"], layer), _layer_spec(w["k_gain"], layer), _const_spec(w["bd"].shape),
        _layer_spec(w["w_pool"], layer), _layer_spec(w["pool_scale"], layer), _chunk_spec(w["w_out"], layer),
        _layer_spec(w["g_ffn"], layer), _chunk_spec(w["w_up"], layer),
        _layer_spec(w["conv_w"], layer), _layer_spec(w["conv_b"], layer), _chunk_spec(w["w_down"], layer),
    ]


def _weight_args(w):
    return (w["g_mix"], w["w_in"], w["q_gain"], w["k_gain"], w["bd"], w["w_pool"],
            w["pool_scale"], w["w_out"], w["g_ffn"], w["w_up"], w["conv_w"], w["conv_b"], w["w_down"])


def _weight_scratch(w):
    return [pltpu.VMEM(w[name].shape[1:], BF16) for name in ("w_in", "w_out", "w_up", "w_down")]


def _resident_spec(shape):
    nd = len(shape)
    return pl.BlockSpec(shape, lambda *_: (0,) * nd, pipeline_mode=pl.Buffered(1))


def _layer_call(x, tabs, xs, tabs_s, w):
    B, L, D = x.shape
    S, n_new, _ = xs.shape
    M = S * n_new
    T = PROMPT_TILE
    d_ff = w["w_down"].shape[1]
    n_tiles = L // T
    n_total = B * n_tiles
    layer = w["layer"]
    n_pool, n_conv = w["n_pool"], w["conv_prev"].shape[2]
    tile_of = lambda s: jnp.clip(s - N_PREP, 0, n_total - 1)
    tile_spec = pl.BlockSpec((None, T, D), lambda s: (tile_of(s) // n_tiles, tile_of(s) % n_tiles, 0))
    tab_spec = pl.BlockSpec((T, LANES), lambda s: (tile_of(s) % n_tiles, 0))

    def last_spec(rows, cols):
        return pl.BlockSpec((None, rows, cols), lambda s: (tile_of(s) // n_tiles, 0, 0))

    sample_acts = (xs.reshape(M, D), tabs_s[0], tabs_s[1], tabs_s[2])
    sample_state = (w["cache_k"], w["cache_v"], w["pool_prev"], w["conv_prev"])
    in_specs = ([pl.BlockSpec(memory_space=pltpu.SMEM), tile_spec, tab_spec, tab_spec, tab_spec]
                + _weight_specs(w) + [_resident_spec(a.shape) for a in sample_acts]
                + [_layer_spec(a, layer, resident=True) for a in sample_state])
    sample_out = (
        jax.ShapeDtypeStruct((M, D), F32),
        jax.ShapeDtypeStruct((S, WINDOW, D_KV), F32),
        jax.ShapeDtypeStruct((S, WINDOW, D_KV), F32),
        jax.ShapeDtypeStruct((S, n_pool, D_POOL), F32),
        jax.ShapeDtypeStruct((S, n_conv, 2 * d_ff), F32),
    )
    out_shape = (
        jax.ShapeDtypeStruct((B, L, D), F32),
        jax.ShapeDtypeStruct((B, WINDOW, D_KV), F32),
        jax.ShapeDtypeStruct((B, WINDOW, D_KV), F32),
        jax.ShapeDtypeStruct((B, n_pool, D_POOL), F32),
        jax.ShapeDtypeStruct((B, n_conv, 2 * d_ff), F32),
    ) + sample_out
    out_specs = (tile_spec, last_spec(WINDOW, D_KV), last_spec(WINDOW, D_KV),
                 last_spec(n_pool, D_POOL), last_spec(n_conv, 2 * d_ff)
                 ) + tuple(_resident_spec(o.shape) for o in sample_out)
    scratch = _weight_scratch(w) + [
        pltpu.VMEM((WINDOW + T + KV_PAD, D_KV), BF16),
        pltpu.VMEM((WINDOW + T + KV_PAD, D_KV), BF16),
        pltpu.VMEM((2, VT_ROWS, T + WINDOW), BF16),
        pltpu.VMEM((POOL_HALO + T, D_POOL), F32),
        pltpu.VMEM((CONV_HALO, 2 * d_ff), F32),
        pltpu.VMEM((T, D_ATTN + D_POOL), BF16),
        pltpu.VMEM((2 * M, D_POOL), F32),
        pltpu.VMEM((M, D_ATTN + D_POOL), BF16),
    ]
    outs = pl.pallas_call(
        functools.partial(_prompt_kernel, n_tiles=n_tiles, n_total=n_total),
        out_shape=out_shape,
        grid=(N_PREP + n_total + 1,),
        in_specs=in_specs,
        out_specs=out_specs,
        scratch_shapes=scratch,
        name="layer",
        compiler_params=pltpu.CompilerParams(
            dimension_semantics=("arbitrary",),
            vmem_limit_bytes=VMEM_LIMIT_BYTES),
    )(w["sinks"], x, tabs[0], tabs[1], tabs[2], *_weight_args(w), *sample_acts, *sample_state)
    ys, ko, vo, uo, ho = outs[5:]
    return outs[:5], (ys.reshape(S, n_new, D), ko, vo, uo, ho)


class _SampleRefs(NamedTuple):
    x_ref: Any
    tables: Any
    ck_ref: Any
    cv_ref: Any
    pprev_ref: Any
    cprev_ref: Any
    y_ref: Any
    kout_ref: Any
    vout_ref: Any
    uout_ref: Any
    hout_ref: Any
    ufull: Any
    mix_buf: Any


def _sample_mixer(q, k, v, u, sinks_ref, wpool_ref, pscale_ref, sample):
    ck_ref, cv_ref, mix_buf, ufull = sample.ck_ref, sample.cv_ref, sample.mix_buf, sample.ufull
    M = q.shape[0]
    n_streams, past = ck_ref.shape[0], ck_ref.shape[1]
    n_new = M // n_streams
    uout_ref = sample.uout_ref
    uout_ref[...] = u.reshape(n_streams, n_new, u.shape[1])[:, n_new - uout_ref.shape[1]:, :]

    for s_ in range(n_streams):
        rows = slice(s_ * n_new, (s_ + 1) * n_new)
        keys = jnp.concatenate([ck_ref[s_], k[rows]], axis=0)
        vals = jnp.concatenate([cv_ref[s_], v[rows]], axis=0)
        sample.kout_ref[s_] = keys[past + n_new - WINDOW:, :]
        sample.vout_ref[s_] = vals[past + n_new - WINDOW:, :]
        keys16 = keys.astype(BF16)
        vals16 = vals.astype(BF16)
        outs = []
        for g in range(N_KV_HEADS):
            sink_col = jnp.concatenate(
                [jnp.full((n_new, 1), sinks_ref[g * Q_PER_KV + hh], F32) for hh in range(Q_PER_KV)],
                axis=0)
            qs = jnp.concatenate(
                [q[rows, hh * LANES + g * HEAD_DIM:hh * LANES + (g + 1) * HEAD_DIM]
                 for hh in range(Q_PER_KV)], axis=0)
            sc = _dot_nt(qs, keys16[:, g * HEAD_DIM:(g + 1) * HEAD_DIM])
            m = jnp.maximum(jnp.max(sc, axis=-1, keepdims=True), sink_col)
            e = jnp.exp(sc - m)
            den = jnp.sum(e, axis=-1, keepdims=True) + jnp.exp(sink_col - m)
            outs.append(_dot(e.astype(BF16), vals16[:, g * HEAD_DIM:(g + 1) * HEAD_DIM]) * (1.0 / den))
        for hh in range(Q_PER_KV):
            pair = jnp.concatenate([o[hh * n_new:(hh + 1) * n_new] for o in outs], axis=1)
            mix_buf[rows, hh * LANES:(hh + 1) * LANES] = pair.astype(BF16)
        ufull[s_ * 2 * n_new:s_ * 2 * n_new + n_new, :] = sample.pprev_ref[s_]
        ufull[s_ * 2 * n_new + n_new:(s_ + 1) * 2 * n_new, :] = u[rows]

    uf = ufull[...]
    for gi, w in enumerate(POOL_WINDOWS):
        sl = slice(gi * POOL_GROUP, (gi + 1) * POOL_GROUP)
        ws = _window_sum(uf[:, sl], w)
        tsum = jnp.concatenate(
            [ws[s_ * 2 * n_new + n_new:(s_ + 1) * 2 * n_new] for s_ in range(n_streams)], axis=0)
        d = (tsum / float(w) - u[:, sl]).astype(BF16)
        pool = _dot(d, wpool_ref[gi].astype(BF16)) * pscale_ref[:, sl]
        mix_buf[:, D_ATTN + gi * POOL_GROUP:D_ATTN + (gi + 1) * POOL_GROUP] = pool.astype(BF16)


def _sample_taps(h, cs, sample):
    cprev_ref, hout_ref = sample.cprev_ref, sample.hout_ref
    M, F = h.shape
    n_streams, n_conv = cprev_ref.shape[0], cprev_ref.shape[1]
    n_new = M // n_streams
    t_in_stream = lax.broadcasted_iota(jnp.int32, (M, 1), 0) % n_new
    hout_ref[:, :, cs] = h.reshape(n_streams, n_new, F)[:, n_new - hout_ref.shape[1]:, :]
    prev = [jnp.broadcast_to(cprev_ref[:, j:j + 1, cs], (n_streams, n_new, F)).reshape(M, F)
            for j in range(n_conv)]
    return (h, jnp.where(t_in_stream >= 1, pltpu.roll(h, 1, 0), prev[1]),
            jnp.where(t_in_stream >= 2, pltpu.roll(h, 2, 0),
                      jnp.where(t_in_stream == 0, prev[0], prev[1])))


def _sample_step(sinks_ref, gmix_ref, win_ref, qg_ref, kg_ref, bd_ref, wpool_ref, pscale_ref,
                 wout_ref, gffn_ref, wup_ref, cw_ref, cb_ref, wdown_ref, sample):
    d_ff = wdown_ref.shape[0]
    x = sample.x_ref[...]
    q, k, v, u = _mixer_inputs(x, gmix_ref[...], win_ref[...], qg_ref[...], kg_ref[...],
                               bd_ref[...], *[t[...] for t in sample.tables])
    _sample_mixer(q.astype(BF16), k, v, u, sinks_ref, wpool_ref, pscale_ref, sample)
    x1 = x + _dot(sample.mix_buf[...], wout_ref[...])
    xn2 = _rms_rows(x1, gffn_ref[...]).astype(BF16)
    acc = x1
    for c in range(d_ff // FF_CHUNK):
        cg = slice(c * FF_CHUNK, (c + 1) * FF_CHUNK)
        cv = slice(d_ff + c * FF_CHUNK, d_ff + (c + 1) * FF_CHUNK)
        taps_g = _sample_taps(_dot(xn2, wup_ref[:, cg]), cg, sample)
        taps_v = _sample_taps(_dot(xn2, wup_ref[:, cv]), cv, sample)
        act = _conv_gate(taps_g, taps_v, cw_ref[:, cg], cw_ref[:, cv], cb_ref[:, cg], cb_ref[:, cv])
        acc = acc + _dot(act, wdown_ref[cg, :])
    sample.y_ref[...] = acc


def _rope_tables(pos, reps=1):
    half = ROT_DIM // 2
    inv = ROPE_THETA ** (-jnp.arange(0, ROT_DIM, 2, dtype=F32) / ROT_DIM)
    ang = pos.astype(F32)[:, None] * inv[None, :]
    cos, sin = jnp.cos(ang), jnp.sin(ang)
    n = pos.shape[0]
    ones = jnp.ones((n, HEAD_DIM - ROT_DIM), F32)
    zeros_h = jnp.zeros((n, half), F32)
    zeros_r = jnp.zeros((n, HEAD_DIM - ROT_DIM), F32)
    c = jnp.concatenate([cos, cos, ones], axis=1)
    sa = jnp.concatenate([-sin, zeros_h, zeros_r], axis=1)
    sb = jnp.concatenate([zeros_h, sin, zeros_r], axis=1)
    tile = lambda t: jnp.tile(t, (reps, LANES // HEAD_DIM))
    return tile(c), tile(sa), tile(sb)


def kernel(x_prompt, x_sample, cache_k, cache_v, state_pool, state_conv, norm_mix, w_in, q_norm,
           k_norm, attn_sinks, w_pool, pool_scale, w_out, norm_ffn, w_up, conv_w, conv_b, w_down):
    depth = w_in.shape[0]
    B, L, D = x_prompt.shape
    S, n_new, _ = x_sample.shape
    past_len = L
    assert L % PROMPT_TILE == 0 and PROMPT_TILE % CHUNK == 0 and PROMPT_TILE >= WINDOW
    assert w_down.shape[1] % FF_CHUNK == 0
    assert all(m.shape[1] % (BF16_SUBLANES * N_PREP) == 0 for m in (w_in, w_up, w_down))
    assert w_out.shape[1] == N_PREP * HEAD_DIM and LANES == N_KV_HEADS * HEAD_DIM

    tabs_p = _rope_tables(jnp.arange(L))
    tabs_s = _rope_tables(past_len + jnp.arange(n_new), reps=S)
    head_id = jnp.arange(MXU_COLS) // HEAD_DIM
    bd = jnp.where(head_id[:, None] == head_id[None, :], 1.0 / HEAD_DIM, 0.0).astype(BF16)
    q_scale = HEAD_DIM ** -0.5
    n_pool = state_pool.shape[2]
    n_conv = state_conv.shape[2]
    past = cache_k.shape[2]

    stacked = dict(
        g_mix=norm_mix[:, None, :], w_in=w_in,
        q_gain=jnp.tile(q_norm * q_scale, (1, N_Q_HEADS))[:, None, :],
        k_gain=jnp.tile(k_norm, (1, N_KV_HEADS))[:, None, :],
        bd=bd, w_pool=w_pool, pool_scale=pool_scale[:, None, :],
        w_out=w_out, g_ffn=norm_ffn[:, None, :], w_up=w_up,
        conv_w=conv_w, conv_b=conv_b[:, None, :], w_down=w_down,
        cache_k=cache_k.reshape(depth, S, past, D_KV), cache_v=cache_v.reshape(depth, S, past, D_KV),
        pool_prev=jnp.pad(state_pool, ((0, 0), (0, 0), (n_new - n_pool, 0), (0, 0))),
        conv_prev=state_conv,
    )

    yp, ys = x_prompt, x_sample
    outs = [[] for _ in range(8)]
    for i in range(depth):
        w = dict(stacked, layer=i, sinks=attn_sinks[i], n_pool=n_pool)
        (yp, k1, v1, u1, h1), (ys, k2, v2, u2, h2) = _layer_call(yp, tabs_p, ys, tabs_s, w)
        outs[0].append(k1.reshape(B, WINDOW, N_KV_HEADS, HEAD_DIM))
        outs[1].append(v1.reshape(B, WINDOW, N_KV_HEADS, HEAD_DIM))
        outs[2].append(u1)
        outs[3].append(h1)
        outs[4].append(k2.reshape(S, WINDOW, N_KV_HEADS, HEAD_DIM))
        outs[5].append(v2.reshape(S, WINDOW, N_KV_HEADS, HEAD_DIM))
        outs[6].append(u2)
        outs[7].append(h2)
    return (yp, ys) + tuple(jnp.stack(o) for o in outs)
```

```python
import functools
from typing import Any, NamedTuple

import jax
import jax.numpy as jnp
from jax import lax
from jax.experimental import pallas as pl
from jax.experimental.pallas import tpu as pltpu

F32 = jnp.float32
BF16 = jnp.bfloat16

CHUNK = 64
HEAD_DIM = 64
N_Q_HEADS = 8
N_KV_HEADS = 2
Q_PER_KV = N_Q_HEADS // N_KV_HEADS
D_ATTN = N_Q_HEADS * HEAD_DIM
D_KV = N_KV_HEADS * HEAD_DIM
WINDOW = 128
ROT_DIM = 16
ROPE_THETA = 500000.0
POOL_WINDOWS = (2, 4, 8, 16)
POOL_GROUP = 128
D_POOL = POOL_GROUP * len(POOL_WINDOWS)
POOL_HALO = 16
CONV_W = 3
CONV_HALO = 8
EPS = 1e-6
NEG_INF = -1e30
NEG_LOG2_E = -1.4426950408889634

LANES = 128
MXU_COLS = 256
BF16_SUBLANES = 16
KV_PAD = CHUNK
VT_ROWS = D_KV + BF16_SUBLANES

PROMPT_TILE = 512
ATT_GROUP = 4
FF_CHUNK = 256
DOWN_GROUP = 3
DOWN_LAG = 2
N_PREP = 16
VMEM_LIMIT_BYTES = 56 * 1024 * 1024


def _dot(a, b):
    return jnp.dot(a, b, preferred_element_type=F32)


def _dot_nt(a, b):
    return lax.dot_general(a, b, (((1,), (1,)), ((), ())), preferred_element_type=F32)


def _rms_rows(x, gain):
    ms = jnp.mean(x * x, axis=-1, keepdims=True)
    return x * lax.rsqrt(ms + EPS) * gain


def _head_rms(t, bd, gain):
    ms = _dot((t * t).astype(BF16), bd)
    return t * lax.rsqrt(ms + EPS) * gain


def _rope(t, cos, sin_a, sin_b):
    return (t * cos + pltpu.roll(t, LANES - ROT_DIM // 2, 1) * sin_a
            + pltpu.roll(t, ROT_DIM // 2, 1) * sin_b)


def _rope_coefficients(tab_ref):
    return [tab_ref[:, j * LANES:(j + 1) * LANES] for j in range(3)]


def _mixer_inputs(x, gmix, w_in, qg, kg, bd, cos, sin_a, sin_b):
    xn = _rms_rows(x, gmix).astype(BF16)
    h = _dot(xn, w_in)
    q_parts = []
    for j in range(D_ATTN // MXU_COLS):
        qb = _head_rms(h[:, j * MXU_COLS:(j + 1) * MXU_COLS], bd,
                       qg[:, j * MXU_COLS:(j + 1) * MXU_COLS])
        for l in range(MXU_COLS // LANES):
            q_parts.append(_rope(qb[:, l * LANES:(l + 1) * LANES], cos, sin_a, sin_b))
    q = jnp.concatenate(q_parts, axis=1)
    k = _head_rms(h[:, D_ATTN:D_ATTN + D_KV], bd[:D_KV, :D_KV], kg)
    k = _rope(k, cos, sin_a, sin_b)
    v = h[:, D_ATTN + D_KV:D_ATTN + 2 * D_KV]
    u = h[:, D_ATTN + 2 * D_KV:]
    return q, k, v, u


def _window_sum(a, w):
    s = 1
    while s < w:
        a = a + pltpu.roll(a, s, 0)
        s *= 2
    return a


def _conv_gate(hg, hv, cw_g, cw_v, cb_g, cb_v):
    cg = cb_g + hg[2] * cw_g[0:1] + hg[1] * cw_g[1:2] + hg[0] * cw_g[2:3]
    cv = cb_v + hv[2] * cw_v[0:1] + hv[1] * cw_v[1:2] + hv[0] * cw_v[2:3]
    return ((cg / (1.0 + jnp.exp2(cg * NEG_LOG2_E))) * cv).astype(BF16)


def _pair_heads(qcols):
    lane = lax.broadcasted_iota(jnp.int32, (1, LANES), 1)
    blocks = [qcols[:, b * LANES:(b + 1) * LANES] for b in range(D_ATTN // LANES)]
    per_block = LANES // HEAD_DIM
    out = []
    for hh in range(Q_PER_KV):
        a = blocks[hh // per_block]
        b = blocks[Q_PER_KV // per_block + hh // per_block]
        if hh % per_block == 0:
            out.append(jnp.where(lane < HEAD_DIM, a, pltpu.roll(b, HEAD_DIM, 1)))
        else:
            out.append(jnp.where(lane < HEAD_DIM, pltpu.roll(a, HEAD_DIM, 1), b))
    return jnp.concatenate(out, axis=1)


def _stage_weights(s, win_c, wout_c, wup_c, wdown_c, w_in_s, w_out_s, w_up_s, w_down_s):
    r_in = win_c.shape[0]
    r0 = pl.multiple_of(s * r_in, r_in)
    c = win_c[...]
    w_in_s[pl.ds(r0, r_in), :] = jnp.concatenate(
        [_pair_heads(c[:, :D_ATTN]), c[:, D_ATTN:]], axis=1).astype(BF16)
    w_up_s[pl.ds(r0, r_in), :] = wup_c[...].astype(BF16)
    r_dn = wdown_c.shape[0]
    w_down_s[pl.ds(pl.multiple_of(s * r_dn, r_dn), r_dn), :] = wdown_c[...].astype(BF16)
    dst = jnp.where(s < N_Q_HEADS, (s % Q_PER_KV) * N_KV_HEADS + s // Q_PER_KV, s)
    w_out_s[pl.ds(pl.multiple_of(dst * HEAD_DIM, HEAD_DIM), HEAD_DIM), :] = wout_c[...].astype(BF16)


def _prompt_kernel(sinks_ref, x_ref, tab_ref,
                   gmix_ref, win_c, qg_ref, kg_ref, bd_ref, wpool_ref, pscale_ref, wout_c,
                   gffn_ref, wup_c, cw_ref, cb_ref, wdown_c,
                   xs_ref, tab_s_ref, ck_ref, cv_ref, pprev_ref, cprev_ref,
                   y_ref, klast_ref, vlast_ref, ulast_ref, hlast_ref,
                   ys_ref, kout_ref, vout_ref, uout_ref, hout_ref,
                   w_in_s, w_out_s, w_up_s, w_down_s,
                   kfull, vfull, vt, ufull, hprev, mix_buf, ufull_s, mix_s,
                   *, layer, n_tiles, n_total):
    s = pl.program_id(0)
    sinks_ref = sinks_ref.at[layer]
    gmix_ref, qg_ref, kg_ref, pscale_ref, gffn_ref, cb_ref = (
        r.at[layer:layer + 1] for r in (gmix_ref, qg_ref, kg_ref, pscale_ref, gffn_ref, cb_ref))

    @pl.when(s < N_PREP)
    def _():
        _stage_weights(s, win_c, wout_c, wup_c, wdown_c, w_in_s, w_out_s, w_up_s, w_down_s)

    @pl.when((s >= N_PREP) & (s < N_PREP + n_total))
    def _():
        _prompt_body((s - N_PREP) % n_tiles, sinks_ref, x_ref, tab_ref,
                     gmix_ref, w_in_s, qg_ref, kg_ref, bd_ref, wpool_ref, pscale_ref, w_out_s,
                     gffn_ref, w_up_s, cw_ref, cb_ref, w_down_s,
                     y_ref, klast_ref, vlast_ref, ulast_ref, hlast_ref,
                     kfull, vfull, vt, ufull, hprev, mix_buf)

    @pl.when(s == N_PREP + n_total)
    def _():
        _sample_step(sinks_ref, gmix_ref, w_in_s, qg_ref, kg_ref, bd_ref, wpool_ref, pscale_ref,
                     w_out_s, gffn_ref, w_up_s, cw_ref, cb_ref, w_down_s,
                     _SampleRefs(xs_ref, tab_s_ref, ck_ref, cv_ref, pprev_ref,
                                 cprev_ref, ys_ref, kout_ref, vout_ref, uout_ref, hout_ref,
                                 ufull_s, mix_s))


def _prompt_body(i, sinks_ref, x_ref, tab_ref,
                 gmix_ref, win_ref, qg_ref, kg_ref, bd_ref, wpool_ref, pscale_ref, wout_ref,
                 gffn_ref, wup_ref, cw_ref, cb_ref, wdown_ref,
                 y_ref, klast_ref, vlast_ref, ulast_ref, hlast_ref,
                 kfull, vfull, vt, ufull, hprev, mix_buf):
    T = x_ref.shape[0]
    d_ff = wdown_ref.shape[0]
    n_chunks = T // CHUNK

    @pl.when(i == 0)
    def _():
        kfull[0:WINDOW, :] = jnp.zeros((WINDOW, D_KV), BF16)
        kfull[WINDOW + T:, :] = jnp.zeros((KV_PAD, D_KV), BF16)
        vfull[0:WINDOW, :] = jnp.zeros((WINDOW, D_KV), BF16)
        vfull[WINDOW + T:, :] = jnp.zeros((KV_PAD, D_KV), BF16)
        for c in range(2):
            vt[c, D_KV:, :] = jnp.ones((VT_ROWS - D_KV, vt.shape[2]), BF16)
        ufull[0:POOL_HALO, :] = jnp.zeros((POOL_HALO, D_POOL), F32)
        hprev[...] = jnp.zeros(hprev.shape, F32)

    x = x_ref[...]
    q, k, v, u = _mixer_inputs(x, gmix_ref[...], win_ref[...], qg_ref[...], kg_ref[...],
                               bd_ref[...], *_rope_coefficients(tab_ref))
    klast_ref[...] = k[T - WINDOW:, :]
    vlast_ref[...] = v[T - WINDOW:, :]
    ulast_ref[...] = u[T - ulast_ref.shape[0]:, :]
    kfull[WINDOW:WINDOW + T, :] = k.astype(BF16)
    vfull[WINDOW:WINDOW + T, :] = v.astype(BF16)
    ufull[POOL_HALO:POOL_HALO + T, :] = u

    for c in range(2):
        vrows = vfull[c * CHUNK:c * CHUNK + T + WINDOW, :]
        vt[c, 0:D_KV, :] = vrows.astype(F32).T.astype(BF16)
    band = WINDOW + CHUNK
    n_q = N_Q_HEADS * CHUNK
    lane = lax.broadcasted_iota(jnp.int32, (1, D_ATTN), 1)
    head_of_col = lax.broadcasted_iota(jnp.int32, (1, n_q), 1) // CHUNK
    key_row = lax.broadcasted_iota(jnp.int32, (band, 1), 0)
    q_of_group = []
    for g in range(N_KV_HEADS):
        in_g = ((lane // HEAD_DIM) % N_KV_HEADS == g).astype(F32)
        q_of_group.append((q * in_g).astype(BF16))
    sink_row = jnp.full((1, n_q), sinks_ref[0], F32)
    for h8 in range(1, N_Q_HEADS):
        sink_row = jnp.where(head_of_col == h8, sinks_ref[h8], sink_row)
    half = Q_PER_KV * CHUNK
    for j0 in range(0, n_chunks, ATT_GROUP):
        chunk_ids = range(j0, min(j0 + ATT_GROUP, n_chunks))
        scores = []
        for j in chunk_ids:
            r0 = j * CHUNK
            qs = jnp.concatenate(
                [q_of_group[g][r0:r0 + CHUNK, hh * LANES:(hh + 1) * LANES]
                 for g in range(N_KV_HEADS) for hh in range(Q_PER_KV)], axis=0)
            scores.append(_dot_nt(kfull[r0:r0 + band, :], qs))
        probs = []
        for j, s in zip(chunk_ids, scores):
            r0 = j * CHUNK
            if r0 < WINDOW:
                first_valid = jnp.where(i == 0, WINDOW - r0, 0)
                s = jnp.where(key_row >= first_valid, s, NEG_INF)
            m = jnp.maximum(jnp.max(s, axis=0, keepdims=True), sink_row)
            probs.append((jnp.exp(s - m).astype(BF16), jnp.exp(sink_row - m)))
        outs_t = []
        for j, (e, sink_e) in zip(chunk_ids, probs):
            vtb = vt[j % 2, :, (j // 2) * LANES:(j // 2) * LANES + band]
            oa = _dot(vtb, e)
            inv_den = 1.0 / (oa[D_KV:D_KV + 1] + sink_e)
            outs_t.append(jnp.concatenate(
                [oa[g * HEAD_DIM:(g + 1) * HEAD_DIM, g * half:(g + 1) * half]
                 * inv_den[:, g * half:(g + 1) * half] for g in range(N_KV_HEADS)],
                axis=0))
        for j, ot in zip(chunk_ids, outs_t):
            r0 = j * CHUNK
            o = ot.T
            for hh in range(Q_PER_KV):
                mix_buf[r0:r0 + CHUNK, hh * LANES:(hh + 1) * LANES] = (
                    o[hh * CHUNK:(hh + 1) * CHUNK].astype(BF16))

    uf = ufull[...]
    pos = i * T + lax.broadcasted_iota(jnp.int32, (T, 1), 0)
    for gi, w in enumerate(POOL_WINDOWS):
        sl = slice(gi * POOL_GROUP, (gi + 1) * POOL_GROUP)
        tsum = _window_sum(uf[:, sl], w)[POOL_HALO:]
        cnt = jnp.minimum(pos + 1, w).astype(F32)
        d = (tsum / cnt - u[:, sl]).astype(BF16)
        pool = _dot(d, wpool_ref[gi].astype(BF16)) * pscale_ref[:, sl]
        mix_buf[:, D_ATTN + gi * POOL_GROUP:D_ATTN + (gi + 1) * POOL_GROUP] = pool.astype(BF16)

    kfull[0:WINDOW, :] = kfull[T:T + WINDOW, :]
    vfull[0:WINDOW, :] = vfull[T:T + WINDOW, :]
    ufull[0:POOL_HALO, :] = u[T - POOL_HALO:, :]

    x1 = x + _dot(mix_buf[...], wout_ref[...])

    xn2 = _rms_rows(x1, gffn_ref[...]).astype(BF16)
    F = FF_CHUNK
    y, k_done, acts = x1, 0, []
    row_in_group = lax.broadcasted_iota(jnp.int32, (1, CONV_HALO, 1), 1)
    for c in range(d_ff // F):
        cols = (slice(c * F, (c + 1) * F), slice(d_ff + c * F, d_ff + (c + 1) * F))
        taps = []
        for half, cs in enumerate(cols):
            h = _dot(xn2, wup_ref[:, cs])
            groups = jnp.concatenate([hprev[:, cs], h], axis=0).reshape(T // CONV_HALO + 1, CONV_HALO, F)
            shifted = []
            for sh in (1, 2):
                r = pltpu.roll(groups, sh, 1)
                shifted.append(jnp.where(row_in_group < sh, r[:-1], r[1:]).reshape(T, F))
            hlast_ref[:, cs] = h[T - hlast_ref.shape[0]:, :]
            hprev[:, cs] = h[T - CONV_HALO:, :]
            taps.append((h, shifted[0], shifted[1]))
        acts.append(_conv_gate(taps[0], taps[1], cw_ref[:, cols[0]], cw_ref[:, cols[1]],
                               cb_ref[:, cols[0]], cb_ref[:, cols[1]]))
        c_done = c - DOWN_LAG
        if c_done >= 0 and (c_done + 1) % DOWN_GROUP == 0:
            ks = slice((c_done + 1 - DOWN_GROUP) * F, (c_done + 1) * F)
            group = jnp.concatenate(acts[c_done + 1 - DOWN_GROUP:c_done + 1], axis=1)
            y = y + _dot(group, wdown_ref[ks, :])
            k_done = (c_done + 1) * F
    y_ref[...] = y + _dot(jnp.concatenate(acts[k_done // F:], axis=1), wdown_ref[k_done:, :])


def _const_spec(shape):
    nd = len(shape)
    return pl.BlockSpec(shape, lambda *_: (0,) * nd)


def _chunk_spec(stacked, layer):
    rows = stacked.shape[1] // N_PREP
    return pl.BlockSpec((None, rows, stacked.shape[2]),
                        lambda s: (layer, jnp.minimum(s, N_PREP - 1), 0))


def _layer_spec(stacked, layer, resident=False):
    nd = stacked.ndim - 1
    return pl.BlockSpec((None,) + stacked.shape[1:], lambda *_: (layer,) + (0,) * nd,
                        pipeline_mode=pl.Buffered(1) if resident else None)


def _weight_specs(w):
    layer = w["layer"]
    return [
        _const_spec(w["g_mix"].shape), _chunk_spec(w["w_in"], layer),
        _const_spec(w["q_gain"].shape), _const_spec(w["k_gain"].shape), _const_spec(w["bd"].shape),
        _layer_spec(w["w_pool"], layer), _const_spec(w["pool_scale"].shape), _chunk_spec(w["w_out"], layer),
        _const_spec(w["g_ffn"].shape), _chunk_spec(w["w_up"], layer),
        _layer_spec(w["conv_w"], layer), _const_spec(w["conv_b"].shape), _chunk_spec(w["w_down"], layer),
    ]


def _weight_args(w):
    return (w["g_mix"], w["w_in"], w["q_gain"], w["k_gain"], w["bd"], w["w_pool"],
            w["pool_scale"], w["w_out"], w["g_ffn"], w["w_up"], w["conv_w"], w["conv_b"], w["w_down"])


def _weight_scratch(w):
    return [pltpu.VMEM(w[name].shape[1:], BF16) for name in ("w_in", "w_out", "w_up", "w_down")]


def _resident_spec(shape):
    nd = len(shape)
    return pl.BlockSpec(shape, lambda *_: (0,) * nd, pipeline_mode=pl.Buffered(1))


def _layer_call(x, tabs, xs, tabs_s, w):
    B, L, D = x.shape
    S, n_new, _ = xs.shape
    M = S * n_new
    T = PROMPT_TILE
    d_ff = w["w_down"].shape[1]
    n_tiles = L // T
    n_total = B * n_tiles
    layer = w["layer"]
    n_pool, n_conv = w["n_pool"], w["conv_prev"].shape[2]
    tile_of = lambda s: jnp.clip(s - N_PREP, 0, n_total - 1)
    tile_spec = pl.BlockSpec((None, T, D), lambda s: (tile_of(s) // n_tiles, tile_of(s) % n_tiles, 0))
    tab_spec = pl.BlockSpec((T, tabs.shape[1]), lambda s: (tile_of(s) % n_tiles, 0))

    def last_spec(rows, cols):
        return pl.BlockSpec((None, rows, cols), lambda s: (tile_of(s) // n_tiles, 0, 0))

    sample_acts = (xs.reshape(M, D), tabs_s)
    sample_state = (w["cache_k"], w["cache_v"], w["pool_prev"], w["conv_prev"])
    in_specs = ([pl.BlockSpec(memory_space=pltpu.SMEM), tile_spec, tab_spec]
                + _weight_specs(w) + [_resident_spec(a.shape) for a in sample_acts]
                + [_layer_spec(a, layer, resident=True) for a in sample_state])
    sample_out = (
        jax.ShapeDtypeStruct((M, D), F32),
        jax.ShapeDtypeStruct((S, WINDOW, D_KV), F32),
        jax.ShapeDtypeStruct((S, WINDOW, D_KV), F32),
        jax.ShapeDtypeStruct((S, n_pool, D_POOL), F32),
        jax.ShapeDtypeStruct((S, n_conv, 2 * d_ff), F32),
    )
    out_shape = (
        jax.ShapeDtypeStruct((B, L, D), F32),
        jax.ShapeDtypeStruct((B, WINDOW, D_KV), F32),
        jax.ShapeDtypeStruct((B, WINDOW, D_KV), F32),
        jax.ShapeDtypeStruct((B, n_pool, D_POOL), F32),
        jax.ShapeDtypeStruct((B, n_conv, 2 * d_ff), F32),
    ) + sample_out
    out_specs = (tile_spec, last_spec(WINDOW, D_KV), last_spec(WINDOW, D_KV),
                 last_spec(n_pool, D_POOL), last_spec(n_conv, 2 * d_ff)
                 ) + tuple(_resident_spec(o.shape) for o in sample_out)
    scratch = _weight_scratch(w) + [
        pltpu.VMEM((WINDOW + T + KV_PAD, D_KV), BF16),
        pltpu.VMEM((WINDOW + T + KV_PAD, D_KV), BF16),
        pltpu.VMEM((2, VT_ROWS, T + WINDOW), BF16),
        pltpu.VMEM((POOL_HALO + T, D_POOL), F32),
        pltpu.VMEM((CONV_HALO, 2 * d_ff), F32),
        pltpu.VMEM((T, D_ATTN + D_POOL), BF16),
        pltpu.VMEM((2 * M, D_POOL), F32),
        pltpu.VMEM((M, D_ATTN + D_POOL), BF16),
    ]
    outs = pl.pallas_call(
        functools.partial(_prompt_kernel, layer=layer, n_tiles=n_tiles, n_total=n_total),
        out_shape=out_shape,
        grid=(N_PREP + n_total + 1,),
        in_specs=in_specs,
        out_specs=out_specs,
        scratch_shapes=scratch,
        name="layer",
        compiler_params=pltpu.CompilerParams(
            dimension_semantics=("arbitrary",),
            vmem_limit_bytes=VMEM_LIMIT_BYTES),
    )(w["sinks"], x, tabs, *_weight_args(w), *sample_acts, *sample_state)
    ys, ko, vo, uo, ho = outs[5:]
    return outs[:5], (ys.reshape(S, n_new, D), ko, vo, uo, ho)


class _SampleRefs(NamedTuple):
    x_ref: Any
    tab_ref: Any
    ck_ref: Any
    cv_ref: Any
    pprev_ref: Any
    cprev_ref: Any
    y_ref: Any
    kout_ref: Any
    vout_ref: Any
    uout_ref: Any
    hout_ref: Any
    ufull: Any
    mix_buf: Any


def _sample_mixer(q, k, v, u, sinks_ref, wpool_ref, pscale_ref, sample):
    ck_ref, cv_ref, mix_buf, ufull = sample.ck_ref, sample.cv_ref, sample.mix_buf, sample.ufull
    M = q.shape[0]
    n_streams, past = ck_ref.shape[0], ck_ref.shape[1]
    n_new = M // n_streams
    n_prev = sample.pprev_ref.shape[1]
    uout_ref = sample.uout_ref
    uout_ref[...] = u.reshape(n_streams, n_new, u.shape[1])[:, n_new - uout_ref.shape[1]:, :]

    for s_ in range(n_streams):
        rows = slice(s_ * n_new, (s_ + 1) * n_new)
        keys = jnp.concatenate([ck_ref[s_], k[rows]], axis=0)
        vals = jnp.concatenate([cv_ref[s_], v[rows]], axis=0)
        sample.kout_ref[s_] = keys[past + n_new - WINDOW:, :]
        sample.vout_ref[s_] = vals[past + n_new - WINDOW:, :]
        keys16 = keys.astype(BF16)
        vals16 = vals.astype(BF16)
        outs = []
        for g in range(N_KV_HEADS):
            sink_col = jnp.concatenate(
                [jnp.full((n_new, 1), sinks_ref[g * Q_PER_KV + hh], F32) for hh in range(Q_PER_KV)],
                axis=0)
            qs = jnp.concatenate(
                [q[rows, hh * LANES + g * HEAD_DIM:hh * LANES + (g + 1) * HEAD_DIM]
                 for hh in range(Q_PER_KV)], axis=0)
            sc = _dot_nt(qs, keys16[:, g * HEAD_DIM:(g + 1) * HEAD_DIM])
            m = jnp.maximum(jnp.max(sc, axis=-1, keepdims=True), sink_col)
            e = jnp.exp(sc - m)
            den = jnp.sum(e, axis=-1, keepdims=True) + jnp.exp(sink_col - m)
            outs.append(_dot(e.astype(BF16), vals16[:, g * HEAD_DIM:(g + 1) * HEAD_DIM]) * (1.0 / den))
        for hh in range(Q_PER_KV):
            pair = jnp.concatenate([o[hh * n_new:(hh + 1) * n_new] for o in outs], axis=1)
            mix_buf[rows, hh * LANES:(hh + 1) * LANES] = pair.astype(BF16)
        base = s_ * 2 * n_new
        ufull[base:base + n_new - n_prev, :] = jnp.zeros((n_new - n_prev, ufull.shape[1]), F32)
        ufull[base + n_new - n_prev:base + n_new, :] = sample.pprev_ref[s_]
        ufull[base + n_new:base + 2 * n_new, :] = u[rows]

    uf = ufull[...]
    for gi, w in enumerate(POOL_WINDOWS):
        sl = slice(gi * POOL_GROUP, (gi + 1) * POOL_GROUP)
        ws = _window_sum(uf[:, sl], w)
        tsum = jnp.concatenate(
            [ws[s_ * 2 * n_new + n_new:(s_ + 1) * 2 * n_new] for s_ in range(n_streams)], axis=0)
        d = (tsum / float(w) - u[:, sl]).astype(BF16)
        pool = _dot(d, wpool_ref[gi].astype(BF16)) * pscale_ref[:, sl]
        mix_buf[:, D_ATTN + gi * POOL_GROUP:D_ATTN + (gi + 1) * POOL_GROUP] = pool.astype(BF16)


def _sample_taps(h, cs, sample):
    cprev_ref, hout_ref = sample.cprev_ref, sample.hout_ref
    M, F = h.shape
    n_streams, n_conv = cprev_ref.shape[0], cprev_ref.shape[1]
    n_new = M // n_streams
    t_in_stream = lax.broadcasted_iota(jnp.int32, (M, 1), 0) % n_new
    hout_ref[:, :, cs] = h.reshape(n_streams, n_new, F)[:, n_new - hout_ref.shape[1]:, :]
    prev = [jnp.broadcast_to(cprev_ref[:, j:j + 1, cs], (n_streams, n_new, F)).reshape(M, F)
            for j in range(n_conv)]
    return (h, jnp.where(t_in_stream >= 1, pltpu.roll(h, 1, 0), prev[1]),
            jnp.where(t_in_stream >= 2, pltpu.roll(h, 2, 0),
                      jnp.where(t_in_stream == 0, prev[0], prev[1])))


def _sample_step(sinks_ref, gmix_ref, win_ref, qg_ref, kg_ref, bd_ref, wpool_ref, pscale_ref,
                 wout_ref, gffn_ref, wup_ref, cw_ref, cb_ref, wdown_ref, sample):
    d_ff = wdown_ref.shape[0]
    x = sample.x_ref[...]
    q, k, v, u = _mixer_inputs(x, gmix_ref[...], win_ref[...], qg_ref[...], kg_ref[...],
                               bd_ref[...], *_rope_coefficients(sample.tab_ref))
    _sample_mixer(q.astype(BF16), k, v, u, sinks_ref, wpool_ref, pscale_ref, sample)
    x1 = x + _dot(sample.mix_buf[...], wout_ref[...])
    xn2 = _rms_rows(x1, gffn_ref[...]).astype(BF16)
    acc = x1
    for c in range(d_ff // FF_CHUNK):
        cg = slice(c * FF_CHUNK, (c + 1) * FF_CHUNK)
        cv = slice(d_ff + c * FF_CHUNK, d_ff + (c + 1) * FF_CHUNK)
        taps_g = _sample_taps(_dot(xn2, wup_ref[:, cg]), cg, sample)
        taps_v = _sample_taps(_dot(xn2, wup_ref[:, cv]), cv, sample)
        act = _conv_gate(taps_g, taps_v, cw_ref[:, cg], cw_ref[:, cv], cb_ref[:, cg], cb_ref[:, cv])
        acc = acc + _dot(act, wdown_ref[cg, :])
    sample.y_ref[...] = acc


def _rope_tables(pos, reps=1):
    half = ROT_DIM // 2
    inv = ROPE_THETA ** (-jnp.arange(0, ROT_DIM, 2, dtype=F32) / ROT_DIM)
    ang = pos.astype(F32)[:, None] * inv[None, :]
    cos, sin = jnp.cos(ang), jnp.sin(ang)
    n = pos.shape[0]
    ones = jnp.ones((n, HEAD_DIM - ROT_DIM), F32)
    zeros_h = jnp.zeros((n, half), F32)
    zeros_r = jnp.zeros((n, HEAD_DIM - ROT_DIM), F32)
    c = jnp.concatenate([cos, cos, ones], axis=1)
    sa = jnp.concatenate([-sin, zeros_h, zeros_r], axis=1)
    sb = jnp.concatenate([zeros_h, sin, zeros_r], axis=1)
    per_head = LANES // HEAD_DIM
    return jnp.tile(jnp.concatenate([c] * per_head + [sa] * per_head + [sb] * per_head, axis=1), (reps, 1))


def kernel(x_prompt, x_sample, cache_k, cache_v, state_pool, state_conv, norm_mix, w_in, q_norm,
           k_norm, attn_sinks, w_pool, pool_scale, w_out, norm_ffn, w_up, conv_w, conv_b, w_down):
    depth = w_in.shape[0]
    B, L, D = x_prompt.shape
    S, n_new, _ = x_sample.shape
    past_len = L
    assert L % PROMPT_TILE == 0 and PROMPT_TILE % CHUNK == 0 and PROMPT_TILE >= WINDOW
    assert w_down.shape[1] % FF_CHUNK == 0
    assert all(m.shape[1] % (BF16_SUBLANES * N_PREP) == 0 for m in (w_in, w_up, w_down))
    assert w_out.shape[1] == N_PREP * HEAD_DIM and LANES == N_KV_HEADS * HEAD_DIM

    tabs_p = _rope_tables(jnp.arange(L))
    tabs_s = _rope_tables(past_len + jnp.arange(n_new), reps=S)
    head_id = jnp.arange(MXU_COLS) // HEAD_DIM
    bd = jnp.where(head_id[:, None] == head_id[None, :], 1.0 / HEAD_DIM, 0.0).astype(BF16)
    q_scale = HEAD_DIM ** -0.5
    n_pool = state_pool.shape[2]
    assert max(POOL_WINDOWS) - 1 <= n_pool <= n_new
    n_conv = state_conv.shape[2]
    past = cache_k.shape[2]

    stacked = dict(
        g_mix=norm_mix, w_in=w_in,
        q_gain=jnp.tile(q_norm * q_scale, (1, N_Q_HEADS)),
        k_gain=jnp.tile(k_norm, (1, N_KV_HEADS)),
        bd=bd, w_pool=w_pool, pool_scale=pool_scale,
        w_out=w_out, g_ffn=norm_ffn, w_up=w_up,
        conv_w=conv_w, conv_b=conv_b, w_down=w_down,
        cache_k=cache_k.reshape(depth, S, past, D_KV), cache_v=cache_v.reshape(depth, S, past, D_KV),
        pool_prev=state_pool, conv_prev=state_conv,
    )

    yp, ys = x_prompt, x_sample
    outs = [[] for _ in range(8)]
    for i in range(depth):
        w = dict(stacked, layer=i, sinks=attn_sinks, n_pool=n_pool)
        (yp, k1, v1, u1, h1), (ys, k2, v2, u2, h2) = _layer_call(yp, tabs_p, ys, tabs_s, w)
        outs[0].append(k1.reshape(B, WINDOW, N_KV_HEADS, HEAD_DIM))
        outs[1].append(v1.reshape(B, WINDOW, N_KV_HEADS, HEAD_DIM))
        outs[2].append(u1)
        outs[3].append(h1)
        outs[4].append(k2.reshape(S, WINDOW, N_KV_HEADS, HEAD_DIM))
        outs[5].append(v2.reshape(S, WINDOW, N_KV_HEADS, HEAD_DIM))
        outs[6].append(u2)
        outs[7].append(h2)
    return (yp, ys) + tuple(jnp.stack(o) for o in outs)
```

```python
import functools
from typing import Any, NamedTuple

import jax
import jax.numpy as jnp
from jax import lax
from jax.experimental import pallas as pl
from jax.experimental.pallas import tpu as pltpu

F32 = jnp.float32
BF16 = jnp.bfloat16

CHUNK = 64
HEAD_DIM = 64
N_Q_HEADS = 8
N_KV_HEADS = 2
Q_PER_KV = N_Q_HEADS // N_KV_HEADS
D_ATTN = N_Q_HEADS * HEAD_DIM
D_KV = N_KV_HEADS * HEAD_DIM
WINDOW = 128
ROT_DIM = 16
ROPE_THETA = 500000.0
POOL_WINDOWS = (2, 4, 8, 16)
POOL_GROUP = 128
D_POOL = POOL_GROUP * len(POOL_WINDOWS)
POOL_HALO = 16
CONV_W = 3
CONV_HALO = 8
EPS = 1e-6
NEG_INF = -1e30
NEG_LOG2_E = -1.4426950408889634

LANES = 128
MXU_COLS = 256
BF16_SUBLANES = 16
KV_PAD = CHUNK
VT_ROWS = D_KV + BF16_SUBLANES

PROMPT_TILE = 512
ATT_GROUP = 4
FF_CHUNK = 256
DOWN_GROUP = 3
DOWN_LAG = 2
N_PREP = 16
VMEM_LIMIT_BYTES = 56 * 1024 * 1024


def _dot(a, b):
    return jnp.dot(a, b, preferred_element_type=F32)


def _dot_nt(a, b):
    return lax.dot_general(a, b, (((1,), (1,)), ((), ())), preferred_element_type=F32)


def _rms_rows(x, gain):
    ms = jnp.mean(x * x, axis=-1, keepdims=True)
    return x * lax.rsqrt(ms + EPS) * gain


def _head_rms(t, bd, gain):
    ms = _dot((t * t).astype(BF16), bd)
    return t * lax.rsqrt(ms + EPS) * gain


def _rope(t, cos, sin_a, sin_b):
    return (t * cos + pltpu.roll(t, LANES - ROT_DIM // 2, 1) * sin_a
            + pltpu.roll(t, ROT_DIM // 2, 1) * sin_b)


def _rope_coefficients(tab_ref):
    return [tab_ref[:, j * LANES:(j + 1) * LANES] for j in range(3)]


def _mixer_inputs(x, gmix, w_in, qg, kg, bd, cos, sin_a, sin_b):
    xn = _rms_rows(x, gmix).astype(BF16)
    h = _dot(xn, w_in)
    q_parts = []
    for j in range(D_ATTN // MXU_COLS):
        qb = _head_rms(h[:, j * MXU_COLS:(j + 1) * MXU_COLS], bd,
                       qg[:, j * MXU_COLS:(j + 1) * MXU_COLS])
        for l in range(MXU_COLS // LANES):
            q_parts.append(_rope(qb[:, l * LANES:(l + 1) * LANES], cos, sin_a, sin_b))
    q = jnp.concatenate(q_parts, axis=1)
    k = _head_rms(h[:, D_ATTN:D_ATTN + D_KV], bd[:D_KV, :D_KV], kg)
    k = _rope(k, cos, sin_a, sin_b)
    v = h[:, D_ATTN + D_KV:D_ATTN + 2 * D_KV]
    u = h[:, D_ATTN + 2 * D_KV:]
    return q, k, v, u


def _window_sum(a, w):
    s = 1
    while s < w:
        a = a + pltpu.roll(a, s, 0)
        s *= 2
    return a


def _conv_gate(hg, hv, cw_g, cw_v, cb_g, cb_v):
    cg = cb_g + hg[2] * cw_g[0:1] + hg[1] * cw_g[1:2] + hg[0] * cw_g[2:3]
    cv = cb_v + hv[2] * cw_v[0:1] + hv[1] * cw_v[1:2] + hv[0] * cw_v[2:3]
    return ((cg / (1.0 + jnp.exp2(cg * NEG_LOG2_E))) * cv).astype(BF16)


def _pair_heads(qcols):
    lane = lax.broadcasted_iota(jnp.int32, (1, LANES), 1)
    blocks = [qcols[:, b * LANES:(b + 1) * LANES] for b in range(D_ATTN // LANES)]
    per_block = LANES // HEAD_DIM
    out = []
    for hh in range(Q_PER_KV):
        a = blocks[hh // per_block]
        b = blocks[Q_PER_KV // per_block + hh // per_block]
        if hh % per_block == 0:
            out.append(jnp.where(lane < HEAD_DIM, a, pltpu.roll(b, HEAD_DIM, 1)))
        else:
            out.append(jnp.where(lane < HEAD_DIM, pltpu.roll(a, HEAD_DIM, 1), b))
    return jnp.concatenate(out, axis=1)


def _stage_weights(s, win_c, wout_c, wup_c, wdown_c, w_in_s, w_out_s, w_up_s, w_down_s):
    r_in = win_c.shape[0]
    r0 = pl.multiple_of(s * r_in, r_in)
    c = win_c[...]
    w_in_s[pl.ds(r0, r_in), :] = jnp.concatenate(
        [_pair_heads(c[:, :D_ATTN]), c[:, D_ATTN:]], axis=1).astype(BF16)
    w_up_s[pl.ds(r0, r_in), :] = wup_c[...].astype(BF16)
    r_dn = wdown_c.shape[0]
    w_down_s[pl.ds(pl.multiple_of(s * r_dn, r_dn), r_dn), :] = wdown_c[...].astype(BF16)
    dst = jnp.where(s < N_Q_HEADS, (s % Q_PER_KV) * N_KV_HEADS + s // Q_PER_KV, s)
    w_out_s[pl.ds(pl.multiple_of(dst * HEAD_DIM, HEAD_DIM), HEAD_DIM), :] = wout_c[...].astype(BF16)


def _prompt_kernel(sinks_ref, x_ref, tab_ref,
                   gmix_ref, win_c, qg_ref, kg_ref, bd_ref, wpool_ref, pscale_ref, wout_c,
                   gffn_ref, wup_c, cw_ref, cb_ref, wdown_c,
                   xs_ref, tab_s_ref, ck_ref, cv_ref, pprev_ref, cprev_ref,
                   y_ref, klast_ref, vlast_ref, ulast_ref, hlast_ref,
                   ys_ref, kout_ref, vout_ref, uout_ref, hout_ref,
                   w_in_s, w_out_s, w_up_s, w_down_s,
                   kfull, vfull, vt, ufull, hprev, mix_buf, ufull_s, mix_s,
                   *, layer, n_tiles, n_total):
    s = pl.program_id(0)
    sinks_ref = sinks_ref.at[layer]
    gmix_ref, qg_ref, kg_ref, pscale_ref, gffn_ref, cb_ref = (
        r.at[layer:layer + 1] for r in (gmix_ref, qg_ref, kg_ref, pscale_ref, gffn_ref, cb_ref))

    @pl.when(s < N_PREP)
    def _():
        _stage_weights(s, win_c, wout_c, wup_c, wdown_c, w_in_s, w_out_s, w_up_s, w_down_s)

    @pl.when((s >= N_PREP) & (s < N_PREP + n_total))
    def _():
        _prompt_body((s - N_PREP) % n_tiles, sinks_ref, x_ref, tab_ref,
                     gmix_ref, w_in_s, qg_ref, kg_ref, bd_ref, wpool_ref, pscale_ref, w_out_s,
                     gffn_ref, w_up_s, cw_ref, cb_ref, w_down_s,
                     y_ref, klast_ref, vlast_ref, ulast_ref, hlast_ref,
                     kfull, vfull, vt, ufull, hprev, mix_buf)

    @pl.when(s == N_PREP + n_total)
    def _():
        _sample_step(sinks_ref, gmix_ref, w_in_s, qg_ref, kg_ref, bd_ref, wpool_ref, pscale_ref,
                     w_out_s, gffn_ref, w_up_s, cw_ref, cb_ref, w_down_s,
                     _SampleRefs(xs_ref, tab_s_ref, ck_ref, cv_ref, pprev_ref,
                                 cprev_ref, ys_ref, kout_ref, vout_ref, uout_ref, hout_ref,
                                 ufull_s, mix_s))


def _prompt_body(i, sinks_ref, x_ref, tab_ref,
                 gmix_ref, win_ref, qg_ref, kg_ref, bd_ref, wpool_ref, pscale_ref, wout_ref,
                 gffn_ref, wup_ref, cw_ref, cb_ref, wdown_ref,
                 y_ref, klast_ref, vlast_ref, ulast_ref, hlast_ref,
                 kfull, vfull, vt, ufull, hprev, mix_buf):
    T = x_ref.shape[0]
    d_ff = wdown_ref.shape[0]
    n_chunks = T // CHUNK

    @pl.when(i == 0)
    def _():
        kfull[0:WINDOW, :] = jnp.zeros((WINDOW, D_KV), BF16)
        kfull[WINDOW + T:, :] = jnp.zeros((KV_PAD, D_KV), BF16)
        vfull[0:WINDOW, :] = jnp.zeros((WINDOW, D_KV), BF16)
        vfull[WINDOW + T:, :] = jnp.zeros((KV_PAD, D_KV), BF16)
        for c in range(2):
            vt[c, D_KV:, :] = jnp.ones((VT_ROWS - D_KV, vt.shape[2]), BF16)
        ufull[0:POOL_HALO, :] = jnp.zeros((POOL_HALO, D_POOL), F32)
        hprev[...] = jnp.zeros(hprev.shape, F32)

    x = x_ref[...]
    q, k, v, u = _mixer_inputs(x, gmix_ref[...], win_ref[...], qg_ref[...], kg_ref[...],
                               bd_ref[...], *_rope_coefficients(tab_ref))
    klast_ref[...] = k[T - WINDOW:, :]
    vlast_ref[...] = v[T - WINDOW:, :]
    ulast_ref[...] = u[T - ulast_ref.shape[0]:, :]
    kfull[WINDOW:WINDOW + T, :] = k.astype(BF16)
    vfull[WINDOW:WINDOW + T, :] = v.astype(BF16)
    ufull[POOL_HALO:POOL_HALO + T, :] = u

    for c in range(2):
        vrows = vfull[c * CHUNK:c * CHUNK + T + WINDOW, :]
        vt[c, 0:D_KV, :] = vrows.astype(F32).T.astype(BF16)
    band = WINDOW + CHUNK
    n_q = N_Q_HEADS * CHUNK
    lane = lax.broadcasted_iota(jnp.int32, (1, D_ATTN), 1)
    head_of_col = lax.broadcasted_iota(jnp.int32, (1, n_q), 1) // CHUNK
    key_row = lax.broadcasted_iota(jnp.int32, (band, 1), 0)
    q_of_group = []
    for g in range(N_KV_HEADS):
        in_g = ((lane // HEAD_DIM) % N_KV_HEADS == g).astype(F32)
        q_of_group.append((q * in_g).astype(BF16))
    sink_row = jnp.full((1, n_q), sinks_ref[0], F32)
    for h8 in range(1, N_Q_HEADS):
        sink_row = jnp.where(head_of_col == h8, sinks_ref[h8], sink_row)
    half = Q_PER_KV * CHUNK
    for j0 in range(0, n_chunks, ATT_GROUP):
        chunk_ids = range(j0, min(j0 + ATT_GROUP, n_chunks))
        scores = []
        for j in chunk_ids:
            r0 = j * CHUNK
            qs = jnp.concatenate(
                [q_of_group[g][r0:r0 + CHUNK, hh * LANES:(hh + 1) * LANES]
                 for g in range(N_KV_HEADS) for hh in range(Q_PER_KV)], axis=0)
            scores.append(_dot_nt(kfull[r0:r0 + band, :], qs))
        probs = []
        for j, s in zip(chunk_ids, scores):
            r0 = j * CHUNK
            if r0 < WINDOW:
                first_valid = jnp.where(i == 0, WINDOW - r0, 0)
                s = jnp.where(key_row >= first_valid, s, NEG_INF)
            m = jnp.maximum(jnp.max(s, axis=0, keepdims=True), sink_row)
            probs.append((jnp.exp(s - m).astype(BF16), jnp.exp(sink_row - m)))
        outs_t = []
        for j, (e, sink_e) in zip(chunk_ids, probs):
            vtb = vt[j % 2, :, (j // 2) * LANES:(j // 2) * LANES + band]
            oa = _dot(vtb, e)
            inv_den = 1.0 / (oa[D_KV:D_KV + 1] + sink_e)
            outs_t.append(jnp.concatenate(
                [oa[g * HEAD_DIM:(g + 1) * HEAD_DIM, g * half:(g + 1) * half]
                 * inv_den[:, g * half:(g + 1) * half] for g in range(N_KV_HEADS)],
                axis=0))
        for j, ot in zip(chunk_ids, outs_t):
            r0 = j * CHUNK
            o = ot.T
            for hh in range(Q_PER_KV):
                mix_buf[r0:r0 + CHUNK, hh * LANES:(hh + 1) * LANES] = (
                    o[hh * CHUNK:(hh + 1) * CHUNK].astype(BF16))

    uf = ufull[...]
    pos = i * T + lax.broadcasted_iota(jnp.int32, (T, 1), 0)
    for gi, w in enumerate(POOL_WINDOWS):
        sl = slice(gi * POOL_GROUP, (gi + 1) * POOL_GROUP)
        tsum = _window_sum(uf[:, sl], w)[POOL_HALO:]
        cnt = jnp.minimum(pos + 1, w).astype(F32)
        d = (tsum / cnt - u[:, sl]).astype(BF16)
        pool = _dot(d, wpool_ref[gi].astype(BF16)) * pscale_ref[:, sl]
        mix_buf[:, D_ATTN + gi * POOL_GROUP:D_ATTN + (gi + 1) * POOL_GROUP] = pool.astype(BF16)

    kfull[0:WINDOW, :] = kfull[T:T + WINDOW, :]
    vfull[0:WINDOW, :] = vfull[T:T + WINDOW, :]
    ufull[0:POOL_HALO, :] = u[T - POOL_HALO:, :]

    x1 = x + _dot(mix_buf[...], wout_ref[...])

    xn2 = _rms_rows(x1, gffn_ref[...]).astype(BF16)
    F = FF_CHUNK
    y, k_done, acts = x1, 0, []
    row_in_group = lax.broadcasted_iota(jnp.int32, (1, CONV_HALO, 1), 1)
    for c in range(d_ff // F):
        cols = (slice(c * F, (c + 1) * F), slice(d_ff + c * F, d_ff + (c + 1) * F))
        taps = []
        for half, cs in enumerate(cols):
            h = _dot(xn2, wup_ref[:, cs])
            groups = jnp.concatenate([hprev[:, cs], h], axis=0).reshape(T // CONV_HALO + 1, CONV_HALO, F)
            shifted = []
            for sh in (1, 2):
                r = pltpu.roll(groups, sh, 1)
                shifted.append(jnp.where(row_in_group < sh, r[:-1], r[1:]).reshape(T, F))
            hlast_ref[:, cs] = h[T - hlast_ref.shape[0]:, :]
            hprev[:, cs] = h[T - CONV_HALO:, :]
            taps.append((h, shifted[0], shifted[1]))
        acts.append(_conv_gate(taps[0], taps[1], cw_ref[:, cols[0]], cw_ref[:, cols[1]],
                               cb_ref[:, cols[0]], cb_ref[:, cols[1]]))
        c_done = c - DOWN_LAG
        if c_done >= 0 and (c_done + 1) % DOWN_GROUP == 0:
            ks = slice((c_done + 1 - DOWN_GROUP) * F, (c_done + 1) * F)
            group = jnp.concatenate(acts[c_done + 1 - DOWN_GROUP:c_done + 1], axis=1)
            y = y + _dot(group, wdown_ref[ks, :])
            k_done = (c_done + 1) * F
    y_ref[...] = y + _dot(jnp.concatenate(acts[k_done // F:], axis=1), wdown_ref[k_done:, :])


def _const_spec(shape):
    nd = len(shape)
    return pl.BlockSpec(shape, lambda *_: (0,) * nd)


def _chunk_spec(stacked, layer):
    rows = stacked.shape[1] // N_PREP
    return pl.BlockSpec((None, rows, stacked.shape[2]),
                        lambda s: (layer, jnp.minimum(s, N_PREP - 1), 0))


def _layer_spec(stacked, layer, resident=False):
    nd = stacked.ndim - 1
    return pl.BlockSpec((None,) + stacked.shape[1:], lambda *_: (layer,) + (0,) * nd,
                        pipeline_mode=pl.Buffered(1) if resident else None)


def _weight_specs(w):
    layer = w["layer"]
    return [
        _const_spec(w["g_mix"].shape), _chunk_spec(w["w_in"], layer),
        _const_spec(w["q_gain"].shape), _const_spec(w["k_gain"].shape), _const_spec(w["bd"].shape),
        _layer_spec(w["w_pool"], layer), _const_spec(w["pool_scale"].shape), _chunk_spec(w["w_out"], layer),
        _const_spec(w["g_ffn"].shape), _chunk_spec(w["w_up"], layer),
        _layer_spec(w["conv_w"], layer), _const_spec(w["conv_b"].shape), _chunk_spec(w["w_down"], layer),
    ]


def _weight_args(w):
    return (w["g_mix"], w["w_in"], w["q_gain"], w["k_gain"], w["bd"], w["w_pool"],
            w["pool_scale"], w["w_out"], w["g_ffn"], w["w_up"], w["conv_w"], w["conv_b"], w["w_down"])


def _weight_scratch(w):
    return [pltpu.VMEM(w[name].shape[1:], BF16) for name in ("w_in", "w_out", "w_up", "w_down")]


def _resident_spec(shape):
    nd = len(shape)
    return pl.BlockSpec(shape, lambda *_: (0,) * nd, pipeline_mode=pl.Buffered(1))


def _layer_call(x, tabs, xs, tabs_s, w):
    B, L, D = x.shape
    S, n_new, _ = xs.shape
    M = S * n_new
    T = PROMPT_TILE
    d_ff = w["w_down"].shape[1]
    n_tiles = L // T
    n_total = B * n_tiles
    layer = w["layer"]
    n_pool, n_conv = w["n_pool"], w["conv_prev"].shape[2]
    tile_of = lambda s: jnp.clip(s - N_PREP, 0, n_total - 1)
    tile_spec = pl.BlockSpec((None, T, D), lambda s: (tile_of(s) // n_tiles, tile_of(s) % n_tiles, 0))
    tab_spec = pl.BlockSpec((T, tabs.shape[1]), lambda s: (tile_of(s) % n_tiles, 0))

    def last_spec(rows, cols):
        return pl.BlockSpec((None, rows, cols), lambda s: (tile_of(s) // n_tiles, 0, 0))

    sample_acts = (xs.reshape(M, D), tabs_s)
    sample_state = (w["cache_k"], w["cache_v"], w["pool_prev"], w["conv_prev"])
    in_specs = ([pl.BlockSpec(memory_space=pltpu.SMEM), tile_spec, tab_spec]
                + _weight_specs(w) + [_resident_spec(a.shape) for a in sample_acts]
                + [_layer_spec(a, layer, resident=True) for a in sample_state])
    sample_out = (
        jax.ShapeDtypeStruct((M, D), F32),
        jax.ShapeDtypeStruct((S, WINDOW, D_KV), F32),
        jax.ShapeDtypeStruct((S, WINDOW, D_KV), F32),
        jax.ShapeDtypeStruct((S, n_pool, D_POOL), F32),
        jax.ShapeDtypeStruct((S, n_conv, 2 * d_ff), F32),
    )
    out_shape = (
        jax.ShapeDtypeStruct((B, L, D), F32),
        jax.ShapeDtypeStruct((B, WINDOW, D_KV), F32),
        jax.ShapeDtypeStruct((B, WINDOW, D_KV), F32),
        jax.ShapeDtypeStruct((B, n_pool, D_POOL), F32),
        jax.ShapeDtypeStruct((B, n_conv, 2 * d_ff), F32),
    ) + sample_out
    out_specs = (tile_spec, last_spec(WINDOW, D_KV), last_spec(WINDOW, D_KV),
                 last_spec(n_pool, D_POOL), last_spec(n_conv, 2 * d_ff)
                 ) + tuple(_resident_spec(o.shape) for o in sample_out)
    scratch = _weight_scratch(w) + [
        pltpu.VMEM((WINDOW + T + KV_PAD, D_KV), BF16),
        pltpu.VMEM((WINDOW + T + KV_PAD, D_KV), BF16),
        pltpu.VMEM((2, VT_ROWS, T + WINDOW), BF16),
        pltpu.VMEM((POOL_HALO + T, D_POOL), F32),
        pltpu.VMEM((CONV_HALO, 2 * d_ff), F32),
        pltpu.VMEM((T, D_ATTN + D_POOL), BF16),
        pltpu.VMEM((2 * M, D_POOL), F32),
        pltpu.VMEM((M, D_ATTN + D_POOL), BF16),
    ]
    outs = pl.pallas_call(
        functools.partial(_prompt_kernel, layer=layer, n_tiles=n_tiles, n_total=n_total),
        out_shape=out_shape,
        grid=(N_PREP + n_total + 1,),
        in_specs=in_specs,
        out_specs=out_specs,
        scratch_shapes=scratch,
        name="layer",
        compiler_params=pltpu.CompilerParams(
            dimension_semantics=("arbitrary",),
            vmem_limit_bytes=VMEM_LIMIT_BYTES),
    )(w["sinks"], x, tabs, *_weight_args(w), *sample_acts, *sample_state)
    ys, ko, vo, uo, ho = outs[5:]
    return outs[:5], (ys.reshape(S, n_new, D), ko, vo, uo, ho)


class _SampleRefs(NamedTuple):
    x_ref: Any
    tab_ref: Any
    ck_ref: Any
    cv_ref: Any
    pprev_ref: Any
    cprev_ref: Any
    y_ref: Any
    kout_ref: Any
    vout_ref: Any
    uout_ref: Any
    hout_ref: Any
    ufull: Any
    mix_buf: Any


def _sample_mixer(q, k, v, u, sinks_ref, wpool_ref, pscale_ref, sample):
    ck_ref, cv_ref, mix_buf, ufull = sample.ck_ref, sample.cv_ref, sample.mix_buf, sample.ufull
    M = q.shape[0]
    n_streams, past = ck_ref.shape[0], ck_ref.shape[1]
    n_new = M // n_streams
    n_prev = sample.pprev_ref.shape[1]
    uout_ref = sample.uout_ref
    uout_ref[...] = u.reshape(n_streams, n_new, u.shape[1])[:, n_new - uout_ref.shape[1]:, :]

    for s_ in range(n_streams):
        rows = slice(s_ * n_new, (s_ + 1) * n_new)
        keys = jnp.concatenate([ck_ref[s_], k[rows]], axis=0)
        vals = jnp.concatenate([cv_ref[s_], v[rows]], axis=0)
        sample.kout_ref[s_] = keys[past + n_new - WINDOW:, :]
        sample.vout_ref[s_] = vals[past + n_new - WINDOW:, :]
        keys16 = keys.astype(BF16)
        vals16 = vals.astype(BF16)
        outs = []
        for g in range(N_KV_HEADS):
            sink_col = jnp.concatenate(
                [jnp.full((n_new, 1), sinks_ref[g * Q_PER_KV + hh], F32) for hh in range(Q_PER_KV)],
                axis=0)
            qs = jnp.concatenate(
                [q[rows, hh * LANES + g * HEAD_DIM:hh * LANES + (g + 1) * HEAD_DIM]
                 for hh in range(Q_PER_KV)], axis=0)
            sc = _dot_nt(qs, keys16[:, g * HEAD_DIM:(g + 1) * HEAD_DIM])
            m = jnp.maximum(jnp.max(sc, axis=-1, keepdims=True), sink_col)
            e = jnp.exp(sc - m)
            den = jnp.sum(e, axis=-1, keepdims=True) + jnp.exp(sink_col - m)
            outs.append(_dot(e.astype(BF16), vals16[:, g * HEAD_DIM:(g + 1) * HEAD_DIM]) * (1.0 / den))
        for hh in range(Q_PER_KV):
            pair = jnp.concatenate([o[hh * n_new:(hh + 1) * n_new] for o in outs], axis=1)
            mix_buf[rows, hh * LANES:(hh + 1) * LANES] = pair.astype(BF16)
        base = s_ * 2 * n_new
        ufull[base:base + n_new - n_prev, :] = jnp.zeros((n_new - n_prev, ufull.shape[1]), F32)
        ufull[base + n_new - n_prev:base + n_new, :] = sample.pprev_ref[s_]
        ufull[base + n_new:base + 2 * n_new, :] = u[rows]

    uf = ufull[...]
    for gi, w in enumerate(POOL_WINDOWS):
        sl = slice(gi * POOL_GROUP, (gi + 1) * POOL_GROUP)
        ws = _window_sum(uf[:, sl], w)
        tsum = jnp.concatenate(
            [ws[s_ * 2 * n_new + n_new:(s_ + 1) * 2 * n_new] for s_ in range(n_streams)], axis=0)
        d = (tsum / float(w) - u[:, sl]).astype(BF16)
        pool = _dot(d, wpool_ref[gi].astype(BF16)) * pscale_ref[:, sl]
        mix_buf[:, D_ATTN + gi * POOL_GROUP:D_ATTN + (gi + 1) * POOL_GROUP] = pool.astype(BF16)


def _sample_taps(h, cs, sample):
    cprev_ref, hout_ref = sample.cprev_ref, sample.hout_ref
    M, F = h.shape
    n_streams, n_conv = cprev_ref.shape[0], cprev_ref.shape[1]
    n_new = M // n_streams
    t_in_stream = lax.broadcasted_iota(jnp.int32, (M, 1), 0) % n_new
    hout_ref[:, :, cs] = h.reshape(n_streams, n_new, F)[:, n_new - hout_ref.shape[1]:, :]
    prev = [jnp.broadcast_to(cprev_ref[:, j:j + 1, cs], (n_streams, n_new, F)).reshape(M, F)
            for j in range(n_conv)]
    return (h, jnp.where(t_in_stream >= 1, pltpu.roll(h, 1, 0), prev[1]),
            jnp.where(t_in_stream >= 2, pltpu.roll(h, 2, 0),
                      jnp.where(t_in_stream == 0, prev[0], prev[1])))


def _sample_step(sinks_ref, gmix_ref, win_ref, qg_ref, kg_ref, bd_ref, wpool_ref, pscale_ref,
                 wout_ref, gffn_ref, wup_ref, cw_ref, cb_ref, wdown_ref, sample):
    d_ff = wdown_ref.shape[0]
    x = sample.x_ref[...]
    q, k, v, u = _mixer_inputs(x, gmix_ref[...], win_ref[...], qg_ref[...], kg_ref[...],
                               bd_ref[...], *_rope_coefficients(sample.tab_ref))
    _sample_mixer(q.astype(BF16), k, v, u, sinks_ref, wpool_ref, pscale_ref, sample)
    x1 = x + _dot(sample.mix_buf[...], wout_ref[...])
    xn2 = _rms_rows(x1, gffn_ref[...]).astype(BF16)
    acc = x1
    for c in range(d_ff // FF_CHUNK):
        cg = slice(c * FF_CHUNK, (c + 1) * FF_CHUNK)
        cv = slice(d_ff + c * FF_CHUNK, d_ff + (c + 1) * FF_CHUNK)
        taps_g = _sample_taps(_dot(xn2, wup_ref[:, cg]), cg, sample)
        taps_v = _sample_taps(_dot(xn2, wup_ref[:, cv]), cv, sample)
        act = _conv_gate(taps_g, taps_v, cw_ref[:, cg], cw_ref[:, cv], cb_ref[:, cg], cb_ref[:, cv])
        acc = acc + _dot(act, wdown_ref[cg, :])
    sample.y_ref[...] = acc


def _rope_tables(pos, reps=1):
    half = ROT_DIM // 2
    inv = ROPE_THETA ** (-jnp.arange(0, ROT_DIM, 2, dtype=F32) / ROT_DIM)
    ang = pos.astype(F32)[:, None] * inv[None, :]
    compact = jnp.concatenate([jnp.cos(ang), jnp.sin(ang), jnp.ones((pos.shape[0], 1), F32)], axis=1)
    src = jnp.arange(2 * half + 1)[:, None]
    lane = jnp.arange(3 * LANES)[None, :]
    table, dim = lane // LANES, lane % HEAD_DIM
    freq, rotated, low = dim % half, dim < ROT_DIM, dim < half
    plus = (((table == 0) & rotated & (src == freq))
            | ((table == 0) & ~rotated & (src == 2 * half))
            | ((table == 2) & rotated & ~low & (src == half + freq)))
    minus = (table == 1) & low & (src == half + freq)
    sel = plus.astype(F32) - minus.astype(F32)
    return jnp.dot(jnp.tile(compact, (reps, 1)), sel, precision=lax.Precision.HIGHEST)


def kernel(x_prompt, x_sample, cache_k, cache_v, state_pool, state_conv, norm_mix, w_in, q_norm,
           k_norm, attn_sinks, w_pool, pool_scale, w_out, norm_ffn, w_up, conv_w, conv_b, w_down):
    depth = w_in.shape[0]
    B, L, D = x_prompt.shape
    S, n_new, _ = x_sample.shape
    past_len = L
    assert L % PROMPT_TILE == 0 and PROMPT_TILE % CHUNK == 0 and PROMPT_TILE >= WINDOW
    assert w_down.shape[1] % FF_CHUNK == 0
    assert all(m.shape[1] % (BF16_SUBLANES * N_PREP) == 0 for m in (w_in, w_up, w_down))
    assert w_out.shape[1] == N_PREP * HEAD_DIM and LANES == N_KV_HEADS * HEAD_DIM

    tabs_p = _rope_tables(jnp.arange(L))
    tabs_s = _rope_tables(past_len + jnp.arange(n_new), reps=S)
    head_id = jnp.arange(MXU_COLS) // HEAD_DIM
    bd = jnp.where(head_id[:, None] == head_id[None, :], 1.0 / HEAD_DIM, 0.0).astype(BF16)
    q_scale = HEAD_DIM ** -0.5
    n_pool = state_pool.shape[2]
    assert max(POOL_WINDOWS) - 1 <= n_pool <= n_new
    n_conv = state_conv.shape[2]
    past = cache_k.shape[2]

    stacked = dict(
        g_mix=norm_mix, w_in=w_in,
        q_gain=jnp.tile(q_norm * q_scale, (1, N_Q_HEADS)),
        k_gain=jnp.tile(k_norm, (1, N_KV_HEADS)),
        bd=bd, w_pool=w_pool, pool_scale=pool_scale,
        w_out=w_out, g_ffn=norm_ffn, w_up=w_up,
        conv_w=conv_w, conv_b=conv_b, w_down=w_down,
        cache_k=cache_k.reshape(depth, S, past, D_KV), cache_v=cache_v.reshape(depth, S, past, D_KV),
        pool_prev=state_pool, conv_prev=state_conv,
    )

    yp, ys = x_prompt, x_sample
    outs = [[] for _ in range(8)]
    for i in range(depth):
        w = dict(stacked, layer=i, sinks=attn_sinks, n_pool=n_pool)
        (yp, k1, v1, u1, h1), (ys, k2, v2, u2, h2) = _layer_call(yp, tabs_p, ys, tabs_s, w)
        outs[0].append(k1.reshape(B, WINDOW, N_KV_HEADS, HEAD_DIM))
        outs[1].append(v1.reshape(B, WINDOW, N_KV_HEADS, HEAD_DIM))
        outs[2].append(u1)
        outs[3].append(h1)
        outs[4].append(k2.reshape(S, WINDOW, N_KV_HEADS, HEAD_DIM))
        outs[5].append(v2.reshape(S, WINDOW, N_KV_HEADS, HEAD_DIM))
        outs[6].append(u2)
        outs[7].append(h2)
    return (yp, ys) + tuple(jnp.stack(o) for o in outs)
```

```python
import functools
from typing import Any, NamedTuple

import jax
import jax.numpy as jnp
from jax import lax
from jax.experimental import pallas as pl
from jax.experimental.pallas import tpu as pltpu

F32 = jnp.float32
BF16 = jnp.bfloat16

CHUNK = 64
HEAD_DIM = 64
N_Q_HEADS = 8
N_KV_HEADS = 2
Q_PER_KV = N_Q_HEADS // N_KV_HEADS
D_ATTN = N_Q_HEADS * HEAD_DIM
D_KV = N_KV_HEADS * HEAD_DIM
WINDOW = 128
ROT_DIM = 16
ROPE_THETA = 500000.0
POOL_WINDOWS = (2, 4, 8, 16)
POOL_GROUP = 128
D_POOL = POOL_GROUP * len(POOL_WINDOWS)
POOL_HALO = 16
CONV_W = 3
CONV_HALO = 8
EPS = 1e-6
NEG_INF = -1e30
NEG_LOG2_E = -1.4426950408889634

LANES = 128
MXU_COLS = 256
BF16_SUBLANES = 16
KV_PAD = CHUNK
VT_ROWS = D_KV + BF16_SUBLANES

PROMPT_TILE = 512
ATT_GROUP = 4
FF_CHUNK = 256
SAMPLE_FF_SPLIT = 1
DOWN_GROUP = 3
DOWN_LAG = 2
N_PREP = 16
VMEM_LIMIT_BYTES = 56 * 1024 * 1024


def _dot(a, b):
    return jnp.dot(a, b, preferred_element_type=F32)


def _dot_nt(a, b):
    return lax.dot_general(a, b, (((1,), (1,)), ((), ())), preferred_element_type=F32)


def _rms_rows(x, gain):
    ms = jnp.mean(x * x, axis=-1, keepdims=True)
    return x * lax.rsqrt(ms + EPS) * gain


def _head_rms(t, bd, gain):
    ms = _dot((t * t).astype(BF16), bd)
    return t * lax.rsqrt(ms + EPS) * gain


def _rope(t, cos, sin_a, sin_b):
    return (t * cos + pltpu.roll(t, LANES - ROT_DIM // 2, 1) * sin_a
            + pltpu.roll(t, ROT_DIM // 2, 1) * sin_b)


def _rope_coefficients(tab_ref):
    return [tab_ref[:, j * LANES:(j + 1) * LANES] for j in range(3)]


def _mixer_inputs(x, gmix, w_in, qg, kg, bd, cos, sin_a, sin_b):
    xn = _rms_rows(x, gmix).astype(BF16)
    h = _dot(xn, w_in)
    q_parts = []
    for j in range(D_ATTN // MXU_COLS):
        qb = _head_rms(h[:, j * MXU_COLS:(j + 1) * MXU_COLS], bd,
                       qg[:, j * MXU_COLS:(j + 1) * MXU_COLS])
        for l in range(MXU_COLS // LANES):
            q_parts.append(_rope(qb[:, l * LANES:(l + 1) * LANES], cos, sin_a, sin_b))
    q = jnp.concatenate(q_parts, axis=1)
    k = _head_rms(h[:, D_ATTN:D_ATTN + D_KV], bd[:D_KV, :D_KV], kg)
    k = _rope(k, cos, sin_a, sin_b)
    v = h[:, D_ATTN + D_KV:D_ATTN + 2 * D_KV]
    u = h[:, D_ATTN + 2 * D_KV:]
    return q, k, v, u


def _window_sum(a, w):
    s = 1
    while s < w:
        a = a + pltpu.roll(a, s, 0)
        s *= 2
    return a


def _conv_gate(hg, hv, cw_g, cw_v, cb_g, cb_v):
    cg = cb_g + hg[2] * cw_g[0:1] + hg[1] * cw_g[1:2] + hg[0] * cw_g[2:3]
    cv = cb_v + hv[2] * cw_v[0:1] + hv[1] * cw_v[1:2] + hv[0] * cw_v[2:3]
    return ((cg / (1.0 + jnp.exp2(cg * NEG_LOG2_E))) * cv).astype(BF16)


def _pair_heads(qcols):
    lane = lax.broadcasted_iota(jnp.int32, (1, LANES), 1)
    blocks = [qcols[:, b * LANES:(b + 1) * LANES] for b in range(D_ATTN // LANES)]
    per_block = LANES // HEAD_DIM
    out = []
    for hh in range(Q_PER_KV):
        a = blocks[hh // per_block]
        b = blocks[Q_PER_KV // per_block + hh // per_block]
        if hh % per_block == 0:
            out.append(jnp.where(lane < HEAD_DIM, a, pltpu.roll(b, HEAD_DIM, 1)))
        else:
            out.append(jnp.where(lane < HEAD_DIM, pltpu.roll(a, HEAD_DIM, 1), b))
    return jnp.concatenate(out, axis=1)


def _stage_weights(s, win_c, wout_c, wup_c, wdown_c, w_in_s, w_out_s, w_up_s, w_down_s):
    r_in = win_c.shape[0]
    r0 = pl.multiple_of(s * r_in, r_in)
    c = win_c[...]
    w_in_s[pl.ds(r0, r_in), :] = jnp.concatenate(
        [_pair_heads(c[:, :D_ATTN]), c[:, D_ATTN:]], axis=1).astype(BF16)
    w_up_s[pl.ds(r0, r_in), :] = wup_c[...].astype(BF16)
    r_dn = wdown_c.shape[0]
    w_down_s[pl.ds(pl.multiple_of(s * r_dn, r_dn), r_dn), :] = wdown_c[...].astype(BF16)
    dst = jnp.where(s < N_Q_HEADS, (s % Q_PER_KV) * N_KV_HEADS + s // Q_PER_KV, s)
    w_out_s[pl.ds(pl.multiple_of(dst * HEAD_DIM, HEAD_DIM), HEAD_DIM), :] = wout_c[...].astype(BF16)


def _prompt_kernel(sinks_ref, x_ref, tab_ref,
                   gmix_ref, win_c, qg_ref, kg_ref, bd_ref, wpool_ref, pscale_ref, wout_c,
                   gffn_ref, wup_c, cw_ref, cb_ref, wdown_c,
                   xs_ref, tab_s_ref, ck_ref, cv_ref, pprev_ref, cprev_ref,
                   y_ref, klast_ref, vlast_ref, ulast_ref, hlast_ref,
                   ys_ref, kout_ref, vout_ref, uout_ref, hout_ref,
                   w_in_s, w_out_s, w_up_s, w_down_s,
                   kfull, vfull, vt, ufull, hprev, mix_buf, ufull_s, mix_s,
                   *, layer, n_tiles, n_total):
    s = pl.program_id(0)
    sinks_ref = sinks_ref.at[layer]
    gmix_ref, qg_ref, kg_ref, pscale_ref, gffn_ref, cb_ref = (
        r.at[layer:layer + 1] for r in (gmix_ref, qg_ref, kg_ref, pscale_ref, gffn_ref, cb_ref))

    @pl.when(s < N_PREP)
    def _():
        _stage_weights(s, win_c, wout_c, wup_c, wdown_c, w_in_s, w_out_s, w_up_s, w_down_s)

    @pl.when((s >= N_PREP) & (s < N_PREP + n_total))
    def _():
        _prompt_body((s - N_PREP) % n_tiles, sinks_ref, x_ref, tab_ref,
                     gmix_ref, w_in_s, qg_ref, kg_ref, bd_ref, wpool_ref, pscale_ref, w_out_s,
                     gffn_ref, w_up_s, cw_ref, cb_ref, w_down_s,
                     y_ref, klast_ref, vlast_ref, ulast_ref, hlast_ref,
                     kfull, vfull, vt, ufull, hprev, mix_buf)

    @pl.when(s == N_PREP + n_total)
    def _():
        _sample_step(sinks_ref, gmix_ref, w_in_s, qg_ref, kg_ref, bd_ref, wpool_ref, pscale_ref,
                     w_out_s, gffn_ref, w_up_s, cw_ref, cb_ref, w_down_s,
                     _SampleRefs(xs_ref, tab_s_ref, ck_ref, cv_ref, pprev_ref,
                                 cprev_ref, ys_ref, kout_ref, vout_ref, uout_ref, hout_ref,
                                 ufull_s, mix_s))


def _prompt_body(i, sinks_ref, x_ref, tab_ref,
                 gmix_ref, win_ref, qg_ref, kg_ref, bd_ref, wpool_ref, pscale_ref, wout_ref,
                 gffn_ref, wup_ref, cw_ref, cb_ref, wdown_ref,
                 y_ref, klast_ref, vlast_ref, ulast_ref, hlast_ref,
                 kfull, vfull, vt, ufull, hprev, mix_buf):
    T = x_ref.shape[0]
    d_ff = wdown_ref.shape[0]
    n_chunks = T // CHUNK

    @pl.when(i == 0)
    def _():
        kfull[0:WINDOW, :] = jnp.zeros((WINDOW, D_KV), BF16)
        kfull[WINDOW + T:, :] = jnp.zeros((KV_PAD, D_KV), BF16)
        vfull[0:WINDOW, :] = jnp.zeros((WINDOW, D_KV), BF16)
        vfull[WINDOW + T:, :] = jnp.zeros((KV_PAD, D_KV), BF16)
        for c in range(2):
            vt[c, D_KV:, :] = jnp.ones((VT_ROWS - D_KV, vt.shape[2]), BF16)
        ufull[0:POOL_HALO, :] = jnp.zeros((POOL_HALO, D_POOL), F32)
        hprev[...] = jnp.zeros(hprev.shape, F32)

    x = x_ref[...]
    q, k, v, u = _mixer_inputs(x, gmix_ref[...], win_ref[...], qg_ref[...], kg_ref[...],
                               bd_ref[...], *_rope_coefficients(tab_ref))
    klast_ref[...] = k[T - WINDOW:, :]
    vlast_ref[...] = v[T - WINDOW:, :]
    ulast_ref[...] = u[T - ulast_ref.shape[0]:, :]
    kfull[WINDOW:WINDOW + T, :] = k.astype(BF16)
    vfull[WINDOW:WINDOW + T, :] = v.astype(BF16)
    ufull[POOL_HALO:POOL_HALO + T, :] = u

    for c in range(2):
        vrows = vfull[c * CHUNK:c * CHUNK + T + WINDOW, :]
        vt[c, 0:D_KV, :] = vrows.astype(F32).T.astype(BF16)
    band = WINDOW + CHUNK
    n_q = N_Q_HEADS * CHUNK
    lane = lax.broadcasted_iota(jnp.int32, (1, D_ATTN), 1)
    head_of_col = lax.broadcasted_iota(jnp.int32, (1, n_q), 1) // CHUNK
    key_row = lax.broadcasted_iota(jnp.int32, (band, 1), 0)
    q_of_group = []
    for g in range(N_KV_HEADS):
        in_g = ((lane // HEAD_DIM) % N_KV_HEADS == g).astype(F32)
        q_of_group.append((q * in_g).astype(BF16))
    sink_row = jnp.full((1, n_q), sinks_ref[0], F32)
    for h8 in range(1, N_Q_HEADS):
        sink_row = jnp.where(head_of_col == h8, sinks_ref[h8], sink_row)
    half = Q_PER_KV * CHUNK
    for j0 in range(0, n_chunks, ATT_GROUP):
        chunk_ids = range(j0, min(j0 + ATT_GROUP, n_chunks))
        scores = []
        for j in chunk_ids:
            r0 = j * CHUNK
            qs = jnp.concatenate(
                [q_of_group[g][r0:r0 + CHUNK, hh * LANES:(hh + 1) * LANES]
                 for g in range(N_KV_HEADS) for hh in range(Q_PER_KV)], axis=0)
            scores.append(_dot_nt(kfull[r0:r0 + band, :], qs))
        probs = []
        for j, s in zip(chunk_ids, scores):
            r0 = j * CHUNK
            if r0 < WINDOW:
                first_valid = jnp.where(i == 0, WINDOW - r0, 0)
                s = jnp.where(key_row >= first_valid, s, NEG_INF)
            m = jnp.maximum(jnp.max(s, axis=0, keepdims=True), sink_row)
            probs.append((jnp.exp(s - m).astype(BF16), jnp.exp(sink_row - m)))
        outs_t = []
        for j, (e, sink_e) in zip(chunk_ids, probs):
            vtb = vt[j % 2, :, (j // 2) * LANES:(j // 2) * LANES + band]
            oa = _dot(vtb, e)
            inv_den = 1.0 / (oa[D_KV:D_KV + 1] + sink_e)
            outs_t.append(jnp.concatenate(
                [oa[g * HEAD_DIM:(g + 1) * HEAD_DIM, g * half:(g + 1) * half]
                 * inv_den[:, g * half:(g + 1) * half] for g in range(N_KV_HEADS)],
                axis=0))
        for j, ot in zip(chunk_ids, outs_t):
            r0 = j * CHUNK
            o = ot.T
            for hh in range(Q_PER_KV):
                mix_buf[r0:r0 + CHUNK, hh * LANES:(hh + 1) * LANES] = (
                    o[hh * CHUNK:(hh + 1) * CHUNK].astype(BF16))

    uf = ufull[...]
    pos = i * T + lax.broadcasted_iota(jnp.int32, (T, 1), 0)
    for gi, w in enumerate(POOL_WINDOWS):
        sl = slice(gi * POOL_GROUP, (gi + 1) * POOL_GROUP)
        tsum = _window_sum(uf[:, sl], w)[POOL_HALO:]
        cnt = jnp.minimum(pos + 1, w).astype(F32)
        d = (tsum / cnt - u[:, sl]).astype(BF16)
        pool = _dot(d, wpool_ref[gi].astype(BF16)) * pscale_ref[:, sl]
        mix_buf[:, D_ATTN + gi * POOL_GROUP:D_ATTN + (gi + 1) * POOL_GROUP] = pool.astype(BF16)

    kfull[0:WINDOW, :] = kfull[T:T + WINDOW, :]
    vfull[0:WINDOW, :] = vfull[T:T + WINDOW, :]
    ufull[0:POOL_HALO, :] = u[T - POOL_HALO:, :]

    x1 = x + _dot(mix_buf[...], wout_ref[...])

    xn2 = _rms_rows(x1, gffn_ref[...]).astype(BF16)
    F = FF_CHUNK
    y, k_done, acts = x1, 0, []
    row_in_group = lax.broadcasted_iota(jnp.int32, (1, CONV_HALO, 1), 1)
    for c in range(d_ff // F):
        cols = (slice(c * F, (c + 1) * F), slice(d_ff + c * F, d_ff + (c + 1) * F))
        taps = []
        for half, cs in enumerate(cols):
            h = _dot(xn2, wup_ref[:, cs])
            groups = jnp.concatenate([hprev[:, cs], h], axis=0).reshape(T // CONV_HALO + 1, CONV_HALO, F)
            shifted = []
            for sh in (1, 2):
                r = pltpu.roll(groups, sh, 1)
                shifted.append(jnp.where(row_in_group < sh, r[:-1], r[1:]).reshape(T, F))
            hlast_ref[:, cs] = h[T - hlast_ref.shape[0]:, :]
            hprev[:, cs] = h[T - CONV_HALO:, :]
            taps.append((h, shifted[0], shifted[1]))
        acts.append(_conv_gate(taps[0], taps[1], cw_ref[:, cols[0]], cw_ref[:, cols[1]],
                               cb_ref[:, cols[0]], cb_ref[:, cols[1]]))
        c_done = c - DOWN_LAG
        if c_done >= 0 and (c_done + 1) % DOWN_GROUP == 0:
            ks = slice((c_done + 1 - DOWN_GROUP) * F, (c_done + 1) * F)
            group = jnp.concatenate(acts[c_done + 1 - DOWN_GROUP:c_done + 1], axis=1)
            y = y + _dot(group, wdown_ref[ks, :])
            k_done = (c_done + 1) * F
    y_ref[...] = y + _dot(jnp.concatenate(acts[k_done // F:], axis=1), wdown_ref[k_done:, :])


def _const_spec(shape):
    nd = len(shape)
    return pl.BlockSpec(shape, lambda *_: (0,) * nd)


def _chunk_spec(stacked, layer):
    rows = stacked.shape[1] // N_PREP
    return pl.BlockSpec((None, rows, stacked.shape[2]),
                        lambda s: (layer, jnp.minimum(s, N_PREP - 1), 0))


def _layer_spec(stacked, layer, resident=False):
    nd = stacked.ndim - 1
    return pl.BlockSpec((None,) + stacked.shape[1:], lambda *_: (layer,) + (0,) * nd,
                        pipeline_mode=pl.Buffered(1) if resident else None)


def _weight_specs(w):
    layer = w["layer"]
    return [
        _const_spec(w["g_mix"].shape), _chunk_spec(w["w_in"], layer),
        _const_spec(w["q_gain"].shape), _const_spec(w["k_gain"].shape), _const_spec(w["bd"].shape),
        _layer_spec(w["w_pool"], layer), _const_spec(w["pool_scale"].shape), _chunk_spec(w["w_out"], layer),
        _const_spec(w["g_ffn"].shape), _chunk_spec(w["w_up"], layer),
        _layer_spec(w["conv_w"], layer), _const_spec(w["conv_b"].shape), _chunk_spec(w["w_down"], layer),
    ]


def _weight_args(w):
    return (w["g_mix"], w["w_in"], w["q_gain"], w["k_gain"], w["bd"], w["w_pool"],
            w["pool_scale"], w["w_out"], w["g_ffn"], w["w_up"], w["conv_w"], w["conv_b"], w["w_down"])


def _weight_scratch(w):
    return [pltpu.VMEM(w[name].shape[1:], BF16) for name in ("w_in", "w_out", "w_up", "w_down")]


def _resident_spec(shape):
    nd = len(shape)
    return pl.BlockSpec(shape, lambda *_: (0,) * nd, pipeline_mode=pl.Buffered(1))


def _layer_call(x, tabs, xs, tabs_s, w):
    B, L, D = x.shape
    S, n_new, _ = xs.shape
    M = S * n_new
    T = PROMPT_TILE
    d_ff = w["w_down"].shape[1]
    n_tiles = L // T
    n_total = B * n_tiles
    layer = w["layer"]
    n_pool, n_conv = w["n_pool"], w["conv_prev"].shape[2]
    tile_of = lambda s: jnp.clip(s - N_PREP, 0, n_total - 1)
    tile_spec = pl.BlockSpec((None, T, D), lambda s: (tile_of(s) // n_tiles, tile_of(s) % n_tiles, 0))
    tab_spec = pl.BlockSpec((T, tabs.shape[1]), lambda s: (tile_of(s) % n_tiles, 0))

    def last_spec(rows, cols):
        return pl.BlockSpec((None, rows, cols), lambda s: (tile_of(s) // n_tiles, 0, 0))

    sample_acts = (xs.reshape(M, D), tabs_s)
    sample_state = (w["cache_k"], w["cache_v"], w["pool_prev"], w["conv_prev"])
    in_specs = ([pl.BlockSpec(memory_space=pltpu.SMEM), tile_spec, tab_spec]
                + _weight_specs(w) + [_resident_spec(a.shape) for a in sample_acts]
                + [_layer_spec(a, layer, resident=True) for a in sample_state])
    sample_out = (
        jax.ShapeDtypeStruct((M, D), F32),
        jax.ShapeDtypeStruct((S, WINDOW, D_KV), F32),
        jax.ShapeDtypeStruct((S, WINDOW, D_KV), F32),
        jax.ShapeDtypeStruct((S, n_pool, D_POOL), F32),
        jax.ShapeDtypeStruct((S, n_conv, 2 * d_ff), F32),
    )
    out_shape = (
        jax.ShapeDtypeStruct((B, L, D), F32),
        jax.ShapeDtypeStruct((B, WINDOW, D_KV), F32),
        jax.ShapeDtypeStruct((B, WINDOW, D_KV), F32),
        jax.ShapeDtypeStruct((B, n_pool, D_POOL), F32),
        jax.ShapeDtypeStruct((B, n_conv, 2 * d_ff), F32),
    ) + sample_out
    out_specs = (tile_spec, last_spec(WINDOW, D_KV), last_spec(WINDOW, D_KV),
                 last_spec(n_pool, D_POOL), last_spec(n_conv, 2 * d_ff)
                 ) + tuple(_resident_spec(o.shape) for o in sample_out)
    scratch = _weight_scratch(w) + [
        pltpu.VMEM((WINDOW + T + KV_PAD, D_KV), BF16),
        pltpu.VMEM((WINDOW + T + KV_PAD, D_KV), BF16),
        pltpu.VMEM((2, VT_ROWS, T + WINDOW), BF16),
        pltpu.VMEM((POOL_HALO + T, D_POOL), F32),
        pltpu.VMEM((CONV_HALO, 2 * d_ff), F32),
        pltpu.VMEM((T, D_ATTN + D_POOL), BF16),
        pltpu.VMEM((2 * M, D_POOL), F32),
        pltpu.VMEM((M, D_ATTN + D_POOL), BF16),
    ]
    outs = pl.pallas_call(
        functools.partial(_prompt_kernel, layer=layer, n_tiles=n_tiles, n_total=n_total),
        out_shape=out_shape,
        grid=(N_PREP + n_total + 1,),
        in_specs=in_specs,
        out_specs=out_specs,
        scratch_shapes=scratch,
        name="layer",
        compiler_params=pltpu.CompilerParams(
            dimension_semantics=("arbitrary",),
            vmem_limit_bytes=VMEM_LIMIT_BYTES),
    )(w["sinks"], x, tabs, *_weight_args(w), *sample_acts, *sample_state)
    ys, ko, vo, uo, ho = outs[5:]
    return outs[:5], (ys.reshape(S, n_new, D), ko, vo, uo, ho)


class _SampleRefs(NamedTuple):
    x_ref: Any
    tab_ref: Any
    ck_ref: Any
    cv_ref: Any
    pprev_ref: Any
    cprev_ref: Any
    y_ref: Any
    kout_ref: Any
    vout_ref: Any
    uout_ref: Any
    hout_ref: Any
    ufull: Any
    mix_buf: Any


def _sample_mixer(q, k, v, u, sinks_ref, wpool_ref, pscale_ref, sample):
    ck_ref, cv_ref, mix_buf, ufull = sample.ck_ref, sample.cv_ref, sample.mix_buf, sample.ufull
    M = q.shape[0]
    n_streams, past = ck_ref.shape[0], ck_ref.shape[1]
    n_new = M // n_streams
    n_prev = sample.pprev_ref.shape[1]
    uout_ref = sample.uout_ref
    uout_ref[...] = u.reshape(n_streams, n_new, u.shape[1])[:, n_new - uout_ref.shape[1]:, :]

    sink_cols = [jnp.concatenate(
        [jnp.full((n_new, 1), sinks_ref[g * Q_PER_KV + hh], F32) for hh in range(Q_PER_KV)], axis=0)
        for g in range(N_KV_HEADS)]
    scores, values = [], []
    for s_ in range(n_streams):
        rows = slice(s_ * n_new, (s_ + 1) * n_new)
        keys = jnp.concatenate([ck_ref[s_], k[rows]], axis=0)
        vals = jnp.concatenate([cv_ref[s_], v[rows]], axis=0)
        sample.kout_ref[s_] = keys[past + n_new - WINDOW:, :]
        sample.vout_ref[s_] = vals[past + n_new - WINDOW:, :]
        keys16 = keys.astype(BF16)
        values.append(vals.astype(BF16))
        for g in range(N_KV_HEADS):
            qs = jnp.concatenate(
                [q[rows, hh * LANES + g * HEAD_DIM:hh * LANES + (g + 1) * HEAD_DIM]
                 for hh in range(Q_PER_KV)], axis=0)
            scores.append(_dot_nt(qs, keys16[:, g * HEAD_DIM:(g + 1) * HEAD_DIM]))
        base = s_ * 2 * n_new
        ufull[base:base + n_new - n_prev, :] = jnp.zeros((n_new - n_prev, ufull.shape[1]), F32)
        ufull[base + n_new - n_prev:base + n_new, :] = sample.pprev_ref[s_]
        ufull[base + n_new:base + 2 * n_new, :] = u[rows]
    probs = []
    for j, sc in enumerate(scores):
        sink_col = sink_cols[j % N_KV_HEADS]
        m = jnp.maximum(jnp.max(sc, axis=-1, keepdims=True), sink_col)
        e = jnp.exp(sc - m)
        den = jnp.sum(e, axis=-1, keepdims=True) + jnp.exp(sink_col - m)
        probs.append((e.astype(BF16), 1.0 / den))
    for s_ in range(n_streams):
        rows = slice(s_ * n_new, (s_ + 1) * n_new)
        outs = []
        for g in range(N_KV_HEADS):
            e16, inv_den = probs[s_ * N_KV_HEADS + g]
            outs.append(_dot(e16, values[s_][:, g * HEAD_DIM:(g + 1) * HEAD_DIM]) * inv_den)
        for hh in range(Q_PER_KV):
            pair = jnp.concatenate([o[hh * n_new:(hh + 1) * n_new] for o in outs], axis=1)
            mix_buf[rows, hh * LANES:(hh + 1) * LANES] = pair.astype(BF16)

    uf = ufull[...]
    for gi, w in enumerate(POOL_WINDOWS):
        sl = slice(gi * POOL_GROUP, (gi + 1) * POOL_GROUP)
        ws = _window_sum(uf[:, sl], w)
        tsum = jnp.concatenate(
            [ws[s_ * 2 * n_new + n_new:(s_ + 1) * 2 * n_new] for s_ in range(n_streams)], axis=0)
        d = (tsum / float(w) - u[:, sl]).astype(BF16)
        pool = _dot(d, wpool_ref[gi].astype(BF16)) * pscale_ref[:, sl]
        mix_buf[:, D_ATTN + gi * POOL_GROUP:D_ATTN + (gi + 1) * POOL_GROUP] = pool.astype(BF16)


def _sample_taps(h, cs, sample):
    cprev_ref, hout_ref = sample.cprev_ref, sample.hout_ref
    M, F = h.shape
    n_streams, n_conv = cprev_ref.shape[0], cprev_ref.shape[1]
    n_new = M // n_streams
    t_in_stream = lax.broadcasted_iota(jnp.int32, (M, 1), 0) % n_new
    hout_ref[:, :, cs] = h.reshape(n_streams, n_new, F)[:, n_new - hout_ref.shape[1]:, :]
    prev = [jnp.broadcast_to(cprev_ref[:, j:j + 1, cs], (n_streams, n_new, F)).reshape(M, F)
            for j in range(n_conv)]
    return (h, jnp.where(t_in_stream >= 1, pltpu.roll(h, 1, 0), prev[1]),
            jnp.where(t_in_stream >= 2, pltpu.roll(h, 2, 0),
                      jnp.where(t_in_stream == 0, prev[0], prev[1])))


def _sample_step(sinks_ref, gmix_ref, win_ref, qg_ref, kg_ref, bd_ref, wpool_ref, pscale_ref,
                 wout_ref, gffn_ref, wup_ref, cw_ref, cb_ref, wdown_ref, sample):
    d_ff = wdown_ref.shape[0]
    x = sample.x_ref[...]
    q, k, v, u = _mixer_inputs(x, gmix_ref[...], win_ref[...], qg_ref[...], kg_ref[...],
                               bd_ref[...], *_rope_coefficients(sample.tab_ref))
    _sample_mixer(q.astype(BF16), k, v, u, sinks_ref, wpool_ref, pscale_ref, sample)
    x1 = x + _dot(sample.mix_buf[...], wout_ref[...])
    xn2 = _rms_rows(x1, gffn_ref[...]).astype(BF16)
    acc = x1
    F = d_ff // SAMPLE_FF_SPLIT
    for c in range(SAMPLE_FF_SPLIT):
        cg = slice(c * F, (c + 1) * F)
        cv = slice(d_ff + c * F, d_ff + (c + 1) * F)
        taps_g = _sample_taps(_dot(xn2, wup_ref[:, cg]), cg, sample)
        taps_v = _sample_taps(_dot(xn2, wup_ref[:, cv]), cv, sample)
        act = _conv_gate(taps_g, taps_v, cw_ref[:, cg], cw_ref[:, cv], cb_ref[:, cg], cb_ref[:, cv])
        acc = acc + _dot(act, wdown_ref[cg, :])
    sample.y_ref[...] = acc


def _rope_tables(pos, reps=1):
    half = ROT_DIM // 2
    inv = ROPE_THETA ** (-jnp.arange(0, ROT_DIM, 2, dtype=F32) / ROT_DIM)
    ang = pos.astype(F32)[:, None] * inv[None, :]
    compact = jnp.concatenate([jnp.cos(ang), jnp.sin(ang), jnp.ones((pos.shape[0], 1), F32)], axis=1)
    src = jnp.arange(2 * half + 1)[:, None]
    lane = jnp.arange(3 * LANES)[None, :]
    table, dim = lane // LANES, lane % HEAD_DIM
    freq, rotated, low = dim % half, dim < ROT_DIM, dim < half
    plus = (((table == 0) & rotated & (src == freq))
            | ((table == 0) & ~rotated & (src == 2 * half))
            | ((table == 2) & rotated & ~low & (src == half + freq)))
    minus = (table == 1) & low & (src == half + freq)
    sel = plus.astype(F32) - minus.astype(F32)
    return jnp.dot(jnp.tile(compact, (reps, 1)), sel, precision=lax.Precision.HIGHEST)


def kernel(x_prompt, x_sample, cache_k, cache_v, state_pool, state_conv, norm_mix, w_in, q_norm,
           k_norm, attn_sinks, w_pool, pool_scale, w_out, norm_ffn, w_up, conv_w, conv_b, w_down):
    depth = w_in.shape[0]
    B, L, D = x_prompt.shape
    S, n_new, _ = x_sample.shape
    past_len = L
    assert L % PROMPT_TILE == 0 and PROMPT_TILE % CHUNK == 0 and PROMPT_TILE >= WINDOW
    assert w_down.shape[1] % FF_CHUNK == 0
    assert all(m.shape[1] % (BF16_SUBLANES * N_PREP) == 0 for m in (w_in, w_up, w_down))
    assert w_out.shape[1] == N_PREP * HEAD_DIM and LANES == N_KV_HEADS * HEAD_DIM

    tabs_p = _rope_tables(jnp.arange(L))
    tabs_s = _rope_tables(past_len + jnp.arange(n_new), reps=S)
    head_id = jnp.arange(MXU_COLS) // HEAD_DIM
    bd = jnp.where(head_id[:, None] == head_id[None, :], 1.0 / HEAD_DIM, 0.0).astype(BF16)
    q_scale = HEAD_DIM ** -0.5
    n_pool = state_pool.shape[2]
    assert max(POOL_WINDOWS) - 1 <= n_pool <= n_new
    n_conv = state_conv.shape[2]
    past = cache_k.shape[2]

    stacked = dict(
        g_mix=norm_mix, w_in=w_in,
        q_gain=jnp.tile(q_norm * q_scale, (1, N_Q_HEADS)),
        k_gain=jnp.tile(k_norm, (1, N_KV_HEADS)),
        bd=bd, w_pool=w_pool, pool_scale=pool_scale,
        w_out=w_out, g_ffn=norm_ffn, w_up=w_up,
        conv_w=conv_w, conv_b=conv_b, w_down=w_down,
        cache_k=cache_k.reshape(depth, S, past, D_KV), cache_v=cache_v.reshape(depth, S, past, D_KV),
        pool_prev=state_pool, conv_prev=state_conv,
    )

    yp, ys = x_prompt, x_sample
    outs = [[] for _ in range(8)]
    for i in range(depth):
        w = dict(stacked, layer=i, sinks=attn_sinks, n_pool=n_pool)
        (yp, k1, v1, u1, h1), (ys, k2, v2, u2, h2) = _layer_call(yp, tabs_p, ys, tabs_s, w)
        outs[0].append(k1.reshape(B, WINDOW, N_KV_HEADS, HEAD_DIM))
        outs[1].append(v1.reshape(B, WINDOW, N_KV_HEADS, HEAD_DIM))
        outs[2].append(u1)
        outs[3].append(h1)
        outs[4].append(k2.reshape(S, WINDOW, N_KV_HEADS, HEAD_DIM))
        outs[5].append(v2.reshape(S, WINDOW, N_KV_HEADS, HEAD_DIM))
        outs[6].append(u2)
        outs[7].append(h2)
    return (yp, ys) + tuple(jnp.stack(o) for o in outs)
```

```python
import functools
from typing import Any, NamedTuple

import jax
import jax.numpy as jnp
from jax import lax
from jax.experimental import pallas as pl
from jax.experimental.pallas import tpu as pltpu

F32 = jnp.float32
BF16 = jnp.bfloat16

CHUNK = 64
HEAD_DIM = 64
N_Q_HEADS = 8
N_KV_HEADS = 2
Q_PER_KV = N_Q_HEADS // N_KV_HEADS
D_ATTN = N_Q_HEADS * HEAD_DIM
D_KV = N_KV_HEADS * HEAD_DIM
WINDOW = 128
ROT_DIM = 16
ROPE_THETA = 500000.0
POOL_WINDOWS = (2, 4, 8, 16)
POOL_GROUP = 128
D_POOL = POOL_GROUP * len(POOL_WINDOWS)
POOL_HALO = 16
CONV_W = 3
CONV_HALO = 8
EPS = 1e-6
NEG_INF = -1e30
LOG2_E = 1.4426950408889634
NEG_LOG2_E = -LOG2_E

LANES = 128
MXU_COLS = 256
BF16_SUBLANES = 16
KV_PAD = CHUNK
VT_ROWS = D_KV + BF16_SUBLANES

PROMPT_TILE = 512
ATT_GROUP = 4
FF_CHUNK = 256
SAMPLE_FF_SPLIT = 1
DOWN_GROUP = 3
DOWN_LAG = 2
N_PREP = 16
VMEM_LIMIT_BYTES = 56 * 1024 * 1024


def _dot(a, b):
    return jnp.dot(a, b, preferred_element_type=F32)


def _dot_nt(a, b):
    return lax.dot_general(a, b, (((1,), (1,)), ((), ())), preferred_element_type=F32)


def _rms_rows(x, gain):
    ms = jnp.mean(x * x, axis=-1, keepdims=True)
    return x * lax.rsqrt(ms + EPS) * gain


def _head_rms(t, bd, gain):
    ms = _dot((t * t).astype(BF16), bd)
    return t * lax.rsqrt(ms + EPS) * gain


def _rope(t, cos, sin_a, sin_b):
    return (t * cos + pltpu.roll(t, LANES - ROT_DIM // 2, 1) * sin_a
            + pltpu.roll(t, ROT_DIM // 2, 1) * sin_b)


def _rope_coefficients(tab_ref):
    return [tab_ref[:, j * LANES:(j + 1) * LANES] for j in range(3)]


def _mixer_inputs(x, gmix, w_in, qg, kg, bd, cos, sin_a, sin_b):
    xn = _rms_rows(x, gmix).astype(BF16)
    h = _dot(xn, w_in)
    q_parts = []
    for j in range(D_ATTN // MXU_COLS):
        qb = _head_rms(h[:, j * MXU_COLS:(j + 1) * MXU_COLS], bd,
                       qg[:, j * MXU_COLS:(j + 1) * MXU_COLS])
        for l in range(MXU_COLS // LANES):
            q_parts.append(_rope(qb[:, l * LANES:(l + 1) * LANES], cos, sin_a, sin_b))
    q = jnp.concatenate(q_parts, axis=1)
    k = _head_rms(h[:, D_ATTN:D_ATTN + D_KV], bd[:D_KV, :D_KV], kg)
    k = _rope(k, cos, sin_a, sin_b)
    v = h[:, D_ATTN + D_KV:D_ATTN + 2 * D_KV]
    u = h[:, D_ATTN + 2 * D_KV:]
    return q, k, v, u


def _window_sum(a, w):
    s = 1
    while s < w:
        a = a + pltpu.roll(a, s, 0)
        s *= 2
    return a


def _conv_gate(hg, hv, cw_g, cw_v, cb_g, cb_v):
    cg = cb_g + hg[2] * cw_g[0:1] + hg[1] * cw_g[1:2] + hg[0] * cw_g[2:3]
    cv = cb_v + hv[2] * cw_v[0:1] + hv[1] * cw_v[1:2] + hv[0] * cw_v[2:3]
    return ((cg / (1.0 + jnp.exp2(cg * NEG_LOG2_E))) * cv).astype(BF16)


def _pair_heads(qcols):
    lane = lax.broadcasted_iota(jnp.int32, (1, LANES), 1)
    blocks = [qcols[:, b * LANES:(b + 1) * LANES] for b in range(D_ATTN // LANES)]
    per_block = LANES // HEAD_DIM
    out = []
    for hh in range(Q_PER_KV):
        a = blocks[hh // per_block]
        b = blocks[Q_PER_KV // per_block + hh // per_block]
        if hh % per_block == 0:
            out.append(jnp.where(lane < HEAD_DIM, a, pltpu.roll(b, HEAD_DIM, 1)))
        else:
            out.append(jnp.where(lane < HEAD_DIM, pltpu.roll(a, HEAD_DIM, 1), b))
    return jnp.concatenate(out, axis=1)


def _stage_weights(s, win_c, wout_c, wup_c, wdown_c, w_in_s, w_out_s, w_up_s, w_down_s):
    r_in = win_c.shape[0]
    r0 = pl.multiple_of(s * r_in, r_in)
    c = win_c[...]
    w_in_s[pl.ds(r0, r_in), :] = jnp.concatenate(
        [_pair_heads(c[:, :D_ATTN]), c[:, D_ATTN:]], axis=1).astype(BF16)
    w_up_s[pl.ds(r0, r_in), :] = wup_c[...].astype(BF16)
    r_dn = wdown_c.shape[0]
    w_down_s[pl.ds(pl.multiple_of(s * r_dn, r_dn), r_dn), :] = wdown_c[...].astype(BF16)
    dst = jnp.where(s < N_Q_HEADS, (s % Q_PER_KV) * N_KV_HEADS + s // Q_PER_KV, s)
    w_out_s[pl.ds(pl.multiple_of(dst * HEAD_DIM, HEAD_DIM), HEAD_DIM), :] = wout_c[...].astype(BF16)


def _prompt_kernel(sinks_ref, x_ref, tab_ref,
                   gmix_ref, win_c, qg_ref, kg_ref, bd_ref, wpool_ref, pscale_ref, wout_c,
                   gffn_ref, wup_c, cw_ref, cb_ref, wdown_c,
                   xs_ref, tab_s_ref, ck_ref, cv_ref, pprev_ref, cprev_ref,
                   y_ref, klast_ref, vlast_ref, ulast_ref, hlast_ref,
                   ys_ref, kout_ref, vout_ref, uout_ref, hout_ref,
                   w_in_s, w_out_s, w_up_s, w_down_s,
                   kfull, vfull, vt, ufull, hprev, mix_buf, ufull_s, mix_s,
                   *, layer, n_tiles, n_total):
    s = pl.program_id(0)
    sinks_ref = sinks_ref.at[layer]
    gmix_ref, qg_ref, kg_ref, pscale_ref, gffn_ref, cb_ref = (
        r.at[layer:layer + 1] for r in (gmix_ref, qg_ref, kg_ref, pscale_ref, gffn_ref, cb_ref))

    @pl.when(s < N_PREP)
    def _():
        _stage_weights(s, win_c, wout_c, wup_c, wdown_c, w_in_s, w_out_s, w_up_s, w_down_s)

    @pl.when((s >= N_PREP) & (s < N_PREP + n_total))
    def _():
        _prompt_body((s - N_PREP) % n_tiles, sinks_ref, x_ref, tab_ref,
                     gmix_ref, w_in_s, qg_ref, kg_ref, bd_ref, wpool_ref, pscale_ref, w_out_s,
                     gffn_ref, w_up_s, cw_ref, cb_ref, w_down_s,
                     y_ref, klast_ref, vlast_ref, ulast_ref, hlast_ref,
                     kfull, vfull, vt, ufull, hprev, mix_buf)

    @pl.when(s == N_PREP + n_total)
    def _():
        _sample_step(sinks_ref, gmix_ref, w_in_s, qg_ref, kg_ref, bd_ref, wpool_ref, pscale_ref,
                     w_out_s, gffn_ref, w_up_s, cw_ref, cb_ref, w_down_s,
                     _SampleRefs(xs_ref, tab_s_ref, ck_ref, cv_ref, pprev_ref,
                                 cprev_ref, ys_ref, kout_ref, vout_ref, uout_ref, hout_ref,
                                 ufull_s, mix_s))


def _prompt_body(i, sinks_ref, x_ref, tab_ref,
                 gmix_ref, win_ref, qg_ref, kg_ref, bd_ref, wpool_ref, pscale_ref, wout_ref,
                 gffn_ref, wup_ref, cw_ref, cb_ref, wdown_ref,
                 y_ref, klast_ref, vlast_ref, ulast_ref, hlast_ref,
                 kfull, vfull, vt, ufull, hprev, mix_buf):
    T = x_ref.shape[0]
    d_ff = wdown_ref.shape[0]
    n_chunks = T // CHUNK

    @pl.when(i == 0)
    def _():
        kfull[0:WINDOW, :] = jnp.zeros((WINDOW, D_KV), BF16)
        kfull[WINDOW + T:, :] = jnp.zeros((KV_PAD, D_KV), BF16)
        vfull[0:WINDOW, :] = jnp.zeros((WINDOW, D_KV), BF16)
        vfull[WINDOW + T:, :] = jnp.zeros((KV_PAD, D_KV), BF16)
        for c in range(2):
            vt[c, D_KV:, :] = jnp.ones((VT_ROWS - D_KV, vt.shape[2]), BF16)
        ufull[0:POOL_HALO, :] = jnp.zeros((POOL_HALO, D_POOL), F32)
        hprev[...] = jnp.zeros(hprev.shape, F32)

    x = x_ref[...]
    q, k, v, u = _mixer_inputs(x, gmix_ref[...], win_ref[...], qg_ref[...], kg_ref[...],
                               bd_ref[...], *_rope_coefficients(tab_ref))
    klast_ref[...] = k[T - WINDOW:, :]
    vlast_ref[...] = v[T - WINDOW:, :]
    ulast_ref[...] = u[T - ulast_ref.shape[0]:, :]
    kfull[WINDOW:WINDOW + T, :] = k.astype(BF16)
    vfull[WINDOW:WINDOW + T, :] = v.astype(BF16)
    ufull[POOL_HALO:POOL_HALO + T, :] = u

    for c in range(2):
        vrows = vfull[c * CHUNK:c * CHUNK + T + WINDOW, :]
        vt[c, 0:D_KV, :] = vrows.astype(F32).T.astype(BF16)
    band = WINDOW + CHUNK
    n_q = N_Q_HEADS * CHUNK
    lane = lax.broadcasted_iota(jnp.int32, (1, D_ATTN), 1)
    head_of_col = lax.broadcasted_iota(jnp.int32, (1, n_q), 1) // CHUNK
    key_row = lax.broadcasted_iota(jnp.int32, (band, 1), 0)
    q16 = q.astype(BF16)
    q_of_group = [jnp.where((lane // HEAD_DIM) % N_KV_HEADS == g, q16, jnp.zeros_like(q16))
                  for g in range(N_KV_HEADS)]
    sink_row = jnp.full((1, n_q), sinks_ref[0], F32)
    for h8 in range(1, N_Q_HEADS):
        sink_row = jnp.where(head_of_col == h8, sinks_ref[h8], sink_row)
    sink_row = sink_row * LOG2_E
    half = Q_PER_KV * CHUNK
    for j0 in range(0, n_chunks, ATT_GROUP):
        chunk_ids = range(j0, min(j0 + ATT_GROUP, n_chunks))
        scores = []
        for j in chunk_ids:
            r0 = j * CHUNK
            qs = jnp.concatenate(
                [q_of_group[g][r0:r0 + CHUNK, hh * LANES:(hh + 1) * LANES]
                 for g in range(N_KV_HEADS) for hh in range(Q_PER_KV)], axis=0)
            scores.append(_dot_nt(kfull[r0:r0 + band, :], qs))
        probs = []
        for j, s in zip(chunk_ids, scores):
            r0 = j * CHUNK
            if r0 < WINDOW:
                first_valid = jnp.where(i == 0, WINDOW - r0, 0)
                s = jnp.where(key_row >= first_valid, s, NEG_INF)
            m = jnp.maximum(jnp.max(s, axis=0, keepdims=True), sink_row)
            probs.append((jnp.exp2(s - m).astype(BF16), jnp.exp2(sink_row - m)))
        outs_t = []
        for j, (e, sink_e) in zip(chunk_ids, probs):
            vtb = vt[j % 2, :, (j // 2) * LANES:(j // 2) * LANES + band]
            oa = _dot(vtb, e)
            inv_den = 1.0 / (oa[D_KV:D_KV + 1] + sink_e)
            outs_t.append(jnp.concatenate(
                [oa[g * HEAD_DIM:(g + 1) * HEAD_DIM, g * half:(g + 1) * half]
                 * inv_den[:, g * half:(g + 1) * half] for g in range(N_KV_HEADS)],
                axis=0))
        for j, ot in zip(chunk_ids, outs_t):
            r0 = j * CHUNK
            o = ot.T
            for hh in range(Q_PER_KV):
                mix_buf[r0:r0 + CHUNK, hh * LANES:(hh + 1) * LANES] = (
                    o[hh * CHUNK:(hh + 1) * CHUNK].astype(BF16))

    uf = ufull[...]
    pos = i * T + lax.broadcasted_iota(jnp.int32, (T, 1), 0)
    for gi, w in enumerate(POOL_WINDOWS):
        sl = slice(gi * POOL_GROUP, (gi + 1) * POOL_GROUP)
        tsum = _window_sum(uf[:, sl], w)[POOL_HALO:]
        cnt = jnp.minimum(pos + 1, w).astype(F32)
        d = (tsum / cnt - u[:, sl]).astype(BF16)
        pool = _dot(d, wpool_ref[gi].astype(BF16)) * pscale_ref[:, sl]
        mix_buf[:, D_ATTN + gi * POOL_GROUP:D_ATTN + (gi + 1) * POOL_GROUP] = pool.astype(BF16)

    kfull[0:WINDOW, :] = kfull[T:T + WINDOW, :]
    vfull[0:WINDOW, :] = vfull[T:T + WINDOW, :]
    ufull[0:POOL_HALO, :] = u[T - POOL_HALO:, :]

    x1 = x + _dot(mix_buf[...], wout_ref[...])

    xn2 = _rms_rows(x1, gffn_ref[...]).astype(BF16)
    F = FF_CHUNK
    y, k_done, acts = x1, 0, []
    row_in_group = lax.broadcasted_iota(jnp.int32, (1, CONV_HALO, 1), 1)
    for c in range(d_ff // F):
        cols = (slice(c * F, (c + 1) * F), slice(d_ff + c * F, d_ff + (c + 1) * F))
        taps = []
        for half, cs in enumerate(cols):
            h = _dot(xn2, wup_ref[:, cs])
            groups = jnp.concatenate([hprev[:, cs], h], axis=0).reshape(T // CONV_HALO + 1, CONV_HALO, F)
            shifted = []
            for sh in (1, 2):
                r = pltpu.roll(groups, sh, 1)
                shifted.append(jnp.where(row_in_group < sh, r[:-1], r[1:]).reshape(T, F))
            hlast_ref[:, cs] = h[T - hlast_ref.shape[0]:, :]
            hprev[:, cs] = h[T - CONV_HALO:, :]
            taps.append((h, shifted[0], shifted[1]))
        acts.append(_conv_gate(taps[0], taps[1], cw_ref[:, cols[0]], cw_ref[:, cols[1]],
                               cb_ref[:, cols[0]], cb_ref[:, cols[1]]))
        c_done = c - DOWN_LAG
        if c_done >= 0 and (c_done + 1) % DOWN_GROUP == 0:
            ks = slice((c_done + 1 - DOWN_GROUP) * F, (c_done + 1) * F)
            group = jnp.concatenate(acts[c_done + 1 - DOWN_GROUP:c_done + 1], axis=1)
            y = y + _dot(group, wdown_ref[ks, :])
            k_done = (c_done + 1) * F
    y_ref[...] = y + _dot(jnp.concatenate(acts[k_done // F:], axis=1), wdown_ref[k_done:, :])


def _const_spec(shape):
    nd = len(shape)
    return pl.BlockSpec(shape, lambda *_: (0,) * nd)


def _chunk_spec(stacked, layer):
    rows = stacked.shape[1] // N_PREP
    return pl.BlockSpec((None, rows, stacked.shape[2]),
                        lambda s: (layer, jnp.minimum(s, N_PREP - 1), 0))


def _layer_spec(stacked, layer, resident=False):
    nd = stacked.ndim - 1
    return pl.BlockSpec((None,) + stacked.shape[1:], lambda *_: (layer,) + (0,) * nd,
                        pipeline_mode=pl.Buffered(1) if resident else None)


def _weight_specs(w):
    layer = w["layer"]
    return [
        _const_spec(w["g_mix"].shape), _chunk_spec(w["w_in"], layer),
        _const_spec(w["q_gain"].shape), _const_spec(w["k_gain"].shape), _const_spec(w["bd"].shape),
        _layer_spec(w["w_pool"], layer), _const_spec(w["pool_scale"].shape), _chunk_spec(w["w_out"], layer),
        _const_spec(w["g_ffn"].shape), _chunk_spec(w["w_up"], layer),
        _layer_spec(w["conv_w"], layer), _const_spec(w["conv_b"].shape), _chunk_spec(w["w_down"], layer),
    ]


def _weight_args(w):
    return (w["g_mix"], w["w_in"], w["q_gain"], w["k_gain"], w["bd"], w["w_pool"],
            w["pool_scale"], w["w_out"], w["g_ffn"], w["w_up"], w["conv_w"], w["conv_b"], w["w_down"])


def _weight_scratch(w):
    return [pltpu.VMEM(w[name].shape[1:], BF16) for name in ("w_in", "w_out", "w_up", "w_down")]


def _resident_spec(shape):
    nd = len(shape)
    return pl.BlockSpec(shape, lambda *_: (0,) * nd, pipeline_mode=pl.Buffered(1))


def _layer_call(x, tabs, xs, tabs_s, w):
    B, L, D = x.shape
    S, n_new, _ = xs.shape
    M = S * n_new
    T = PROMPT_TILE
    d_ff = w["w_down"].shape[1]
    n_tiles = L // T
    n_total = B * n_tiles
    layer = w["layer"]
    n_pool, n_conv = w["n_pool"], w["conv_prev"].shape[2]
    tile_of = lambda s: jnp.clip(s - N_PREP, 0, n_total - 1)
    tile_spec = pl.BlockSpec((None, T, D), lambda s: (tile_of(s) // n_tiles, tile_of(s) % n_tiles, 0))
    tab_spec = pl.BlockSpec((T, tabs.shape[1]), lambda s: (tile_of(s) % n_tiles, 0))

    def last_spec(rows, cols):
        return pl.BlockSpec((None, rows, cols), lambda s: (tile_of(s) // n_tiles, 0, 0))

    sample_acts = (xs.reshape(M, D), tabs_s)
    sample_state = (w["cache_k"], w["cache_v"], w["pool_prev"], w["conv_prev"])
    in_specs = ([pl.BlockSpec(memory_space=pltpu.SMEM), tile_spec, tab_spec]
                + _weight_specs(w) + [_resident_spec(a.shape) for a in sample_acts]
                + [_layer_spec(a, layer, resident=True) for a in sample_state])
    sample_out = (
        jax.ShapeDtypeStruct((M, D), F32),
        jax.ShapeDtypeStruct((S, WINDOW, D_KV), F32),
        jax.ShapeDtypeStruct((S, WINDOW, D_KV), F32),
        jax.ShapeDtypeStruct((S, n_pool, D_POOL), F32),
        jax.ShapeDtypeStruct((S, n_conv, 2 * d_ff), F32),
    )
    out_shape = (
        jax.ShapeDtypeStruct((B, L, D), F32),
        jax.ShapeDtypeStruct((B, WINDOW, D_KV), F32),
        jax.ShapeDtypeStruct((B, WINDOW, D_KV), F32),
        jax.ShapeDtypeStruct((B, n_pool, D_POOL), F32),
        jax.ShapeDtypeStruct((B, n_conv, 2 * d_ff), F32),
    ) + sample_out
    out_specs = (tile_spec, last_spec(WINDOW, D_KV), last_spec(WINDOW, D_KV),
                 last_spec(n_pool, D_POOL), last_spec(n_conv, 2 * d_ff)
                 ) + tuple(_resident_spec(o.shape) for o in sample_out)
    scratch = _weight_scratch(w) + [
        pltpu.VMEM((WINDOW + T + KV_PAD, D_KV), BF16),
        pltpu.VMEM((WINDOW + T + KV_PAD, D_KV), BF16),
        pltpu.VMEM((2, VT_ROWS, T + WINDOW), BF16),
        pltpu.VMEM((POOL_HALO + T, D_POOL), F32),
        pltpu.VMEM((CONV_HALO, 2 * d_ff), F32),
        pltpu.VMEM((T, D_ATTN + D_POOL), BF16),
        pltpu.VMEM((2 * M, D_POOL), F32),
        pltpu.VMEM((M, D_ATTN + D_POOL), BF16),
    ]
    outs = pl.pallas_call(
        functools.partial(_prompt_kernel, layer=layer, n_tiles=n_tiles, n_total=n_total),
        out_shape=out_shape,
        grid=(N_PREP + n_total + 1,),
        in_specs=in_specs,
        out_specs=out_specs,
        scratch_shapes=scratch,
        name="layer",
        compiler_params=pltpu.CompilerParams(
            dimension_semantics=("arbitrary",),
            vmem_limit_bytes=VMEM_LIMIT_BYTES),
    )(w["sinks"], x, tabs, *_weight_args(w), *sample_acts, *sample_state)
    ys, ko, vo, uo, ho = outs[5:]
    return outs[:5], (ys.reshape(S, n_new, D), ko, vo, uo, ho)


class _SampleRefs(NamedTuple):
    x_ref: Any
    tab_ref: Any
    ck_ref: Any
    cv_ref: Any
    pprev_ref: Any
    cprev_ref: Any
    y_ref: Any
    kout_ref: Any
    vout_ref: Any
    uout_ref: Any
    hout_ref: Any
    ufull: Any
    mix_buf: Any


def _sample_mixer(q, k, v, u, sinks_ref, wpool_ref, pscale_ref, sample):
    ck_ref, cv_ref, mix_buf, ufull = sample.ck_ref, sample.cv_ref, sample.mix_buf, sample.ufull
    M = q.shape[0]
    n_streams, past = ck_ref.shape[0], ck_ref.shape[1]
    n_new = M // n_streams
    n_prev = sample.pprev_ref.shape[1]
    uout_ref = sample.uout_ref
    uout_ref[...] = u.reshape(n_streams, n_new, u.shape[1])[:, n_new - uout_ref.shape[1]:, :]

    sink_cols = [jnp.concatenate(
        [jnp.full((n_new, 1), sinks_ref[g * Q_PER_KV + hh], F32) for hh in range(Q_PER_KV)], axis=0)
        * LOG2_E for g in range(N_KV_HEADS)]
    scores, values = [], []
    for s_ in range(n_streams):
        rows = slice(s_ * n_new, (s_ + 1) * n_new)
        keys = jnp.concatenate([ck_ref[s_], k[rows]], axis=0)
        vals = jnp.concatenate([cv_ref[s_], v[rows]], axis=0)
        sample.kout_ref[s_] = keys[past + n_new - WINDOW:, :]
        sample.vout_ref[s_] = vals[past + n_new - WINDOW:, :]
        keys16 = keys.astype(BF16)
        values.append(vals.astype(BF16))
        for g in range(N_KV_HEADS):
            qs = jnp.concatenate(
                [q[rows, hh * LANES + g * HEAD_DIM:hh * LANES + (g + 1) * HEAD_DIM]
                 for hh in range(Q_PER_KV)], axis=0)
            scores.append(_dot_nt(qs, keys16[:, g * HEAD_DIM:(g + 1) * HEAD_DIM]))
        base = s_ * 2 * n_new
        ufull[base:base + n_new - n_prev, :] = jnp.zeros((n_new - n_prev, ufull.shape[1]), F32)
        ufull[base + n_new - n_prev:base + n_new, :] = sample.pprev_ref[s_]
        ufull[base + n_new:base + 2 * n_new, :] = u[rows]
    probs = []
    for j, sc in enumerate(scores):
        sink_col = sink_cols[j % N_KV_HEADS]
        m = jnp.maximum(jnp.max(sc, axis=-1, keepdims=True), sink_col)
        e = jnp.exp2(sc - m)
        den = jnp.sum(e, axis=-1, keepdims=True) + jnp.exp2(sink_col - m)
        probs.append((e.astype(BF16), 1.0 / den))
    for s_ in range(n_streams):
        rows = slice(s_ * n_new, (s_ + 1) * n_new)
        outs = []
        for g in range(N_KV_HEADS):
            e16, inv_den = probs[s_ * N_KV_HEADS + g]
            outs.append(_dot(e16, values[s_][:, g * HEAD_DIM:(g + 1) * HEAD_DIM]) * inv_den)
        for hh in range(Q_PER_KV):
            pair = jnp.concatenate([o[hh * n_new:(hh + 1) * n_new] for o in outs], axis=1)
            mix_buf[rows, hh * LANES:(hh + 1) * LANES] = pair.astype(BF16)

    uf = ufull[...]
    for gi, w in enumerate(POOL_WINDOWS):
        sl = slice(gi * POOL_GROUP, (gi + 1) * POOL_GROUP)
        ws = _window_sum(uf[:, sl], w)
        tsum = jnp.concatenate(
            [ws[s_ * 2 * n_new + n_new:(s_ + 1) * 2 * n_new] for s_ in range(n_streams)], axis=0)
        d = (tsum / float(w) - u[:, sl]).astype(BF16)
        pool = _dot(d, wpool_ref[gi].astype(BF16)) * pscale_ref[:, sl]
        mix_buf[:, D_ATTN + gi * POOL_GROUP:D_ATTN + (gi + 1) * POOL_GROUP] = pool.astype(BF16)


def _sample_taps(h, cs, sample):
    cprev_ref, hout_ref = sample.cprev_ref, sample.hout_ref
    M, F = h.shape
    n_streams, n_conv = cprev_ref.shape[0], cprev_ref.shape[1]
    n_new = M // n_streams
    t_in_stream = lax.broadcasted_iota(jnp.int32, (M, 1), 0) % n_new
    hout_ref[:, :, cs] = h.reshape(n_streams, n_new, F)[:, n_new - hout_ref.shape[1]:, :]
    prev = [jnp.broadcast_to(cprev_ref[:, j:j + 1, cs], (n_streams, n_new, F)).reshape(M, F)
            for j in range(n_conv)]
    return (h, jnp.where(t_in_stream >= 1, pltpu.roll(h, 1, 0), prev[1]),
            jnp.where(t_in_stream >= 2, pltpu.roll(h, 2, 0),
                      jnp.where(t_in_stream == 0, prev[0], prev[1])))


def _sample_step(sinks_ref, gmix_ref, win_ref, qg_ref, kg_ref, bd_ref, wpool_ref, pscale_ref,
                 wout_ref, gffn_ref, wup_ref, cw_ref, cb_ref, wdown_ref, sample):
    d_ff = wdown_ref.shape[0]
    x = sample.x_ref[...]
    q, k, v, u = _mixer_inputs(x, gmix_ref[...], win_ref[...], qg_ref[...], kg_ref[...],
                               bd_ref[...], *_rope_coefficients(sample.tab_ref))
    _sample_mixer(q.astype(BF16), k, v, u, sinks_ref, wpool_ref, pscale_ref, sample)
    x1 = x + _dot(sample.mix_buf[...], wout_ref[...])
    xn2 = _rms_rows(x1, gffn_ref[...]).astype(BF16)
    acc = x1
    F = d_ff // SAMPLE_FF_SPLIT
    for c in range(SAMPLE_FF_SPLIT):
        cg = slice(c * F, (c + 1) * F)
        cv = slice(d_ff + c * F, d_ff + (c + 1) * F)
        taps_g = _sample_taps(_dot(xn2, wup_ref[:, cg]), cg, sample)
        taps_v = _sample_taps(_dot(xn2, wup_ref[:, cv]), cv, sample)
        act = _conv_gate(taps_g, taps_v, cw_ref[:, cg], cw_ref[:, cv], cb_ref[:, cg], cb_ref[:, cv])
        acc = acc + _dot(act, wdown_ref[cg, :])
    sample.y_ref[...] = acc


def _rope_tables(pos, reps=1):
    half = ROT_DIM // 2
    inv = ROPE_THETA ** (-jnp.arange(0, ROT_DIM, 2, dtype=F32) / ROT_DIM)
    ang = pos.astype(F32)[:, None] * inv[None, :]
    compact = jnp.concatenate([jnp.cos(ang), jnp.sin(ang), jnp.ones((pos.shape[0], 1), F32)], axis=1)
    src = jnp.arange(2 * half + 1)[:, None]
    lane = jnp.arange(3 * LANES)[None, :]
    table, dim = lane // LANES, lane % HEAD_DIM
    freq, rotated, low = dim % half, dim < ROT_DIM, dim < half
    plus = (((table == 0) & rotated & (src == freq))
            | ((table == 0) & ~rotated & (src == 2 * half))
            | ((table == 2) & rotated & ~low & (src == half + freq)))
    minus = (table == 1) & low & (src == half + freq)
    sel = plus.astype(F32) - minus.astype(F32)
    return jnp.dot(jnp.tile(compact, (reps, 1)), sel, precision=lax.Precision.HIGHEST)


def kernel(x_prompt, x_sample, cache_k, cache_v, state_pool, state_conv, norm_mix, w_in, q_norm,
           k_norm, attn_sinks, w_pool, pool_scale, w_out, norm_ffn, w_up, conv_w, conv_b, w_down):
    depth = w_in.shape[0]
    B, L, D = x_prompt.shape
    S, n_new, _ = x_sample.shape
    past_len = L
    assert L % PROMPT_TILE == 0 and PROMPT_TILE % CHUNK == 0 and PROMPT_TILE >= WINDOW
    assert w_down.shape[1] % FF_CHUNK == 0
    assert all(m.shape[1] % (BF16_SUBLANES * N_PREP) == 0 for m in (w_in, w_up, w_down))
    assert w_out.shape[1] == N_PREP * HEAD_DIM and LANES == N_KV_HEADS * HEAD_DIM

    tabs_p = _rope_tables(jnp.arange(L))
    tabs_s = _rope_tables(past_len + jnp.arange(n_new), reps=S)
    head_id = jnp.arange(MXU_COLS) // HEAD_DIM
    bd = jnp.where(head_id[:, None] == head_id[None, :], 1.0 / HEAD_DIM, 0.0).astype(BF16)
    q_scale = HEAD_DIM ** -0.5 * LOG2_E
    n_pool = state_pool.shape[2]
    assert max(POOL_WINDOWS) - 1 <= n_pool <= n_new
    n_conv = state_conv.shape[2]
    past = cache_k.shape[2]

    stacked = dict(
        g_mix=norm_mix, w_in=w_in,
        q_gain=jnp.tile(q_norm * q_scale, (1, N_Q_HEADS)),
        k_gain=jnp.tile(k_norm, (1, N_KV_HEADS)),
        bd=bd, w_pool=w_pool, pool_scale=pool_scale,
        w_out=w_out, g_ffn=norm_ffn, w_up=w_up,
        conv_w=conv_w, conv_b=conv_b, w_down=w_down,
        cache_k=cache_k.reshape(depth, S, past, D_KV), cache_v=cache_v.reshape(depth, S, past, D_KV),
        pool_prev=state_pool, conv_prev=state_conv,
    )

    yp, ys = x_prompt, x_sample
    outs = [[] for _ in range(8)]
    for i in range(depth):
        w = dict(stacked, layer=i, sinks=attn_sinks, n_pool=n_pool)
        (yp, k1, v1, u1, h1), (ys, k2, v2, u2, h2) = _layer_call(yp, tabs_p, ys, tabs_s, w)
        outs[0].append(k1.reshape(B, WINDOW, N_KV_HEADS, HEAD_DIM))
        outs[1].append(v1.reshape(B, WINDOW, N_KV_HEADS, HEAD_DIM))
        outs[2].append(u1)
        outs[3].append(h1)
        outs[4].append(k2.reshape(S, WINDOW, N_KV_HEADS, HEAD_DIM))
        outs[5].append(v2.reshape(S, WINDOW, N_KV_HEADS, HEAD_DIM))
        outs[6].append(u2)
        outs[7].append(h2)
    return (yp, ys) + tuple(jnp.stack(o) for o in outs)
```

```python
import functools
from typing import Any, NamedTuple

import jax
import jax.numpy as jnp
from jax import lax
from jax.experimental import pallas as pl
from jax.experimental.pallas import tpu as pltpu

F32 = jnp.float32
BF16 = jnp.bfloat16

CHUNK = 64
HEAD_DIM = 64
N_Q_HEADS = 8
N_KV_HEADS = 2
Q_PER_KV = N_Q_HEADS // N_KV_HEADS
D_ATTN = N_Q_HEADS * HEAD_DIM
D_KV = N_KV_HEADS * HEAD_DIM
WINDOW = 128
ROT_DIM = 16
ROPE_THETA = 500000.0
POOL_WINDOWS = (2, 4, 8, 16)
POOL_GROUP = 128
D_POOL = POOL_GROUP * len(POOL_WINDOWS)
POOL_HALO = 16
CONV_W = 3
CONV_HALO = 8
EPS = 1e-6
NEG_INF = -1e30
LOG2_E = 1.4426950408889634
NEG_LOG2_E = -LOG2_E

LANES = 128
MXU_COLS = 256
BF16_SUBLANES = 16
KV_PAD = CHUNK
VT_ROWS = D_KV + BF16_SUBLANES

PROMPT_TILE = 512
ATT_GROUP = 4
FF_CHUNK = 256
SAMPLE_FF_SPLIT = 1
DOWN_GROUP = 3
DOWN_LAG = 2
N_PREP = 16
VMEM_LIMIT_BYTES = 56 * 1024 * 1024


def _dot(a, b):
    return jnp.dot(a, b, preferred_element_type=F32)


def _dot_nt(a, b):
    return lax.dot_general(a, b, (((1,), (1,)), ((), ())), preferred_element_type=F32)


def _rms_rows(x, gain):
    ms = jnp.mean(x * x, axis=-1, keepdims=True)
    return x * lax.rsqrt(ms + EPS) * gain


def _head_rms(t, bd, gain):
    ms = _dot((t * t).astype(BF16), bd)
    return t * lax.rsqrt(ms + EPS) * gain


def _rope(t, cos, sin_a, sin_b):
    return (t * cos + pltpu.roll(t, LANES - ROT_DIM // 2, 1) * sin_a
            + pltpu.roll(t, ROT_DIM // 2, 1) * sin_b)


def _rope_coefficients(tab_ref):
    return [tab_ref[:, j * LANES:(j + 1) * LANES] for j in range(3)]


def _mixer_inputs(x, gmix, w_in, qg, kg, bd, cos, sin_a, sin_b):
    xn = _rms_rows(x, gmix).astype(BF16)
    h = _dot(xn, w_in)
    q_parts = []
    for j in range(D_ATTN // MXU_COLS):
        qb = _head_rms(h[:, j * MXU_COLS:(j + 1) * MXU_COLS], bd,
                       qg[:, j * MXU_COLS:(j + 1) * MXU_COLS])
        for l in range(MXU_COLS // LANES):
            q_parts.append(_rope(qb[:, l * LANES:(l + 1) * LANES], cos, sin_a, sin_b))
    q = jnp.concatenate(q_parts, axis=1)
    k = _head_rms(h[:, D_ATTN:D_ATTN + D_KV], bd[:D_KV, :D_KV], kg)
    k = _rope(k, cos, sin_a, sin_b)
    v = h[:, D_ATTN + D_KV:D_ATTN + 2 * D_KV]
    u = h[:, D_ATTN + 2 * D_KV:]
    return q, k, v, u


def _window_sum(a, w):
    s = 1
    while s < w:
        a = a + pltpu.roll(a, s, 0)
        s *= 2
    return a


def _conv_gate(hg, hv, cw_g, cw_v, cb_g, cb_v):
    cg = cb_g + hg[2] * cw_g[0:1] + hg[1] * cw_g[1:2] + hg[0] * cw_g[2:3]
    cv = cb_v + hv[2] * cw_v[0:1] + hv[1] * cw_v[1:2] + hv[0] * cw_v[2:3]
    return ((cg / (1.0 + jnp.exp2(cg * NEG_LOG2_E))) * cv).astype(BF16)


def _pair_heads(qcols):
    lane = lax.broadcasted_iota(jnp.int32, (1, LANES), 1)
    blocks = [qcols[:, b * LANES:(b + 1) * LANES] for b in range(D_ATTN // LANES)]
    per_block = LANES // HEAD_DIM
    out = []
    for hh in range(Q_PER_KV):
        a = blocks[hh // per_block]
        b = blocks[Q_PER_KV // per_block + hh // per_block]
        if hh % per_block == 0:
            out.append(jnp.where(lane < HEAD_DIM, a, pltpu.roll(b, HEAD_DIM, 1)))
        else:
            out.append(jnp.where(lane < HEAD_DIM, pltpu.roll(a, HEAD_DIM, 1), b))
    return jnp.concatenate(out, axis=1)


def _stage_weights(s, win_c, wout_c, wup_c, wdown_c, w_in_s, w_out_s, w_up_s, w_down_s):
    r_in = win_c.shape[0]
    r0 = pl.multiple_of(s * r_in, r_in)
    c = win_c[...]
    w_in_s[pl.ds(r0, r_in), :] = jnp.concatenate(
        [_pair_heads(c[:, :D_ATTN]), c[:, D_ATTN:]], axis=1).astype(BF16)
    w_up_s[pl.ds(r0, r_in), :] = wup_c[...].astype(BF16)
    r_dn = wdown_c.shape[0]
    w_down_s[pl.ds(pl.multiple_of(s * r_dn, r_dn), r_dn), :] = wdown_c[...].astype(BF16)
    dst = jnp.where(s < N_Q_HEADS, (s % Q_PER_KV) * N_KV_HEADS + s // Q_PER_KV, s)
    w_out_s[pl.ds(pl.multiple_of(dst * HEAD_DIM, HEAD_DIM), HEAD_DIM), :] = wout_c[...].astype(BF16)


def _prompt_kernel(sinks_ref, x_ref, tab_ref,
                   gmix_ref, win_c, qg_ref, kg_ref, bd_ref, wpool_ref, pscale_ref, wout_c,
                   gffn_ref, wup_c, cw_ref, cb_ref, wdown_c,
                   xs_ref, tab_s_ref, ck_ref, cv_ref, pprev_ref, cprev_ref,
                   y_ref, klast_ref, vlast_ref, ulast_ref, hlast_ref,
                   ys_ref, kout_ref, vout_ref, uout_ref, hout_ref,
                   w_in_s, w_out_s, w_up_s, w_down_s,
                   kfull, vfull, vt, ufull, hprev, mix_buf, ufull_s, mix_s,
                   *, layer, n_tiles, n_total):
    s = pl.program_id(0)
    sinks_ref = sinks_ref.at[layer]
    gmix_ref, qg_ref, kg_ref, pscale_ref, gffn_ref, cb_ref = (
        r.at[layer:layer + 1] for r in (gmix_ref, qg_ref, kg_ref, pscale_ref, gffn_ref, cb_ref))

    @pl.when(s < N_PREP)
    def _():
        _stage_weights(s, win_c, wout_c, wup_c, wdown_c, w_in_s, w_out_s, w_up_s, w_down_s)

    @pl.when((s >= N_PREP) & (s < N_PREP + n_total))
    def _():
        _prompt_body((s - N_PREP) % n_tiles, sinks_ref, x_ref, tab_ref,
                     gmix_ref, w_in_s, qg_ref, kg_ref, bd_ref, wpool_ref, pscale_ref, w_out_s,
                     gffn_ref, w_up_s, cw_ref, cb_ref, w_down_s,
                     y_ref, klast_ref, vlast_ref, ulast_ref, hlast_ref,
                     kfull, vfull, vt, ufull, hprev, mix_buf)

    @pl.when(s == N_PREP + n_total)
    def _():
        _sample_step(sinks_ref, gmix_ref, w_in_s, qg_ref, kg_ref, bd_ref, wpool_ref, pscale_ref,
                     w_out_s, gffn_ref, w_up_s, cw_ref, cb_ref, w_down_s,
                     _SampleRefs(xs_ref, tab_s_ref, ck_ref, cv_ref, pprev_ref,
                                 cprev_ref, ys_ref, kout_ref, vout_ref, uout_ref, hout_ref,
                                 ufull_s, mix_s))


def _prompt_body(i, sinks_ref, x_ref, tab_ref,
                 gmix_ref, win_ref, qg_ref, kg_ref, bd_ref, wpool_ref, pscale_ref, wout_ref,
                 gffn_ref, wup_ref, cw_ref, cb_ref, wdown_ref,
                 y_ref, klast_ref, vlast_ref, ulast_ref, hlast_ref,
                 kfull, vfull, vt, ufull, hprev, mix_buf):
    T = x_ref.shape[0]
    d_ff = wdown_ref.shape[0]
    n_chunks = T // CHUNK

    @pl.when(i == 0)
    def _():
        kfull[0:WINDOW, :] = jnp.zeros((WINDOW, D_KV), BF16)
        kfull[WINDOW + T:, :] = jnp.zeros((KV_PAD, D_KV), BF16)
        vfull[0:WINDOW, :] = jnp.zeros((WINDOW, D_KV), BF16)
        vfull[WINDOW + T:, :] = jnp.zeros((KV_PAD, D_KV), BF16)
        for c in range(2):
            vt[c, D_KV:, :] = jnp.ones((VT_ROWS - D_KV, vt.shape[2]), BF16)
        ufull[0:POOL_HALO, :] = jnp.zeros((POOL_HALO, D_POOL), F32)
        hprev[...] = jnp.zeros(hprev.shape, F32)

    x = x_ref[...]
    q, k, v, u = _mixer_inputs(x, gmix_ref[...], win_ref[...], qg_ref[...], kg_ref[...],
                               bd_ref[...], *_rope_coefficients(tab_ref))
    klast_ref[...] = k[T - WINDOW:, :].T
    vlast_ref[...] = v[T - WINDOW:, :].T
    ulast_ref[...] = u[T - ulast_ref.shape[0]:, :]
    kfull[WINDOW:WINDOW + T, :] = k.astype(BF16)
    vfull[WINDOW:WINDOW + T, :] = v.astype(BF16)
    ufull[POOL_HALO:POOL_HALO + T, :] = u

    for c in range(2):
        vrows = vfull[c * CHUNK:c * CHUNK + T + WINDOW, :]
        vt[c, 0:D_KV, :] = vrows.astype(F32).T.astype(BF16)
    band = WINDOW + CHUNK
    n_q = N_Q_HEADS * CHUNK
    lane = lax.broadcasted_iota(jnp.int32, (1, D_ATTN), 1)
    head_of_col = lax.broadcasted_iota(jnp.int32, (1, n_q), 1) // CHUNK
    key_row = lax.broadcasted_iota(jnp.int32, (band, 1), 0)
    q16 = q.astype(BF16)
    q_of_group = [jnp.where((lane // HEAD_DIM) % N_KV_HEADS == g, q16, jnp.zeros_like(q16))
                  for g in range(N_KV_HEADS)]
    sink_row = jnp.full((1, n_q), sinks_ref[0], F32)
    for h8 in range(1, N_Q_HEADS):
        sink_row = jnp.where(head_of_col == h8, sinks_ref[h8], sink_row)
    sink_row = sink_row * LOG2_E
    half = Q_PER_KV * CHUNK
    for j0 in range(0, n_chunks, ATT_GROUP):
        chunk_ids = range(j0, min(j0 + ATT_GROUP, n_chunks))
        scores = []
        for j in chunk_ids:
            r0 = j * CHUNK
            qs = jnp.concatenate(
                [q_of_group[g][r0:r0 + CHUNK, hh * LANES:(hh + 1) * LANES]
                 for g in range(N_KV_HEADS) for hh in range(Q_PER_KV)], axis=0)
            scores.append(_dot_nt(kfull[r0:r0 + band, :], qs))
        probs = []
        for j, s in zip(chunk_ids, scores):
            r0 = j * CHUNK
            if r0 < WINDOW:
                first_valid = jnp.where(i == 0, WINDOW - r0, 0)
                s = jnp.where(key_row >= first_valid, s, NEG_INF)
            m = jnp.maximum(jnp.max(s, axis=0, keepdims=True), sink_row)
            probs.append((jnp.exp2(s - m).astype(BF16), jnp.exp2(sink_row - m)))
        outs_t = []
        for j, (e, sink_e) in zip(chunk_ids, probs):
            vtb = vt[j % 2, :, (j // 2) * LANES:(j // 2) * LANES + band]
            oa = _dot(vtb, e)
            inv_den = 1.0 / (oa[D_KV:D_KV + 1] + sink_e)
            outs_t.append(jnp.concatenate(
                [oa[g * HEAD_DIM:(g + 1) * HEAD_DIM, g * half:(g + 1) * half]
                 * inv_den[:, g * half:(g + 1) * half] for g in range(N_KV_HEADS)],
                axis=0))
        for j, ot in zip(chunk_ids, outs_t):
            r0 = j * CHUNK
            o = ot.T
            for hh in range(Q_PER_KV):
                mix_buf[r0:r0 + CHUNK, hh * LANES:(hh + 1) * LANES] = (
                    o[hh * CHUNK:(hh + 1) * CHUNK].astype(BF16))

    uf = ufull[...]
    pos = i * T + lax.broadcasted_iota(jnp.int32, (T, 1), 0)
    for gi, w in enumerate(POOL_WINDOWS):
        sl = slice(gi * POOL_GROUP, (gi + 1) * POOL_GROUP)
        tsum = _window_sum(uf[:, sl], w)[POOL_HALO:]
        cnt = jnp.minimum(pos + 1, w).astype(F32)
        d = (tsum / cnt - u[:, sl]).astype(BF16)
        pool = _dot(d, wpool_ref[gi].astype(BF16)) * pscale_ref[:, sl]
        mix_buf[:, D_ATTN + gi * POOL_GROUP:D_ATTN + (gi + 1) * POOL_GROUP] = pool.astype(BF16)

    kfull[0:WINDOW, :] = kfull[T:T + WINDOW, :]
    vfull[0:WINDOW, :] = vfull[T:T + WINDOW, :]
    ufull[0:POOL_HALO, :] = u[T - POOL_HALO:, :]

    x1 = x + _dot(mix_buf[...], wout_ref[...])

    xn2 = _rms_rows(x1, gffn_ref[...]).astype(BF16)
    F = FF_CHUNK
    y, k_done, acts = x1, 0, []
    row_in_group = lax.broadcasted_iota(jnp.int32, (1, CONV_HALO, 1), 1)
    for c in range(d_ff // F):
        cols = (slice(c * F, (c + 1) * F), slice(d_ff + c * F, d_ff + (c + 1) * F))
        taps = []
        for half, cs in enumerate(cols):
            h = _dot(xn2, wup_ref[:, cs])
            groups = jnp.concatenate([hprev[:, cs], h], axis=0).reshape(T // CONV_HALO + 1, CONV_HALO, F)
            shifted = []
            for sh in (1, 2):
                r = pltpu.roll(groups, sh, 1)
                shifted.append(jnp.where(row_in_group < sh, r[:-1], r[1:]).reshape(T, F))
            hlast_ref[:, cs] = h[T - hlast_ref.shape[0]:, :]
            hprev[:, cs] = h[T - CONV_HALO:, :]
            taps.append((h, shifted[0], shifted[1]))
        acts.append(_conv_gate(taps[0], taps[1], cw_ref[:, cols[0]], cw_ref[:, cols[1]],
                               cb_ref[:, cols[0]], cb_ref[:, cols[1]]))
        c_done = c - DOWN_LAG
        if c_done >= 0 and (c_done + 1) % DOWN_GROUP == 0:
            ks = slice((c_done + 1 - DOWN_GROUP) * F, (c_done + 1) * F)
            group = jnp.concatenate(acts[c_done + 1 - DOWN_GROUP:c_done + 1], axis=1)
            y = y + _dot(group, wdown_ref[ks, :])
            k_done = (c_done + 1) * F
    y_ref[...] = y + _dot(jnp.concatenate(acts[k_done // F:], axis=1), wdown_ref[k_done:, :])


def _const_spec(shape):
    nd = len(shape)
    return pl.BlockSpec(shape, lambda *_: (0,) * nd)


def _chunk_spec(stacked, layer):
    rows = stacked.shape[1] // N_PREP
    return pl.BlockSpec((None, rows, stacked.shape[2]),
                        lambda s: (layer, jnp.minimum(s, N_PREP - 1), 0))


def _layer_spec(stacked, layer, resident=False):
    nd = stacked.ndim - 1
    return pl.BlockSpec((None,) + stacked.shape[1:], lambda *_: (layer,) + (0,) * nd,
                        pipeline_mode=pl.Buffered(1) if resident else None)


def _weight_specs(w):
    layer = w["layer"]
    return [
        _const_spec(w["g_mix"].shape), _chunk_spec(w["w_in"], layer),
        _const_spec(w["q_gain"].shape), _const_spec(w["k_gain"].shape), _const_spec(w["bd"].shape),
        _layer_spec(w["w_pool"], layer), _const_spec(w["pool_scale"].shape), _chunk_spec(w["w_out"], layer),
        _const_spec(w["g_ffn"].shape), _chunk_spec(w["w_up"], layer),
        _layer_spec(w["conv_w"], layer), _const_spec(w["conv_b"].shape), _chunk_spec(w["w_down"], layer),
    ]


def _weight_args(w):
    return (w["g_mix"], w["w_in"], w["q_gain"], w["k_gain"], w["bd"], w["w_pool"],
            w["pool_scale"], w["w_out"], w["g_ffn"], w["w_up"], w["conv_w"], w["conv_b"], w["w_down"])


def _weight_scratch(w):
    return [pltpu.VMEM(w[name].shape[1:], BF16) for name in ("w_in", "w_out", "w_up", "w_down")]


def _resident_spec(shape):
    nd = len(shape)
    return pl.BlockSpec(shape, lambda *_: (0,) * nd, pipeline_mode=pl.Buffered(1))


def _layer_call(x, tabs, xs, tabs_s, w):
    B, L, D = x.shape
    S, n_new, _ = xs.shape
    M = S * n_new
    T = PROMPT_TILE
    d_ff = w["w_down"].shape[1]
    n_tiles = L // T
    n_total = B * n_tiles
    layer = w["layer"]
    n_pool, n_conv = w["n_pool"], w["conv_prev"].shape[2]
    tile_of = lambda s: jnp.clip(s - N_PREP, 0, n_total - 1)
    tile_spec = pl.BlockSpec((None, T, D), lambda s: (tile_of(s) // n_tiles, tile_of(s) % n_tiles, 0))
    tab_spec = pl.BlockSpec((T, tabs.shape[1]), lambda s: (tile_of(s) % n_tiles, 0))

    def last_spec(rows, cols):
        return pl.BlockSpec((None, rows, cols), lambda s: (tile_of(s) // n_tiles, 0, 0))

    sample_acts = (xs.reshape(M, D), tabs_s)
    sample_state = (w["cache_k"], w["cache_v"], w["pool_prev"], w["conv_prev"])
    in_specs = ([pl.BlockSpec(memory_space=pltpu.SMEM), tile_spec, tab_spec]
                + _weight_specs(w) + [_resident_spec(a.shape) for a in sample_acts]
                + [_layer_spec(a, layer, resident=True) for a in sample_state])
    sample_out = (
        jax.ShapeDtypeStruct((M, D), F32),
        jax.ShapeDtypeStruct((S, D_KV, WINDOW), F32),
        jax.ShapeDtypeStruct((S, D_KV, WINDOW), F32),
        jax.ShapeDtypeStruct((S, n_pool, D_POOL), F32),
        jax.ShapeDtypeStruct((S, n_conv, 2 * d_ff), F32),
    )
    out_shape = (
        jax.ShapeDtypeStruct((B, L, D), F32),
        jax.ShapeDtypeStruct((B, D_KV, WINDOW), F32),
        jax.ShapeDtypeStruct((B, D_KV, WINDOW), F32),
        jax.ShapeDtypeStruct((B, n_pool, D_POOL), F32),
        jax.ShapeDtypeStruct((B, n_conv, 2 * d_ff), F32),
    ) + sample_out
    out_specs = (tile_spec, last_spec(D_KV, WINDOW), last_spec(D_KV, WINDOW),
                 last_spec(n_pool, D_POOL), last_spec(n_conv, 2 * d_ff)
                 ) + tuple(_resident_spec(o.shape) for o in sample_out)
    scratch = _weight_scratch(w) + [
        pltpu.VMEM((WINDOW + T + KV_PAD, D_KV), BF16),
        pltpu.VMEM((WINDOW + T + KV_PAD, D_KV), BF16),
        pltpu.VMEM((2, VT_ROWS, T + WINDOW), BF16),
        pltpu.VMEM((POOL_HALO + T, D_POOL), F32),
        pltpu.VMEM((CONV_HALO, 2 * d_ff), F32),
        pltpu.VMEM((T, D_ATTN + D_POOL), BF16),
        pltpu.VMEM((2 * M, D_POOL), F32),
        pltpu.VMEM((M, D_ATTN + D_POOL), BF16),
    ]
    outs = pl.pallas_call(
        functools.partial(_prompt_kernel, layer=layer, n_tiles=n_tiles, n_total=n_total),
        out_shape=out_shape,
        grid=(N_PREP + n_total + 1,),
        in_specs=in_specs,
        out_specs=out_specs,
        scratch_shapes=scratch,
        name="layer",
        compiler_params=pltpu.CompilerParams(
            dimension_semantics=("arbitrary",),
            vmem_limit_bytes=VMEM_LIMIT_BYTES),
    )(w["sinks"], x, tabs, *_weight_args(w), *sample_acts, *sample_state)
    ys, ko, vo, uo, ho = outs[5:]
    return outs[:5], (ys.reshape(S, n_new, D), ko, vo, uo, ho)


class _SampleRefs(NamedTuple):
    x_ref: Any
    tab_ref: Any
    ck_ref: Any
    cv_ref: Any
    pprev_ref: Any
    cprev_ref: Any
    y_ref: Any
    kout_ref: Any
    vout_ref: Any
    uout_ref: Any
    hout_ref: Any
    ufull: Any
    mix_buf: Any


def _sample_mixer(q, k, v, u, sinks_ref, wpool_ref, pscale_ref, sample):
    ck_ref, cv_ref, mix_buf, ufull = sample.ck_ref, sample.cv_ref, sample.mix_buf, sample.ufull
    M = q.shape[0]
    n_streams, past = ck_ref.shape[0], ck_ref.shape[2]
    n_new = M // n_streams
    n_prev = sample.pprev_ref.shape[1]
    uout_ref = sample.uout_ref
    uout_ref[...] = u.reshape(n_streams, n_new, u.shape[1])[:, n_new - uout_ref.shape[1]:, :]

    sink_cols = [jnp.concatenate(
        [jnp.full((n_new, 1), sinks_ref[g * Q_PER_KV + hh], F32) for hh in range(Q_PER_KV)], axis=0)
        * LOG2_E for g in range(N_KV_HEADS)]
    scores, values = [], []
    for s_ in range(n_streams):
        rows = slice(s_ * n_new, (s_ + 1) * n_new)
        keys = jnp.concatenate([ck_ref[s_].T, k[rows]], axis=0)
        vals = jnp.concatenate([cv_ref[s_].T, v[rows]], axis=0)
        sample.kout_ref[s_] = keys[past + n_new - WINDOW:, :].T
        sample.vout_ref[s_] = vals[past + n_new - WINDOW:, :].T
        keys16 = keys.astype(BF16)
        values.append(vals.astype(BF16))
        for g in range(N_KV_HEADS):
            qs = jnp.concatenate(
                [q[rows, hh * LANES + g * HEAD_DIM:hh * LANES + (g + 1) * HEAD_DIM]
                 for hh in range(Q_PER_KV)], axis=0)
            scores.append(_dot_nt(qs, keys16[:, g * HEAD_DIM:(g + 1) * HEAD_DIM]))
        base = s_ * 2 * n_new
        ufull[base:base + n_new - n_prev, :] = jnp.zeros((n_new - n_prev, ufull.shape[1]), F32)
        ufull[base + n_new - n_prev:base + n_new, :] = sample.pprev_ref[s_]
        ufull[base + n_new:base + 2 * n_new, :] = u[rows]
    probs = []
    for j, sc in enumerate(scores):
        sink_col = sink_cols[j % N_KV_HEADS]
        m = jnp.maximum(jnp.max(sc, axis=-1, keepdims=True), sink_col)
        e = jnp.exp2(sc - m)
        den = jnp.sum(e, axis=-1, keepdims=True) + jnp.exp2(sink_col - m)
        probs.append((e.astype(BF16), 1.0 / den))
    for s_ in range(n_streams):
        rows = slice(s_ * n_new, (s_ + 1) * n_new)
        outs = []
        for g in range(N_KV_HEADS):
            e16, inv_den = probs[s_ * N_KV_HEADS + g]
            outs.append(_dot(e16, values[s_][:, g * HEAD_DIM:(g + 1) * HEAD_DIM]) * inv_den)
        for hh in range(Q_PER_KV):
            pair = jnp.concatenate([o[hh * n_new:(hh + 1) * n_new] for o in outs], axis=1)
            mix_buf[rows, hh * LANES:(hh + 1) * LANES] = pair.astype(BF16)

    uf = ufull[...]
    for gi, w in enumerate(POOL_WINDOWS):
        sl = slice(gi * POOL_GROUP, (gi + 1) * POOL_GROUP)
        ws = _window_sum(uf[:, sl], w)
        tsum = jnp.concatenate(
            [ws[s_ * 2 * n_new + n_new:(s_ + 1) * 2 * n_new] for s_ in range(n_streams)], axis=0)
        d = (tsum / float(w) - u[:, sl]).astype(BF16)
        pool = _dot(d, wpool_ref[gi].astype(BF16)) * pscale_ref[:, sl]
        mix_buf[:, D_ATTN + gi * POOL_GROUP:D_ATTN + (gi + 1) * POOL_GROUP] = pool.astype(BF16)


def _sample_taps(h, cs, sample):
    cprev_ref, hout_ref = sample.cprev_ref, sample.hout_ref
    M, F = h.shape
    n_streams, n_conv = cprev_ref.shape[0], cprev_ref.shape[1]
    n_new = M // n_streams
    t_in_stream = lax.broadcasted_iota(jnp.int32, (M, 1), 0) % n_new
    hout_ref[:, :, cs] = h.reshape(n_streams, n_new, F)[:, n_new - hout_ref.shape[1]:, :]
    prev = [jnp.broadcast_to(cprev_ref[:, j:j + 1, cs], (n_streams, n_new, F)).reshape(M, F)
            for j in range(n_conv)]
    return (h, jnp.where(t_in_stream >= 1, pltpu.roll(h, 1, 0), prev[1]),
            jnp.where(t_in_stream >= 2, pltpu.roll(h, 2, 0),
                      jnp.where(t_in_stream == 0, prev[0], prev[1])))


def _sample_step(sinks_ref, gmix_ref, win_ref, qg_ref, kg_ref, bd_ref, wpool_ref, pscale_ref,
                 wout_ref, gffn_ref, wup_ref, cw_ref, cb_ref, wdown_ref, sample):
    d_ff = wdown_ref.shape[0]
    x = sample.x_ref[...]
    q, k, v, u = _mixer_inputs(x, gmix_ref[...], win_ref[...], qg_ref[...], kg_ref[...],
                               bd_ref[...], *_rope_coefficients(sample.tab_ref))
    _sample_mixer(q.astype(BF16), k, v, u, sinks_ref, wpool_ref, pscale_ref, sample)
    x1 = x + _dot(sample.mix_buf[...], wout_ref[...])
    xn2 = _rms_rows(x1, gffn_ref[...]).astype(BF16)
    acc = x1
    F = d_ff // SAMPLE_FF_SPLIT
    for c in range(SAMPLE_FF_SPLIT):
        cg = slice(c * F, (c + 1) * F)
        cv = slice(d_ff + c * F, d_ff + (c + 1) * F)
        taps_g = _sample_taps(_dot(xn2, wup_ref[:, cg]), cg, sample)
        taps_v = _sample_taps(_dot(xn2, wup_ref[:, cv]), cv, sample)
        act = _conv_gate(taps_g, taps_v, cw_ref[:, cg], cw_ref[:, cv], cb_ref[:, cg], cb_ref[:, cv])
        acc = acc + _dot(act, wdown_ref[cg, :])
    sample.y_ref[...] = acc


def _rope_tables(pos, reps=1):
    half = ROT_DIM // 2
    inv = ROPE_THETA ** (-jnp.arange(0, ROT_DIM, 2, dtype=F32) / ROT_DIM)
    ang = pos.astype(F32)[:, None] * inv[None, :]
    compact = jnp.concatenate([jnp.cos(ang), jnp.sin(ang), jnp.ones((pos.shape[0], 1), F32)], axis=1)
    src = jnp.arange(2 * half + 1)[:, None]
    lane = jnp.arange(3 * LANES)[None, :]
    table, dim = lane // LANES, lane % HEAD_DIM
    freq, rotated, low = dim % half, dim < ROT_DIM, dim < half
    plus = (((table == 0) & rotated & (src == freq))
            | ((table == 0) & ~rotated & (src == 2 * half))
            | ((table == 2) & rotated & ~low & (src == half + freq)))
    minus = (table == 1) & low & (src == half + freq)
    sel = plus.astype(F32) - minus.astype(F32)
    return jnp.dot(jnp.tile(compact, (reps, 1)), sel, precision=lax.Precision.HIGHEST)


def kernel(x_prompt, x_sample, cache_k, cache_v, state_pool, state_conv, norm_mix, w_in, q_norm,
           k_norm, attn_sinks, w_pool, pool_scale, w_out, norm_ffn, w_up, conv_w, conv_b, w_down):
    depth = w_in.shape[0]
    B, L, D = x_prompt.shape
    S, n_new, _ = x_sample.shape
    past_len = L
    assert L % PROMPT_TILE == 0 and PROMPT_TILE % CHUNK == 0 and PROMPT_TILE >= WINDOW
    assert w_down.shape[1] % FF_CHUNK == 0
    assert all(m.shape[1] % (BF16_SUBLANES * N_PREP) == 0 for m in (w_in, w_up, w_down))
    assert w_out.shape[1] == N_PREP * HEAD_DIM and LANES == N_KV_HEADS * HEAD_DIM

    tabs_p = _rope_tables(jnp.arange(L))
    tabs_s = _rope_tables(past_len + jnp.arange(n_new), reps=S)
    head_id = jnp.arange(MXU_COLS) // HEAD_DIM
    bd = jnp.where(head_id[:, None] == head_id[None, :], 1.0 / HEAD_DIM, 0.0).astype(BF16)
    q_scale = HEAD_DIM ** -0.5 * LOG2_E
    n_pool = state_pool.shape[2]
    assert max(POOL_WINDOWS) - 1 <= n_pool <= n_new
    n_conv = state_conv.shape[2]
    past = cache_k.shape[2]

    def dims_major(cache):
        return cache.transpose(0, 1, 3, 4, 2).reshape(depth, S, D_KV, past)

    stacked = dict(
        g_mix=norm_mix, w_in=w_in,
        q_gain=jnp.tile(q_norm * q_scale, (1, N_Q_HEADS)),
        k_gain=jnp.tile(k_norm, (1, N_KV_HEADS)),
        bd=bd, w_pool=w_pool, pool_scale=pool_scale,
        w_out=w_out, g_ffn=norm_ffn, w_up=w_up,
        conv_w=conv_w, conv_b=conv_b, w_down=w_down,
        cache_k=dims_major(cache_k), cache_v=dims_major(cache_v),
        pool_prev=state_pool, conv_prev=state_conv,
    )

    def tokens_major(state):
        return state.reshape(-1, N_KV_HEADS, HEAD_DIM, WINDOW).transpose(0, 3, 1, 2)

    yp, ys = x_prompt, x_sample
    outs = [[] for _ in range(8)]
    for i in range(depth):
        w = dict(stacked, layer=i, sinks=attn_sinks, n_pool=n_pool)
        (yp, k1, v1, u1, h1), (ys, k2, v2, u2, h2) = _layer_call(yp, tabs_p, ys, tabs_s, w)
        outs[0].append(tokens_major(k1))
        outs[1].append(tokens_major(v1))
        outs[2].append(u1)
        outs[3].append(h1)
        outs[4].append(tokens_major(k2))
        outs[5].append(tokens_major(v2))
        outs[6].append(u2)
        outs[7].append(h2)
    return (yp, ys) + tuple(jnp.stack(o) for o in outs)
```

```python
import functools
from typing import Any, NamedTuple

import jax
import jax.numpy as jnp
from jax import lax
from jax.experimental import pallas as pl
from jax.experimental.pallas import tpu as pltpu

F32 = jnp.float32
BF16 = jnp.bfloat16

CHUNK = 64
HEAD_DIM = 64
N_Q_HEADS = 8
N_KV_HEADS = 2
Q_PER_KV = N_Q_HEADS // N_KV_HEADS
D_ATTN = N_Q_HEADS * HEAD_DIM
D_KV = N_KV_HEADS * HEAD_DIM
WINDOW = 128
ROT_DIM = 16
ROPE_THETA = 500000.0
POOL_WINDOWS = (2, 4, 8, 16)
POOL_GROUP = 128
D_POOL = POOL_GROUP * len(POOL_WINDOWS)
POOL_HALO = 16
CONV_W = 3
CONV_HALO = 8
EPS = 1e-6
NEG_INF = -1e30
LOG2_E = 1.4426950408889634
NEG_LOG2_E = -LOG2_E

LANES = 128
MXU_COLS = 256
BF16_SUBLANES = 16
KV_PAD = CHUNK
VT_ROWS = D_KV + BF16_SUBLANES

PROMPT_TILE = 512
ATT_GROUP = 4
FF_CHUNK = 256
GATE_ROW_BLOCKS = 4
SAMPLE_FF_SPLIT = 1
DOWN_GROUP = 3
DOWN_LAG = 2
N_PREP = 16
VMEM_LIMIT_BYTES = 56 * 1024 * 1024


def _dot(a, b):
    return jnp.dot(a, b, preferred_element_type=F32)


def _dot_nt(a, b):
    return lax.dot_general(a, b, (((1,), (1,)), ((), ())), preferred_element_type=F32)


def _rms_rows(x, gain):
    ms = jnp.mean(x * x, axis=-1, keepdims=True)
    return x * lax.rsqrt(ms + EPS) * gain


def _head_rms(t, bd, gain):
    ms = _dot((t * t).astype(BF16), bd)
    return t * lax.rsqrt(ms + EPS) * gain


def _rope(t, cos, sin_a, sin_b):
    return (t * cos + pltpu.roll(t, LANES - ROT_DIM // 2, 1) * sin_a
            + pltpu.roll(t, ROT_DIM // 2, 1) * sin_b)


def _rope_coefficients(tab_ref):
    return [tab_ref[:, j * LANES:(j + 1) * LANES] for j in range(3)]


def _mixer_inputs(x, gmix, w_in, qg, kg, bd, cos, sin_a, sin_b):
    xn = _rms_rows(x, gmix).astype(BF16)
    h = _dot(xn, w_in)
    q_parts = []
    for j in range(D_ATTN // MXU_COLS):
        qb = _head_rms(h[:, j * MXU_COLS:(j + 1) * MXU_COLS], bd,
                       qg[:, j * MXU_COLS:(j + 1) * MXU_COLS])
        for l in range(MXU_COLS // LANES):
            q_parts.append(_rope(qb[:, l * LANES:(l + 1) * LANES], cos, sin_a, sin_b))
    q = jnp.concatenate(q_parts, axis=1)
    k = _head_rms(h[:, D_ATTN:D_ATTN + D_KV], bd[:D_KV, :D_KV], kg)
    k = _rope(k, cos, sin_a, sin_b)
    v = h[:, D_ATTN + D_KV:D_ATTN + 2 * D_KV]
    u = h[:, D_ATTN + 2 * D_KV:]
    return q, k, v, u


def _window_sum(a, w):
    s = 1
    while s < w:
        a = a + pltpu.roll(a, s, 0)
        s *= 2
    return a


def _conv_gate(hg, hv, cw_g, cw_v, cb_g, cb_v):
    cg = cb_g + hg[2] * cw_g[0:1] + hg[1] * cw_g[1:2] + hg[0] * cw_g[2:3]
    cv = cb_v + hv[2] * cw_v[0:1] + hv[1] * cw_v[1:2] + hv[0] * cw_v[2:3]
    return ((cg / (1.0 + jnp.exp2(cg * NEG_LOG2_E))) * cv).astype(BF16)


def _pair_heads(qcols):
    lane = lax.broadcasted_iota(jnp.int32, (1, LANES), 1)
    blocks = [qcols[:, b * LANES:(b + 1) * LANES] for b in range(D_ATTN // LANES)]
    per_block = LANES // HEAD_DIM
    out = []
    for hh in range(Q_PER_KV):
        a = blocks[hh // per_block]
        b = blocks[Q_PER_KV // per_block + hh // per_block]
        if hh % per_block == 0:
            out.append(jnp.where(lane < HEAD_DIM, a, pltpu.roll(b, HEAD_DIM, 1)))
        else:
            out.append(jnp.where(lane < HEAD_DIM, pltpu.roll(a, HEAD_DIM, 1), b))
    return jnp.concatenate(out, axis=1)


def _stage_weights(s, win_c, wout_c, wup_c, wdown_c, w_in_s, w_out_s, w_up_s, w_down_s):
    r_in = win_c.shape[0]
    r0 = pl.multiple_of(s * r_in, r_in)
    c = win_c[...]
    w_in_s[pl.ds(r0, r_in), :] = jnp.concatenate(
        [_pair_heads(c[:, :D_ATTN]), c[:, D_ATTN:]], axis=1).astype(BF16)
    w_up_s[pl.ds(r0, r_in), :] = wup_c[...].astype(BF16)
    r_dn = wdown_c.shape[0]
    w_down_s[pl.ds(pl.multiple_of(s * r_dn, r_dn), r_dn), :] = wdown_c[...].astype(BF16)
    dst = jnp.where(s < N_Q_HEADS, (s % Q_PER_KV) * N_KV_HEADS + s // Q_PER_KV, s)
    w_out_s[pl.ds(pl.multiple_of(dst * HEAD_DIM, HEAD_DIM), HEAD_DIM), :] = wout_c[...].astype(BF16)


def _prompt_kernel(sinks_ref, x_ref, tab_ref,
                   gmix_ref, win_c, qg_ref, kg_ref, bd_ref, wpool_ref, pscale_ref, wout_c,
                   gffn_ref, wup_c, cw_ref, cb_ref, wdown_c,
                   xs_ref, tab_s_ref, ck_ref, cv_ref, pprev_ref, cprev_ref,
                   y_ref, klast_ref, vlast_ref, ulast_ref, hlast_ref,
                   ys_ref, kout_ref, vout_ref, uout_ref, hout_ref,
                   w_in_s, w_out_s, w_up_s, w_down_s,
                   kfull, vfull, vt, ufull, hprev, mix_buf, ufull_s, mix_s,
                   *, layer, n_tiles, n_total):
    s = pl.program_id(0)
    sinks_ref = sinks_ref.at[layer]
    gmix_ref, qg_ref, kg_ref, pscale_ref, gffn_ref, cb_ref = (
        r.at[layer:layer + 1] for r in (gmix_ref, qg_ref, kg_ref, pscale_ref, gffn_ref, cb_ref))

    @pl.when(s < N_PREP)
    def _():
        _stage_weights(s, win_c, wout_c, wup_c, wdown_c, w_in_s, w_out_s, w_up_s, w_down_s)

    @pl.when((s >= N_PREP) & (s < N_PREP + n_total))
    def _():
        _prompt_body((s - N_PREP) % n_tiles, sinks_ref, x_ref, tab_ref,
                     gmix_ref, w_in_s, qg_ref, kg_ref, bd_ref, wpool_ref, pscale_ref, w_out_s,
                     gffn_ref, w_up_s, cw_ref, cb_ref, w_down_s,
                     y_ref, klast_ref, vlast_ref, ulast_ref, hlast_ref,
                     kfull, vfull, vt, ufull, hprev, mix_buf)

    @pl.when(s == N_PREP + n_total)
    def _():
        _sample_step(sinks_ref, gmix_ref, w_in_s, qg_ref, kg_ref, bd_ref, wpool_ref, pscale_ref,
                     w_out_s, gffn_ref, w_up_s, cw_ref, cb_ref, w_down_s,
                     _SampleRefs(xs_ref, tab_s_ref, ck_ref, cv_ref, pprev_ref,
                                 cprev_ref, ys_ref, kout_ref, vout_ref, uout_ref, hout_ref,
                                 ufull_s, mix_s))


def _prompt_body(i, sinks_ref, x_ref, tab_ref,
                 gmix_ref, win_ref, qg_ref, kg_ref, bd_ref, wpool_ref, pscale_ref, wout_ref,
                 gffn_ref, wup_ref, cw_ref, cb_ref, wdown_ref,
                 y_ref, klast_ref, vlast_ref, ulast_ref, hlast_ref,
                 kfull, vfull, vt, ufull, hprev, mix_buf):
    T = x_ref.shape[0]
    d_ff = wdown_ref.shape[0]
    n_chunks = T // CHUNK

    @pl.when(i == 0)
    def _():
        kfull[0:WINDOW, :] = jnp.zeros((WINDOW, D_KV), BF16)
        kfull[WINDOW + T:, :] = jnp.zeros((KV_PAD, D_KV), BF16)
        vfull[0:WINDOW, :] = jnp.zeros((WINDOW, D_KV), BF16)
        vfull[WINDOW + T:, :] = jnp.zeros((KV_PAD, D_KV), BF16)
        for c in range(2):
            vt[c, D_KV:, :] = jnp.ones((VT_ROWS - D_KV, vt.shape[2]), BF16)
        ufull[0:POOL_HALO, :] = jnp.zeros((POOL_HALO, D_POOL), F32)
        hprev[...] = jnp.zeros(hprev.shape, F32)

    x = x_ref[...]
    q, k, v, u = _mixer_inputs(x, gmix_ref[...], win_ref[...], qg_ref[...], kg_ref[...],
                               bd_ref[...], *_rope_coefficients(tab_ref))
    klast_ref[...] = k[T - WINDOW:, :].T
    vlast_ref[...] = v[T - WINDOW:, :].T
    ulast_ref[...] = u[T - ulast_ref.shape[0]:, :]
    kfull[WINDOW:WINDOW + T, :] = k.astype(BF16)
    vfull[WINDOW:WINDOW + T, :] = v.astype(BF16)
    ufull[POOL_HALO:POOL_HALO + T, :] = u

    for c in range(2):
        vrows = vfull[c * CHUNK:c * CHUNK + T + WINDOW, :]
        vt[c, 0:D_KV, :] = vrows.astype(F32).T.astype(BF16)
    band = WINDOW + CHUNK
    n_q = N_Q_HEADS * CHUNK
    lane = lax.broadcasted_iota(jnp.int32, (1, D_ATTN), 1)
    head_of_col = lax.broadcasted_iota(jnp.int32, (1, n_q), 1) // CHUNK
    key_row = lax.broadcasted_iota(jnp.int32, (band, 1), 0)
    q16 = q.astype(BF16)
    q_of_group = [jnp.where((lane // HEAD_DIM) % N_KV_HEADS == g, q16, jnp.zeros_like(q16))
                  for g in range(N_KV_HEADS)]
    sink_row = jnp.full((1, n_q), sinks_ref[0], F32)
    for h8 in range(1, N_Q_HEADS):
        sink_row = jnp.where(head_of_col == h8, sinks_ref[h8], sink_row)
    sink_row = sink_row * LOG2_E
    half = Q_PER_KV * CHUNK
    for j0 in range(0, n_chunks, ATT_GROUP):
        chunk_ids = range(j0, min(j0 + ATT_GROUP, n_chunks))
        scores = []
        for j in chunk_ids:
            r0 = j * CHUNK
            qs = jnp.concatenate(
                [q_of_group[g][r0:r0 + CHUNK, hh * LANES:(hh + 1) * LANES]
                 for g in range(N_KV_HEADS) for hh in range(Q_PER_KV)], axis=0)
            scores.append(_dot_nt(kfull[r0:r0 + band, :], qs))
        probs = []
        for j, s in zip(chunk_ids, scores):
            r0 = j * CHUNK
            if r0 < WINDOW:
                first_valid = jnp.where(i == 0, WINDOW - r0, 0)
                s = jnp.where(key_row >= first_valid, s, NEG_INF)
            m = jnp.maximum(jnp.max(s, axis=0, keepdims=True), sink_row)
            probs.append((jnp.exp2(s - m).astype(BF16), jnp.exp2(sink_row - m)))
        outs_t = []
        for j, (e, sink_e) in zip(chunk_ids, probs):
            vtb = vt[j % 2, :, (j // 2) * LANES:(j // 2) * LANES + band]
            oa = _dot(vtb, e)
            inv_den = 1.0 / (oa[D_KV:D_KV + 1] + sink_e)
            outs_t.append(jnp.concatenate(
                [oa[g * HEAD_DIM:(g + 1) * HEAD_DIM, g * half:(g + 1) * half]
                 * inv_den[:, g * half:(g + 1) * half] for g in range(N_KV_HEADS)],
                axis=0))
        for j, ot in zip(chunk_ids, outs_t):
            r0 = j * CHUNK
            o = ot.T
            for hh in range(Q_PER_KV):
                mix_buf[r0:r0 + CHUNK, hh * LANES:(hh + 1) * LANES] = (
                    o[hh * CHUNK:(hh + 1) * CHUNK].astype(BF16))

    uf = ufull[...]
    pos = i * T + lax.broadcasted_iota(jnp.int32, (T, 1), 0)
    for gi, w in enumerate(POOL_WINDOWS):
        sl = slice(gi * POOL_GROUP, (gi + 1) * POOL_GROUP)
        tsum = _window_sum(uf[:, sl], w)[POOL_HALO:]
        cnt = jnp.minimum(pos + 1, w).astype(F32)
        d = (tsum / cnt - u[:, sl]).astype(BF16)
        pool = _dot(d, wpool_ref[gi].astype(BF16)) * pscale_ref[:, sl]
        mix_buf[:, D_ATTN + gi * POOL_GROUP:D_ATTN + (gi + 1) * POOL_GROUP] = pool.astype(BF16)

    kfull[0:WINDOW, :] = kfull[T:T + WINDOW, :]
    vfull[0:WINDOW, :] = vfull[T:T + WINDOW, :]
    ufull[0:POOL_HALO, :] = u[T - POOL_HALO:, :]

    x1 = x_ref[...] + _dot(mix_buf[...], wout_ref[...])

    xn2 = _rms_rows(x1, gffn_ref[...]).astype(BF16)
    F = FF_CHUNK
    y_ref[...] = x1
    k_done, acts = 0, []
    row_in_group = lax.broadcasted_iota(jnp.int32, (1, CONV_HALO, 1), 1)
    for c in range(d_ff // F):
        cols = (slice(c * F, (c + 1) * F), slice(d_ff + c * F, d_ff + (c + 1) * F))
        hs, halos = [], []
        for cs in cols:
            h = _dot(xn2, wup_ref[:, cs])
            halos.append(hprev[:, cs])
            hlast_ref[:, cs] = h[T - hlast_ref.shape[0]:, :]
            hprev[:, cs] = h[T - CONV_HALO:, :]
            hs.append(h)
        R = T // GATE_ROW_BLOCKS
        blocks = []
        for rb in range(GATE_ROW_BLOCKS):
            taps = []
            for h, halo in zip(hs, halos):
                hb = h[rb * R:(rb + 1) * R]
                before = halo if rb == 0 else h[rb * R - CONV_HALO:rb * R]
                groups = jnp.concatenate([before, hb], axis=0).reshape(R // CONV_HALO + 1, CONV_HALO, F)
                shifted = []
                for sh in (1, 2):
                    r = pltpu.roll(groups, sh, 1)
                    shifted.append(jnp.where(row_in_group < sh, r[:-1], r[1:]).reshape(R, F))
                taps.append((hb, shifted[0], shifted[1]))
            blocks.append(_conv_gate(taps[0], taps[1], cw_ref[:, cols[0]], cw_ref[:, cols[1]],
                                     cb_ref[:, cols[0]], cb_ref[:, cols[1]]))
        acts.append(jnp.concatenate(blocks, axis=0))
        c_done = c - DOWN_LAG
        if c_done >= 0 and (c_done + 1) % DOWN_GROUP == 0:
            ks = slice((c_done + 1 - DOWN_GROUP) * F, (c_done + 1) * F)
            group = jnp.concatenate(acts[c_done + 1 - DOWN_GROUP:c_done + 1], axis=1)
            y_ref[...] += _dot(group, wdown_ref[ks, :])
            k_done = (c_done + 1) * F
    y_ref[...] += _dot(jnp.concatenate(acts[k_done // F:], axis=1), wdown_ref[k_done:, :])


def _const_spec(shape):
    nd = len(shape)
    return pl.BlockSpec(shape, lambda *_: (0,) * nd)


def _chunk_spec(stacked, layer):
    rows = stacked.shape[1] // N_PREP
    return pl.BlockSpec((None, rows, stacked.shape[2]),
                        lambda s: (layer, jnp.minimum(s, N_PREP - 1), 0))


def _layer_spec(stacked, layer, resident=False):
    nd = stacked.ndim - 1
    return pl.BlockSpec((None,) + stacked.shape[1:], lambda *_: (layer,) + (0,) * nd,
                        pipeline_mode=pl.Buffered(1) if resident else None)


def _weight_specs(w):
    layer = w["layer"]
    return [
        _const_spec(w["g_mix"].shape), _chunk_spec(w["w_in"], layer),
        _const_spec(w["q_gain"].shape), _const_spec(w["k_gain"].shape), _const_spec(w["bd"].shape),
        _layer_spec(w["w_pool"], layer), _const_spec(w["pool_scale"].shape), _chunk_spec(w["w_out"], layer),
        _const_spec(w["g_ffn"].shape), _chunk_spec(w["w_up"], layer),
        _layer_spec(w["conv_w"], layer), _const_spec(w["conv_b"].shape), _chunk_spec(w["w_down"], layer),
    ]


def _weight_args(w):
    return (w["g_mix"], w["w_in"], w["q_gain"], w["k_gain"], w["bd"], w["w_pool"],
            w["pool_scale"], w["w_out"], w["g_ffn"], w["w_up"], w["conv_w"], w["conv_b"], w["w_down"])


def _weight_scratch(w):
    return [pltpu.VMEM(w[name].shape[1:], BF16) for name in ("w_in", "w_out", "w_up", "w_down")]


def _resident_spec(shape):
    nd = len(shape)
    return pl.BlockSpec(shape, lambda *_: (0,) * nd, pipeline_mode=pl.Buffered(1))


def _layer_call(x, tabs, xs, tabs_s, w):
    B, L, D = x.shape
    S, n_new, _ = xs.shape
    M = S * n_new
    T = PROMPT_TILE
    d_ff = w["w_down"].shape[1]
    n_tiles = L // T
    n_total = B * n_tiles
    layer = w["layer"]
    n_pool, n_conv = w["n_pool"], w["conv_prev"].shape[2]
    tile_of = lambda s: jnp.clip(s - N_PREP, 0, n_total - 1)
    tile_spec = pl.BlockSpec((None, T, D), lambda s: (tile_of(s) // n_tiles, tile_of(s) % n_tiles, 0))
    tab_spec = pl.BlockSpec((T, tabs.shape[1]), lambda s: (tile_of(s) % n_tiles, 0))

    def last_spec(rows, cols):
        return pl.BlockSpec((None, rows, cols), lambda s: (tile_of(s) // n_tiles, 0, 0))

    sample_acts = (xs.reshape(M, D), tabs_s)
    sample_state = (w["cache_k"], w["cache_v"], w["pool_prev"], w["conv_prev"])
    in_specs = ([pl.BlockSpec(memory_space=pltpu.SMEM), tile_spec, tab_spec]
                + _weight_specs(w) + [_resident_spec(a.shape) for a in sample_acts]
                + [_layer_spec(a, layer, resident=True) for a in sample_state])
    sample_out = (
        jax.ShapeDtypeStruct((M, D), F32),
        jax.ShapeDtypeStruct((S, D_KV, WINDOW), F32),
        jax.ShapeDtypeStruct((S, D_KV, WINDOW), F32),
        jax.ShapeDtypeStruct((S, n_pool, D_POOL), F32),
        jax.ShapeDtypeStruct((S, n_conv, 2 * d_ff), F32),
    )
    out_shape = (
        jax.ShapeDtypeStruct((B, L, D), F32),
        jax.ShapeDtypeStruct((B, D_KV, WINDOW), F32),
        jax.ShapeDtypeStruct((B, D_KV, WINDOW), F32),
        jax.ShapeDtypeStruct((B, n_pool, D_POOL), F32),
        jax.ShapeDtypeStruct((B, n_conv, 2 * d_ff), F32),
    ) + sample_out
    out_specs = (tile_spec, last_spec(D_KV, WINDOW), last_spec(D_KV, WINDOW),
                 last_spec(n_pool, D_POOL), last_spec(n_conv, 2 * d_ff)
                 ) + tuple(_resident_spec(o.shape) for o in sample_out)
    scratch = _weight_scratch(w) + [
        pltpu.VMEM((WINDOW + T + KV_PAD, D_KV), BF16),
        pltpu.VMEM((WINDOW + T + KV_PAD, D_KV), BF16),
        pltpu.VMEM((2, VT_ROWS, T + WINDOW), BF16),
        pltpu.VMEM((POOL_HALO + T, D_POOL), F32),
        pltpu.VMEM((CONV_HALO, 2 * d_ff), F32),
        pltpu.VMEM((T, D_ATTN + D_POOL), BF16),
        pltpu.VMEM((2 * M, D_POOL), F32),
        pltpu.VMEM((M, D_ATTN + D_POOL), BF16),
    ]
    outs = pl.pallas_call(
        functools.partial(_prompt_kernel, layer=layer, n_tiles=n_tiles, n_total=n_total),
        out_shape=out_shape,
        grid=(N_PREP + n_total + 1,),
        in_specs=in_specs,
        out_specs=out_specs,
        scratch_shapes=scratch,
        name="layer",
        compiler_params=pltpu.CompilerParams(
            dimension_semantics=("arbitrary",),
            vmem_limit_bytes=VMEM_LIMIT_BYTES),
    )(w["sinks"], x, tabs, *_weight_args(w), *sample_acts, *sample_state)
    ys, ko, vo, uo, ho = outs[5:]
    return outs[:5], (ys.reshape(S, n_new, D), ko, vo, uo, ho)


class _SampleRefs(NamedTuple):
    x_ref: Any
    tab_ref: Any
    ck_ref: Any
    cv_ref: Any
    pprev_ref: Any
    cprev_ref: Any
    y_ref: Any
    kout_ref: Any
    vout_ref: Any
    uout_ref: Any
    hout_ref: Any
    ufull: Any
    mix_buf: Any


def _sample_mixer(q, k, v, u, sinks_ref, wpool_ref, pscale_ref, sample):
    ck_ref, cv_ref, mix_buf, ufull = sample.ck_ref, sample.cv_ref, sample.mix_buf, sample.ufull
    M = q.shape[0]
    n_streams, past = ck_ref.shape[0], ck_ref.shape[2]
    n_new = M // n_streams
    n_prev = sample.pprev_ref.shape[1]
    uout_ref = sample.uout_ref
    uout_ref[...] = u.reshape(n_streams, n_new, u.shape[1])[:, n_new - uout_ref.shape[1]:, :]

    sink_cols = [jnp.concatenate(
        [jnp.full((n_new, 1), sinks_ref[g * Q_PER_KV + hh], F32) for hh in range(Q_PER_KV)], axis=0)
        * LOG2_E for g in range(N_KV_HEADS)]
    scores, values = [], []
    for s_ in range(n_streams):
        rows = slice(s_ * n_new, (s_ + 1) * n_new)
        keys = jnp.concatenate([ck_ref[s_].T, k[rows]], axis=0)
        vals = jnp.concatenate([cv_ref[s_].T, v[rows]], axis=0)
        sample.kout_ref[s_] = keys[past + n_new - WINDOW:, :].T
        sample.vout_ref[s_] = vals[past + n_new - WINDOW:, :].T
        keys16 = keys.astype(BF16)
        values.append(vals.astype(BF16))
        for g in range(N_KV_HEADS):
            qs = jnp.concatenate(
                [q[rows, hh * LANES + g * HEAD_DIM:hh * LANES + (g + 1) * HEAD_DIM]
                 for hh in range(Q_PER_KV)], axis=0)
            scores.append(_dot_nt(qs, keys16[:, g * HEAD_DIM:(g + 1) * HEAD_DIM]))
        base = s_ * 2 * n_new
        ufull[base:base + n_new - n_prev, :] = jnp.zeros((n_new - n_prev, ufull.shape[1]), F32)
        ufull[base + n_new - n_prev:base + n_new, :] = sample.pprev_ref[s_]
        ufull[base + n_new:base + 2 * n_new, :] = u[rows]
    probs = []
    for j, sc in enumerate(scores):
        sink_col = sink_cols[j % N_KV_HEADS]
        m = jnp.maximum(jnp.max(sc, axis=-1, keepdims=True), sink_col)
        e = jnp.exp2(sc - m)
        den = jnp.sum(e, axis=-1, keepdims=True) + jnp.exp2(sink_col - m)
        probs.append((e.astype(BF16), 1.0 / den))
    for s_ in range(n_streams):
        rows = slice(s_ * n_new, (s_ + 1) * n_new)
        outs = []
        for g in range(N_KV_HEADS):
            e16, inv_den = probs[s_ * N_KV_HEADS + g]
            outs.append(_dot(e16, values[s_][:, g * HEAD_DIM:(g + 1) * HEAD_DIM]) * inv_den)
        for hh in range(Q_PER_KV):
            pair = jnp.concatenate([o[hh * n_new:(hh + 1) * n_new] for o in outs], axis=1)
            mix_buf[rows, hh * LANES:(hh + 1) * LANES] = pair.astype(BF16)

    uf = ufull[...]
    for gi, w in enumerate(POOL_WINDOWS):
        sl = slice(gi * POOL_GROUP, (gi + 1) * POOL_GROUP)
        ws = _window_sum(uf[:, sl], w)
        tsum = jnp.concatenate(
            [ws[s_ * 2 * n_new + n_new:(s_ + 1) * 2 * n_new] for s_ in range(n_streams)], axis=0)
        d = (tsum / float(w) - u[:, sl]).astype(BF16)
        pool = _dot(d, wpool_ref[gi].astype(BF16)) * pscale_ref[:, sl]
        mix_buf[:, D_ATTN + gi * POOL_GROUP:D_ATTN + (gi + 1) * POOL_GROUP] = pool.astype(BF16)


def _sample_taps(h, cs, sample):
    cprev_ref, hout_ref = sample.cprev_ref, sample.hout_ref
    M, F = h.shape
    n_streams, n_conv = cprev_ref.shape[0], cprev_ref.shape[1]
    n_new = M // n_streams
    t_in_stream = lax.broadcasted_iota(jnp.int32, (M, 1), 0) % n_new
    hout_ref[:, :, cs] = h.reshape(n_streams, n_new, F)[:, n_new - hout_ref.shape[1]:, :]
    prev = [jnp.broadcast_to(cprev_ref[:, j:j + 1, cs], (n_streams, n_new, F)).reshape(M, F)
            for j in range(n_conv)]
    return (h, jnp.where(t_in_stream >= 1, pltpu.roll(h, 1, 0), prev[1]),
            jnp.where(t_in_stream >= 2, pltpu.roll(h, 2, 0),
                      jnp.where(t_in_stream == 0, prev[0], prev[1])))


def _sample_step(sinks_ref, gmix_ref, win_ref, qg_ref, kg_ref, bd_ref, wpool_ref, pscale_ref,
                 wout_ref, gffn_ref, wup_ref, cw_ref, cb_ref, wdown_ref, sample):
    d_ff = wdown_ref.shape[0]
    x = sample.x_ref[...]
    q, k, v, u = _mixer_inputs(x, gmix_ref[...], win_ref[...], qg_ref[...], kg_ref[...],
                               bd_ref[...], *_rope_coefficients(sample.tab_ref))
    _sample_mixer(q.astype(BF16), k, v, u, sinks_ref, wpool_ref, pscale_ref, sample)
    x1 = x + _dot(sample.mix_buf[...], wout_ref[...])
    xn2 = _rms_rows(x1, gffn_ref[...]).astype(BF16)
    acc = x1
    F = d_ff // SAMPLE_FF_SPLIT
    for c in range(SAMPLE_FF_SPLIT):
        cg = slice(c * F, (c + 1) * F)
        cv = slice(d_ff + c * F, d_ff + (c + 1) * F)
        taps_g = _sample_taps(_dot(xn2, wup_ref[:, cg]), cg, sample)
        taps_v = _sample_taps(_dot(xn2, wup_ref[:, cv]), cv, sample)
        act = _conv_gate(taps_g, taps_v, cw_ref[:, cg], cw_ref[:, cv], cb_ref[:, cg], cb_ref[:, cv])
        acc = acc + _dot(act, wdown_ref[cg, :])
    sample.y_ref[...] = acc


def _rope_tables(pos, reps=1):
    half = ROT_DIM // 2
    inv = ROPE_THETA ** (-jnp.arange(0, ROT_DIM, 2, dtype=F32) / ROT_DIM)
    ang = pos.astype(F32)[:, None] * inv[None, :]
    compact = jnp.concatenate([jnp.cos(ang), jnp.sin(ang), jnp.ones((pos.shape[0], 1), F32)], axis=1)
    src = jnp.arange(2 * half + 1)[:, None]
    lane = jnp.arange(3 * LANES)[None, :]
    table, dim = lane // LANES, lane % HEAD_DIM
    freq, rotated, low = dim % half, dim < ROT_DIM, dim < half
    plus = (((table == 0) & rotated & (src == freq))
            | ((table == 0) & ~rotated & (src == 2 * half))
            | ((table == 2) & rotated & ~low & (src == half + freq)))
    minus = (table == 1) & low & (src == half + freq)
    sel = plus.astype(F32) - minus.astype(F32)
    return jnp.dot(jnp.tile(compact, (reps, 1)), sel, precision=lax.Precision.HIGHEST)


def kernel(x_prompt, x_sample, cache_k, cache_v, state_pool, state_conv, norm_mix, w_in, q_norm,
           k_norm, attn_sinks, w_pool, pool_scale, w_out, norm_ffn, w_up, conv_w, conv_b, w_down):
    depth = w_in.shape[0]
    B, L, D = x_prompt.shape
    S, n_new, _ = x_sample.shape
    past_len = L
    assert L % PROMPT_TILE == 0 and PROMPT_TILE % CHUNK == 0 and PROMPT_TILE >= WINDOW
    assert w_down.shape[1] % FF_CHUNK == 0
    assert all(m.shape[1] % (BF16_SUBLANES * N_PREP) == 0 for m in (w_in, w_up, w_down))
    assert w_out.shape[1] == N_PREP * HEAD_DIM and LANES == N_KV_HEADS * HEAD_DIM

    tabs_p = _rope_tables(jnp.arange(L))
    tabs_s = _rope_tables(past_len + jnp.arange(n_new), reps=S)
    head_id = jnp.arange(MXU_COLS) // HEAD_DIM
    bd = jnp.where(head_id[:, None] == head_id[None, :], 1.0 / HEAD_DIM, 0.0).astype(BF16)
    q_scale = HEAD_DIM ** -0.5 * LOG2_E
    n_pool = state_pool.shape[2]
    assert max(POOL_WINDOWS) - 1 <= n_pool <= n_new
    n_conv = state_conv.shape[2]
    past = cache_k.shape[2]

    def dims_major(cache):
        return cache.transpose(0, 1, 3, 4, 2).reshape(depth, S, D_KV, past)

    stacked = dict(
        g_mix=norm_mix, w_in=w_in,
        q_gain=jnp.tile(q_norm * q_scale, (1, N_Q_HEADS)),
        k_gain=jnp.tile(k_norm, (1, N_KV_HEADS)),
        bd=bd, w_pool=w_pool, pool_scale=pool_scale,
        w_out=w_out, g_ffn=norm_ffn, w_up=w_up,
        conv_w=conv_w, conv_b=conv_b, w_down=w_down,
        cache_k=dims_major(cache_k), cache_v=dims_major(cache_v),
        pool_prev=state_pool, conv_prev=state_conv,
    )

    def tokens_major(state):
        return state.reshape(-1, N_KV_HEADS, HEAD_DIM, WINDOW).transpose(0, 3, 1, 2)

    yp, ys = x_prompt, x_sample
    outs = [[] for _ in range(8)]
    for i in range(depth):
        w = dict(stacked, layer=i, sinks=attn_sinks, n_pool=n_pool)
        (yp, k1, v1, u1, h1), (ys, k2, v2, u2, h2) = _layer_call(yp, tabs_p, ys, tabs_s, w)
        outs[0].append(tokens_major(k1))
        outs[1].append(tokens_major(v1))
        outs[2].append(u1)
        outs[3].append(h1)
        outs[4].append(tokens_major(k2))
        outs[5].append(tokens_major(v2))
        outs[6].append(u2)
        outs[7].append(h2)
    return (yp, ys) + tuple(jnp.stack(o) for o in outs)
```
